```python
import math
import jax
import jax.numpy as jnp
from jax import lax
import numpy as np

D_MODEL = 1024
BATCH = 8
SEQ = 4096
DEPTH = 4

HEAD_DIM = 64
N_MIXERS = 4
MIX_W = D_MODEL // 2
Q_BLOCK = 128
EPS = 1e-6
NEG_INF = -1e30
CONV_K = 3
NSA_HEADS = MIX_W // HEAD_DIM
NSA_KV_HEADS = 2
CMP_BLOCK = 32
CMP_STRIDE = 16
CMP_HIDDEN = 256
SEL_BLOCK = 64
SEL_TOPN = 8
WINDOW = 512
FORCED_SCORE = 1e4
SGU_CHUNK = 128
SGU_GROUPS = 8
DSA_HEADS = MIX_W // HEAD_DIM
IDX_HEADS = 8
IDX_DIM = 64
DSA_TOPK_MAX = 256
N_BUCKETS = 32
MAX_DISTANCE = 1024
REL_HEADS = NSA_HEADS + DSA_HEADS
PEER_HEADS = 8
PEER_KEYS = 128
PEER_QDIM = 128
PEER_TOPK = 16
N_EXPERTS = PEER_KEYS * PEER_KEYS
PEER_CHUNK = 128
PLE_DIM = 256

SPLIT_WIDTHS = (
    MIX_W, MIX_W, MIX_W,
    NSA_HEADS * HEAD_DIM,
    6 * NSA_KV_HEADS * HEAD_DIM,
    3 * NSA_HEADS,
    2 * MIX_W,
    DSA_HEADS * HEAD_DIM, HEAD_DIM, HEAD_DIM,
    IDX_HEADS * IDX_DIM, IDX_DIM, IDX_HEADS,
)
IN_WIDTH = sum(SPLIT_WIDTHS)

kernel_name = "hybrid_gated_nsa_dsa_sgu_conv_peer"


def rmsnorm(x, g):
    xf = x.astype(jnp.float32)
    xf = xf * lax.rsqrt(jnp.mean(xf * xf, axis=-1, keepdims=True) + EPS)
    return (xf * g.astype(jnp.float32)).astype(x.dtype)


def layernorm(x, g):
    xf = x.astype(jnp.float32)
    xf = xf - jnp.mean(xf, axis=-1, keepdims=True)
    xf = xf * lax.rsqrt(jnp.mean(xf * xf, axis=-1, keepdims=True) + EPS)
    return (xf * g.astype(jnp.float32)).astype(x.dtype)


def rel_bucket(dist):
    n = jnp.maximum(dist, 0)
    max_exact = N_BUCKETS // 2
    nf = jnp.maximum(n, 1).astype(jnp.float32)
    large = max_exact + (jnp.log(nf / max_exact) / math.log(MAX_DISTANCE / max_exact)
                         * (N_BUCKETS - max_exact)).astype(jnp.int32)
    return jnp.where(n < max_exact, n, jnp.minimum(large, N_BUCKETS - 1))


def masked_softmax(logits, mask, axis):
    lf = jnp.where(mask, logits.astype(jnp.float32), NEG_INF)
    m = jnp.max(lf, axis=axis, keepdims=True)
    e = jnp.where(mask, jnp.exp(lf - m), 0.0)
    return e / jnp.maximum(jnp.sum(e, axis=axis, keepdims=True), 1e-30)


def short_conv_mixer(b_gate, c_gate, xv, conv_w):
    S = xv.shape[1]
    bx = c_gate * xv
    xp = jnp.pad(bx, ((0, 0), (CONV_K - 1, 0), (0, 0)))
    conv = sum(conv_w[k] * xp[:, k:k + S] for k in range(CONV_K))
    return b_gate * conv


def nsa_mixer(q, kv, gate_logits, cmp_pe, cmp_w1, cmp_w2, q_gain, k_gain, rel_tbl):
    B, S = q.shape[:2]
    H, Hkv, dh = NSA_HEADS, NSA_KV_HEADS, HEAD_DIM
    G = H // Hkv
    q = rmsnorm(q, q_gain) * (dh ** -0.5)
    k_c, v_c, k_s, v_s, k_w, v_w = (kv[:, :, i] for i in range(6))
    k_s = rmsnorm(k_s, k_gain[1])
    k_w = rmsnorm(k_w, k_gain[2])

    n_cmp = (S - CMP_BLOCK) // CMP_STRIDE + 1
    cmp_idx = np.arange(n_cmp)[:, None] * CMP_STRIDE + np.arange(CMP_BLOCK)[None, :]

    def compress(t, j):
        blk = t[:, cmp_idx] + cmp_pe[j][None, None, :, None, :]
        blk = blk.transpose(0, 1, 3, 2, 4).reshape(B, n_cmp, Hkv, CMP_BLOCK * dh)
        return jax.nn.gelu(blk @ cmp_w1[j]) @ cmp_w2[j]

    k_c = rmsnorm(compress(k_c, 0), k_gain[0])
    v_c = compress(v_c, 1)
    cmp_end = jnp.asarray(cmp_idx[:, -1], jnp.int32)

    n_sel = S // SEL_BLOCK
    sel_start = np.arange(n_sel) * SEL_BLOCK
    cmp_start = cmp_idx[:, 0]
    overlap = jnp.asarray((cmp_start[:, None] < sel_start[None, :] + SEL_BLOCK)
                          & (cmp_start[:, None] + CMP_BLOCK > sel_start[None, :]), jnp.float32)
    ks_blk = k_s.reshape(B, n_sel, SEL_BLOCK, Hkv, dh).transpose(0, 3, 1, 2, 4)
    vs_blk = v_s.reshape(B, n_sel, SEL_BLOCK, Hkv, dh).transpose(0, 3, 1, 2, 4)
    top_n = min(SEL_TOPN, n_sel)

    kw_p = jnp.pad(k_w, ((0, 0), (WINDOW, 0), (0, 0), (0, 0)))
    vw_p = jnp.pad(v_w, ((0, 0), (WINDOW, 0), (0, 0), (0, 0)))

    tbl = rel_tbl[:, :NSA_HEADS].reshape(N_BUCKETS, Hkv, G)
    b_idx = jnp.arange(B)[:, None, None]
    g_idx = jnp.arange(Hkv)[None, :, None]

    def block(i):
        q0 = i * Q_BLOCK
        t = q0 + jnp.arange(Q_BLOCK)
        qb = lax.dynamic_slice_in_dim(q, q0, Q_BLOCK, axis=1).reshape(B, Q_BLOCK, Hkv, G, dh)
        gb = lax.dynamic_slice_in_dim(gate_logits, q0, Q_BLOCK, axis=1)

        dist_c = t[:, None] - cmp_end[None, :]
        lc = jnp.einsum('btgrd,bngd->bgrtn', qb, k_c)
        lc = lc + tbl[rel_bucket(dist_c)].transpose(2, 3, 0, 1)[None]
        pc = masked_softmax(lc, dist_c >= 0, -1)
        o_c = jnp.einsum('bgrtn,bngd->btgrd', pc, v_c)

        imp = jnp.einsum('bgrtn,nj->bgtj', pc, overlap)
        cur = t // SEL_BLOCK
        j = jnp.arange(n_sel)
        forced = (j[None, :] == 0) | (j[None, :] == cur[:, None])
        imp = jnp.where(forced, FORCED_SCORE, imp)
        imp = jnp.where(j[None, :] <= cur[:, None], imp, NEG_INF)
        _, sel = lax.top_k(imp, top_n)
        sel_f = sel.reshape(B, Hkv, Q_BLOCK * top_n)
        k_sel = ks_blk[b_idx, g_idx, sel_f].reshape(B, Hkv, Q_BLOCK, top_n, SEL_BLOCK, dh)
        v_sel = vs_blk[b_idx, g_idx, sel_f].reshape(B, Hkv, Q_BLOCK, top_n, SEL_BLOCK, dh)
        pos_s = sel[..., None] * SEL_BLOCK + jnp.arange(SEL_BLOCK)
        dist_s = t[None, None, :, None, None] - pos_s
        ls = jnp.einsum('btgrd,bgtnkd->bgrtnk', qb, k_sel)
        bias_s = tbl[rel_bucket(dist_s), jnp.arange(Hkv)[None, :, None, None, None]]
        ls = ls + jnp.moveaxis(bias_s, -1, 2)
        ps = masked_softmax(ls, (dist_s >= 0)[:, :, None], (-2, -1))
        o_s = jnp.einsum('bgrtnk,bgtnkd->btgrd', ps, v_sel)

        kwb = lax.dynamic_slice_in_dim(kw_p, q0, Q_BLOCK + WINDOW, axis=1)
        vwb = lax.dynamic_slice_in_dim(vw_p, q0, Q_BLOCK + WINDOW, axis=1)
        pos_w = q0 - WINDOW + jnp.arange(Q_BLOCK + WINDOW)
        dist_w = t[:, None] - pos_w[None, :]
        mask_w = (dist_w >= 0) & (dist_w < WINDOW) & (pos_w[None, :] >= 0)
        lw = jnp.einsum('btgrd,bsgd->bgrts', qb, kwb)
        lw = lw + tbl[rel_bucket(dist_w)].transpose(2, 3, 0, 1)[None]
        pw = masked_softmax(lw, mask_w, -1)
        o_w = jnp.einsum('bgrts,bsgd->btgrd', pw, vwb)

        g = jax.nn.sigmoid(gb.astype(jnp.float32)).reshape(B, Q_BLOCK, 3, Hkv, G)[..., None]
        o = g[:, :, 0] * o_c + g[:, :, 1] * o_s + g[:, :, 2] * o_w
        return o.reshape(B, Q_BLOCK, H * dh).astype(q.dtype)

    out = lax.map(block, jnp.arange(S // Q_BLOCK))
    return out.transpose(1, 0, 2, 3).reshape(B, S, H * dh)


def sgu_mixer(uv, ln_gain, w_s, b_s):
    B, S, _ = uv.shape
    uv = jax.nn.gelu(uv)
    u, v = jnp.split(uv, 2, axis=-1)
    v = layernorm(v, ln_gain).reshape(B, S // SGU_CHUNK, SGU_CHUNK, SGU_GROUPS, MIX_W // SGU_GROUPS)
    tri = jnp.tril(jnp.ones((SGU_CHUNK, SGU_CHUNK), dtype=bool))
    w = jnp.where(tri[None], w_s, 0)
    s = jnp.einsum('gts,bcsgd->bctgd', w, v) + b_s.T[:, :, None]
    return u * s.reshape(B, S, MIX_W)


def dsa_mixer(q, k, v, q_idx, k_idx, w_idx, q_gain, k_gain, rel_tbl):
    B, S = q.shape[:2]
    H, dh = DSA_HEADS, HEAD_DIM
    q = rmsnorm(q, q_gain) * (dh ** -0.5)
    k = rmsnorm(k, k_gain)
    w_idx = w_idx * (IDX_HEADS ** -0.5)
    top_k = min(DSA_TOPK_MAX, S // 4)
    tbl = rel_tbl[:, NSA_HEADS:]
    b_idx = jnp.arange(B)[:, None, None]
    key_pos = jnp.arange(S)

    def block(i):
        q0 = i * Q_BLOCK
        t = q0 + jnp.arange(Q_BLOCK)
        qb = lax.dynamic_slice_in_dim(q, q0, Q_BLOCK, axis=1)
        qib = lax.dynamic_slice_in_dim(q_idx, q0, Q_BLOCK, axis=1)
        wib = lax.dynamic_slice_in_dim(w_idx, q0, Q_BLOCK, axis=1)
        score = jnp.einsum('bth,bths->bts', wib,
                           jax.nn.relu(jnp.einsum('bthd,bsd->bths', qib, k_idx)))
        score = jnp.where(key_pos[None, None, :] <= t[None, :, None],
                          score.astype(jnp.float32), NEG_INF)
        _, sel = lax.top_k(score, top_k)
        k_sel = k[b_idx, sel]
        v_sel = v[b_idx, sel]
        dist = t[None, :, None] - sel
        logits = jnp.einsum('bthd,btkd->bhtk', qb, k_sel)
        logits = logits + jnp.moveaxis(tbl[rel_bucket(dist)], -1, 1)
        p = masked_softmax(logits, (dist >= 0)[:, None], -1)
        o = jnp.einsum('bhtk,btkd->bthd', p, v_sel)
        return o.reshape(B, Q_BLOCK, H * dh).astype(q.dtype)

    out = lax.map(block, jnp.arange(S // Q_BLOCK))
    return out.transpose(1, 0, 2, 3).reshape(B, S, H * dh)


def peer_ffn(h, wq, subkeys, u_tab, v_tab):
    B, S, D = h.shape
    T = B * S

    def chunk(hc):
        C = hc.shape[0]
        q = (hc @ wq).reshape(C, PEER_HEADS, 2, PEER_QDIM // 2)
        s = jnp.einsum('chpd,hpkd->chpk', q, subkeys).astype(jnp.float32)
        s_top, i_top = lax.top_k(s, PEER_TOPK)
        cand = s_top[:, :, 0, :, None] + s_top[:, :, 1, None, :]
        cand_idx = i_top[:, :, 0, :, None] * PEER_KEYS + i_top[:, :, 1, None, :]
        c_s, c_i = lax.top_k(cand.reshape(C, PEER_HEADS, PEER_TOPK * PEER_TOPK), PEER_TOPK)
        experts = jnp.take_along_axis(cand_idx.reshape(C, PEER_HEADS, -1), c_i, axis=-1)
        g = jax.nn.softmax(c_s, axis=-1)
        u = u_tab[experts]
        v = v_tab[experts]
        a = jax.nn.gelu(jnp.einsum('ched,cd->che', u, hc))
        return jnp.einsum('che,ched->cd', (g * a).astype(v.dtype), v)

    out = lax.map(chunk, h.reshape(T // PEER_CHUNK, PEER_CHUNK, D))
    return out.reshape(B, S, D)


def setup_inputs(seed: int = 0) -> dict:
    key = jax.random.key(seed)
    ks = jax.random.split(key, 32)
    L = DEPTH

    def nrm(k, shape, scale):
        return jax.random.normal(k, shape, jnp.float32) * scale

    def gain(k, shape):
        return 1.0 + 0.05 * jax.random.normal(k, shape, jnp.float32)

    return {
        "x": nrm(ks[0], (BATCH, SEQ, D_MODEL), 1.0),
        "p": nrm(ks[1], (DEPTH, BATCH, SEQ, PLE_DIM), 1.0),
        "rel_bias": nrm(ks[2], (N_BUCKETS, REL_HEADS), 0.5),
        "g_mix": gain(ks[3], (L, D_MODEL)),
        "w_in": nrm(ks[4], (L, D_MODEL, IN_WIDTH), D_MODEL ** -0.5),
        "conv_w": nrm(ks[5], (L, CONV_K, MIX_W), 0.5),
        "nsa_cmp_pe": nrm(ks[6], (L, 2, CMP_BLOCK, HEAD_DIM), 0.1),
        "nsa_cmp_w1": nrm(ks[7], (L, 2, CMP_BLOCK * HEAD_DIM, CMP_HIDDEN), (CMP_BLOCK * HEAD_DIM) ** -0.5),
        "nsa_cmp_w2": nrm(ks[8], (L, 2, CMP_HIDDEN, HEAD_DIM), CMP_HIDDEN ** -0.5),
        "nsa_q_gain": gain(ks[9], (L, HEAD_DIM)),
        "nsa_k_gain": gain(ks[10], (L, 3, HEAD_DIM)),
        "sgu_ln_gain": gain(ks[11], (L, MIX_W)),
        "sgu_w": nrm(ks[12], (L, SGU_GROUPS, SGU_CHUNK, SGU_CHUNK), 0.5 * SGU_CHUNK ** -0.5),
        "sgu_b": 1.0 + nrm(ks[13], (L, SGU_GROUPS, SGU_CHUNK), 0.02),
        "dsa_q_gain": gain(ks[14], (L, HEAD_DIM)),
        "dsa_k_gain": gain(ks[15], (L, HEAD_DIM)),
        "w_gate": nrm(ks[16], (L, D_MODEL, N_MIXERS * D_MODEL), D_MODEL ** -0.5),
        "w_branch": nrm(ks[17], (L, N_MIXERS * MIX_W, D_MODEL), MIX_W ** -0.5),
        "w_out": nrm(ks[18], (L, D_MODEL, D_MODEL), 0.5 * D_MODEL ** -0.5),
        "g_ffn": gain(ks[19], (L, D_MODEL)),
        "peer_wq": nrm(ks[20], (L, D_MODEL, PEER_HEADS * PEER_QDIM), D_MODEL ** -0.5),
        "peer_subkeys": nrm(ks[21], (L, PEER_HEADS, 2, PEER_KEYS, PEER_QDIM // 2), (PEER_QDIM // 2) ** -0.5),
        "peer_u": nrm(ks[22], (L, N_EXPERTS, D_MODEL), D_MODEL ** -0.5),
        "peer_v": nrm(ks[23], (L, N_EXPERTS, D_MODEL), 0.25),
        "g_ple": gain(ks[24], (L, D_MODEL)),
        "w_ple_gate": nrm(ks[25], (L, D_MODEL, D_MODEL), D_MODEL ** -0.5),
        "w_ple_proj": nrm(ks[26], (L, PLE_DIM, D_MODEL), PLE_DIM ** -0.5),
    }


def reference(x, p, rel_bias, g_mix, w_in, conv_w, nsa_cmp_pe, nsa_cmp_w1, nsa_cmp_w2,
              nsa_q_gain, nsa_k_gain, sgu_ln_gain, sgu_w, sgu_b, dsa_q_gain, dsa_k_gain,
              w_gate, w_branch, w_out, g_ffn, peer_wq, peer_subkeys, peer_u, peer_v,
              g_ple, w_ple_gate, w_ple_proj):
    B, S, D = x.shape
    split_points = [int(c) for c in np.cumsum(SPLIT_WIDTHS)[:-1]]
    for l in range(DEPTH):
        h = rmsnorm(x, g_mix[l])
        proj = h @ w_in[l]
        (a_b, a_c, a_x, n_q, n_kv, n_g, c_uv,
         d_q, d_k, d_v, d_qi, d_ki, d_wi) = jnp.split(proj, split_points, axis=-1)

        y_a = short_conv_mixer(a_b, a_c, a_x, conv_w[l])
        y_b = nsa_mixer(n_q.reshape(B, S, NSA_HEADS, HEAD_DIM),
                        n_kv.reshape(B, S, 6, NSA_KV_HEADS, HEAD_DIM),
                        n_g.reshape(B, S, 3, NSA_HEADS),
                        nsa_cmp_pe[l], nsa_cmp_w1[l], nsa_cmp_w2[l],
                        nsa_q_gain[l], nsa_k_gain[l], rel_bias)
        y_c = sgu_mixer(c_uv, sgu_ln_gain[l], sgu_w[l], sgu_b[l])
        y_d = dsa_mixer(d_q.reshape(B, S, DSA_HEADS, HEAD_DIM), d_k, d_v,
                        d_qi.reshape(B, S, IDX_HEADS, IDX_DIM), d_ki, d_wi,
                        dsa_q_gain[l], dsa_k_gain[l], rel_bias)

        z = jnp.concatenate([y_a, y_b, y_c, y_d], axis=-1).reshape(B, S, N_MIXERS, MIX_W)
        z = jnp.einsum('bsmc,mcd->bsmd', z, w_branch[l].reshape(N_MIXERS, MIX_W, D))
        gates = jax.nn.sigmoid(h @ w_gate[l]).reshape(B, S, N_MIXERS, D)
        merged = jnp.einsum('bsmd,bsmd->bsd', gates, z)
        x = x + merged @ w_out[l]

        x = x + peer_ffn(rmsnorm(x, g_ffn[l]), peer_wq[l], peer_subkeys[l], peer_u[l], peer_v[l])

        x = x + jax.nn.sigmoid(rmsnorm(x, g_ple[l]) @ w_ple_gate[l]) * (p[l] @ w_ple_proj[l])
    return x
```

```python
import functools
import math

import jax
import jax.numpy as jnp
import numpy as np
from jax import lax
from jax.experimental import pallas as pl
from jax.experimental.pallas import tpu as pltpu

F32 = jnp.float32
BF16 = jnp.bfloat16

D_MODEL = 1024
HEAD_DIM = 64
N_MIXERS = 4
MIX_W = D_MODEL // 2
Q_BLOCK = 128
EPS = 1e-6
NEG_INF = -1e30
CONV_K = 3
NSA_HEADS = MIX_W // HEAD_DIM
NSA_KV_HEADS = 2
CMP_BLOCK = 32
CMP_STRIDE = 16
CMP_HIDDEN = 256
SEL_BLOCK = 64
SEL_TOPN = 8
WINDOW = 512
FORCED_SCORE = 1e4
SGU_CHUNK = 128
SGU_GROUPS = 8
DSA_HEADS = MIX_W // HEAD_DIM
IDX_HEADS = 8
IDX_DIM = 64
DSA_TOPK_MAX = 256
N_BUCKETS = 32
MAX_DISTANCE = 1024
PEER_HEADS = 8
PEER_KEYS = 128
PEER_QDIM = 128
PEER_TOPK = 16
N_EXPERTS = PEER_KEYS * PEER_KEYS
PLE_DIM = 256

SPLIT_WIDTHS = (
    MIX_W, MIX_W, MIX_W,
    NSA_HEADS * HEAD_DIM,
    6 * NSA_KV_HEADS * HEAD_DIM,
    3 * NSA_HEADS,
    2 * MIX_W,
    DSA_HEADS * HEAD_DIM, HEAD_DIM, HEAD_DIM,
    IDX_HEADS * IDX_DIM, IDX_DIM, IDX_HEADS,
)
IN_WIDTH = sum(SPLIT_WIDTHS)
IN_WIDTH_PAD = 5120

VMEM_LIMIT_BYTES = 56 * 1024 * 1024


def _cparams(sem):
    return pltpu.CompilerParams(dimension_semantics=sem, vmem_limit_bytes=VMEM_LIMIT_BYTES)


def _rms(x, g):
    return x * lax.rsqrt(jnp.mean(x * x, axis=-1, keepdims=True) + EPS) * g


def _rms_matmul_body(x_ref, g_ref, w_ref, o_ref, h_ref):
    @pl.when(pl.program_id(1) == 0)
    def _():
        h_ref[...] = _rms(x_ref[...], g_ref[...]).astype(BF16)

    o_ref[...] = jnp.dot(h_ref[...], w_ref[...], preferred_element_type=F32)


def rms_matmul(x, g, w, tm, tn):
    T, D = x.shape
    N = w.shape[1]
    return pl.pallas_call(
        _rms_matmul_body,
        grid=(T // tm, N // tn),
        in_specs=[pl.BlockSpec((tm, D), lambda i, j: (i, 0)),
                  pl.BlockSpec((1, D), lambda i, j: (0, 0)),
                  pl.BlockSpec((D, tn), lambda i, j: (0, j))],
        out_specs=pl.BlockSpec((tm, tn), lambda i, j: (i, j)),
        out_shape=jax.ShapeDtypeStruct((T, N), F32),
        scratch_shapes=[pltpu.VMEM((tm, D), BF16)],
        compiler_params=_cparams(("parallel", "arbitrary")),
        name="rms_matmul",
    )(x, g, w)


def _merge_body(x_ref, g_ref, ya_ref, yb_ref, yc_ref, yd_ref, wg_ref, wb_ref, wo_ref, o_ref):
    x = x_ref[...]
    h = _rms(x, g_ref[...]).astype(BF16)
    merged = jnp.zeros(x.shape, F32)
    for m, y_ref in enumerate((ya_ref, yb_ref, yc_ref, yd_ref)):
        z = jnp.dot(y_ref[...].astype(BF16), wb_ref[m], preferred_element_type=F32)
        gate = jax.nn.sigmoid(jnp.dot(h, wg_ref[:, m * D_MODEL:(m + 1) * D_MODEL],
                                      preferred_element_type=F32))
        merged = merged + gate * z
    o_ref[...] = x + jnp.dot(merged.astype(BF16), wo_ref[...], preferred_element_type=F32)


def merge_mixers(x, g, ys, w_gate, w_branch, w_out, tm):
    T, D = x.shape
    const = lambda *shape: pl.BlockSpec(shape, lambda i: (0,) * len(shape), pipeline_mode=pl.Buffered(1))
    return pl.pallas_call(
        _merge_body,
        grid=(T // tm,),
        in_specs=[pl.BlockSpec((tm, D), lambda i: (i, 0)),
                  const(1, D)]
                 + [pl.BlockSpec((tm, MIX_W), lambda i: (i, 0)) for _ in range(N_MIXERS)]
                 + [const(D, N_MIXERS * D), const(N_MIXERS, MIX_W, D), const(D, D)],
        out_specs=pl.BlockSpec((tm, D), lambda i: (i, 0)),
        out_shape=jax.ShapeDtypeStruct((T, D), F32),
        compiler_params=_cparams(("parallel",)),
        name="merge_mixers",
    )(x, g, *ys, w_gate, w_branch, w_out)


def _ple_body(x_ref, g_ref, p_ref, wg_ref, wp_ref, o_ref):
    x = x_ref[...]
    h = _rms(x, g_ref[...]).astype(BF16)
    gate = jax.nn.sigmoid(jnp.dot(h, wg_ref[...], preferred_element_type=F32))
    proj = jnp.dot(p_ref[...].astype(BF16), wp_ref[...], preferred_element_type=F32)
    o_ref[...] = x + gate * proj


def ple_update(x, g, p, w_gate, w_proj, tm):
    T, D = x.shape
    const = lambda *shape: pl.BlockSpec(shape, lambda i: (0,) * len(shape), pipeline_mode=pl.Buffered(1))
    return pl.pallas_call(
        _ple_body,
        grid=(T // tm,),
        in_specs=[pl.BlockSpec((tm, D), lambda i: (i, 0)), const(1, D),
                  pl.BlockSpec((tm, PLE_DIM), lambda i: (i, 0)),
                  const(D, D), const(PLE_DIM, D)],
        out_specs=pl.BlockSpec((tm, D), lambda i: (i, 0)),
        out_shape=jax.ShapeDtypeStruct((T, D), F32),
        compiler_params=_cparams(("parallel",)),
        name="ple_update",
    )(x, g, p, w_gate, w_proj)


def _extract_top16(cur, tops_ref):
    m = None
    for r in range(PEER_TOPK):
        m = jnp.max(cur, axis=0, keepdims=True)
        if tops_ref is not None:
            tops_ref[r:r + 1, :] = m
        cur = jnp.where(cur == m, NEG_INF, cur)
    return m


_CAND_ROWS_J = (16, 8, 5, 4, 3, 2, 2, 2)


def _peer_topk_body(x_ref, g_ref, wq_ref, sk_ref, hn_ref, s1_ref, s2_ref, thr_ref, ln_ref,
                    t1_ref, t2_ref):
    hn = _rms(x_ref[...], g_ref[...]).astype(BF16)
    hn_ref[...] = hn
    q = jnp.dot(hn, wq_ref[...], preferred_element_type=F32).astype(BF16)
    half = PEER_QDIM // 2
    tn = x_ref.shape[0]
    j_iota = lax.broadcasted_iota(jnp.int32, (8, tn), 0)
    for h in range(PEER_HEADS):
        s = []
        for p in range(2):
            c0 = (2 * h + p) * half
            s.append(lax.dot_general(sk_ref[2 * h + p], q[:, c0:c0 + half],
                                     (((1,), (1,)), ((), ())), preferred_element_type=F32))
        th1 = _extract_top16(s[0], t1_ref)
        th2 = _extract_top16(s[1], t2_ref)
        a = t1_ref[...]
        b = t2_ref[...]
        b8 = b[0:8]
        pieces = [a[0:1] + b]
        for i in range(1, 8):
            c = a[i:i + 1] + b8
            if _CAND_ROWS_J[i] < 8:
                c = jnp.where(j_iota < _CAND_ROWS_J[i], c, NEG_INF)
            pieces.append(c)
        pieces.append(a[8:16] + b[0:1])
        cand = jnp.concatenate(pieces, axis=0)
        top = a[0:1] + b[0:1]
        theta = _extract_top16(cand, None)
        z = jnp.sum(jnp.where(cand >= theta, jnp.exp(cand - top), 0.0), axis=0, keepdims=True)
        s1_ref[h] = jnp.where(s[0] >= th1, s[0], NEG_INF)
        s2_ref[h] = jnp.where(s[1] >= th2, s[1], NEG_INF)
        thr_ref[h:h + 1, :] = theta
        ln_ref[h:h + 1, :] = top + jnp.log(z)


def peer_topk(x, g, wq, subkeys, tn):
    T, D = x.shape
    const = lambda *shape: pl.BlockSpec(shape, lambda i: (0,) * len(shape), pipeline_mode=pl.Buffered(1))
    return pl.pallas_call(
        _peer_topk_body,
        grid=(T // tn,),
        in_specs=[pl.BlockSpec((tn, D), lambda i: (i, 0)), const(1, D), const(D, D),
                  const(2 * PEER_HEADS, PEER_KEYS, PEER_QDIM // 2)],
        out_specs=[pl.BlockSpec((tn, D), lambda i: (i, 0)),
                   pl.BlockSpec((PEER_HEADS, PEER_KEYS, tn), lambda i: (0, 0, i)),
                   pl.BlockSpec((PEER_HEADS, PEER_KEYS, tn), lambda i: (0, 0, i)),
                   pl.BlockSpec((PEER_HEADS, tn), lambda i: (0, i)),
                   pl.BlockSpec((PEER_HEADS, tn), lambda i: (0, i))],
        out_shape=[jax.ShapeDtypeStruct((T, D), BF16),
                   jax.ShapeDtypeStruct((PEER_HEADS, PEER_KEYS, T), F32),
                   jax.ShapeDtypeStruct((PEER_HEADS, PEER_KEYS, T), F32),
                   jax.ShapeDtypeStruct((PEER_HEADS, T), F32),
                   jax.ShapeDtypeStruct((PEER_HEADS, T), F32)],
        scratch_shapes=[pltpu.VMEM((PEER_TOPK, tn), F32), pltpu.VMEM((PEER_TOPK, tn), F32)],
        compiler_params=_cparams(("parallel",)),
        name="peer_topk",
    )(x, g, wq, subkeys)


LANE_CHUNK = 128


def _peer_main_body(hn_ref, s1_ref, s2_ref, thr_ref, ln_ref, u_ref, vt_ref, x_ref, o_ref,
                    acc_ref, p_ref, *, ab):
    j = pl.program_id(1)
    tn = hn_ref.shape[0]

    @pl.when(j == 0)
    def _():
        acc_ref[...] = jnp.zeros(acc_ref.shape, F32)

    hn = hn_ref[...]
    for k in range(ab):
        at = lax.dot_general(u_ref[k * PEER_KEYS:(k + 1) * PEER_KEYS, :], hn,
                             (((1,), (1,)), ((), ())), preferred_element_type=F32)
        for c in range(tn // LANE_CHUNK):
            ls = slice(c * LANE_CHUNK, (c + 1) * LANE_CHUNK)
            w = jnp.zeros((PEER_KEYS, LANE_CHUNK), F32)
            for h in range(PEER_HEADS):
                t0 = s2_ref[h, :, ls] + s1_ref[h, k, :, ls]
                w = w + jnp.where(t0 >= thr_ref[h:h + 1, ls], jnp.exp(t0 - ln_ref[h:h + 1, ls]), 0.0)
            p_ref[k * PEER_KEYS:(k + 1) * PEER_KEYS, ls] = (w * jax.nn.gelu(at[:, ls])).astype(BF16)
    acc_ref[...] += jnp.dot(vt_ref[...], p_ref[...], preferred_element_type=F32)

    @pl.when(j == pl.num_programs(1) - 1)
    def _():
        o_ref[...] = x_ref[...] + acc_ref[...].T


def peer_main(hn, s1, s2, thr, ln, u, vt, x, tn, ab):
    T, D = x.shape
    e_blk = ab * PEER_KEYS
    return pl.pallas_call(
        functools.partial(_peer_main_body, ab=ab),
        grid=(T // tn, PEER_KEYS // ab),
        in_specs=[pl.BlockSpec((tn, D), lambda i, j: (i, 0)),
                  pl.BlockSpec((PEER_HEADS, ab, 1, tn), lambda i, j: (0, j, 0, i)),
                  pl.BlockSpec((PEER_HEADS, PEER_KEYS, tn), lambda i, j: (0, 0, i)),
                  pl.BlockSpec((PEER_HEADS, tn), lambda i, j: (0, i)),
                  pl.BlockSpec((PEER_HEADS, tn), lambda i, j: (0, i)),
                  pl.BlockSpec((e_blk, D), lambda i, j: (j, 0)),
                  pl.BlockSpec((D, e_blk), lambda i, j: (0, j)),
                  pl.BlockSpec((tn, D), lambda i, j: (i, 0))],
        out_specs=pl.BlockSpec((tn, D), lambda i, j: (i, 0)),
        out_shape=jax.ShapeDtypeStruct((T, D), F32),
        scratch_shapes=[pltpu.VMEM((D, tn), F32), pltpu.VMEM((e_blk, tn), BF16)],
        compiler_params=_cparams(("parallel", "arbitrary")),
        name="peer_main",
    )(hn, s1, s2, thr, ln, u, vt, x)


def peer_ffn_update(x, g, wq, subkeys, u, vt):
    hn, s1, s2, thr, ln = peer_topk(x, g, wq, subkeys, tn=256)
    s1 = s1.reshape(PEER_HEADS, PEER_KEYS, 1, x.shape[0])
    return peer_main(hn, s1, s2, thr, ln, u, vt, x, tn=512, ab=4)


def rmsnorm(x, g):
    xf = x.astype(jnp.float32)
    xf = xf * lax.rsqrt(jnp.mean(xf * xf, axis=-1, keepdims=True) + EPS)
    return (xf * g.astype(jnp.float32)).astype(x.dtype)


def layernorm(x, g):
    xf = x.astype(jnp.float32)
    xf = xf - jnp.mean(xf, axis=-1, keepdims=True)
    xf = xf * lax.rsqrt(jnp.mean(xf * xf, axis=-1, keepdims=True) + EPS)
    return (xf * g.astype(jnp.float32)).astype(x.dtype)


def rel_bucket(dist):
    n = jnp.maximum(dist, 0)
    max_exact = N_BUCKETS // 2
    nf = jnp.maximum(n, 1).astype(jnp.float32)
    large = max_exact + (jnp.log(nf / max_exact) / math.log(MAX_DISTANCE / max_exact)
                         * (N_BUCKETS - max_exact)).astype(jnp.int32)
    return jnp.where(n < max_exact, n, jnp.minimum(large, N_BUCKETS - 1))


def masked_softmax(logits, mask, axis):
    lf = jnp.where(mask, logits.astype(jnp.float32), NEG_INF)
    m = jnp.max(lf, axis=axis, keepdims=True)
    e = jnp.where(mask, jnp.exp(lf - m), 0.0)
    return e / jnp.maximum(jnp.sum(e, axis=axis, keepdims=True), 1e-30)


def short_conv_mixer(b_gate, c_gate, xv, conv_w):
    S = xv.shape[1]
    bx = c_gate * xv
    xp = jnp.pad(bx, ((0, 0), (CONV_K - 1, 0), (0, 0)))
    conv = sum(conv_w[k] * xp[:, k:k + S] for k in range(CONV_K))
    return b_gate * conv


def nsa_mixer(q, kv, gate_logits, cmp_pe, cmp_w1, cmp_w2, q_gain, k_gain, rel_tbl):
    B, S = q.shape[:2]
    H, Hkv, dh = NSA_HEADS, NSA_KV_HEADS, HEAD_DIM
    G = H // Hkv
    q = rmsnorm(q, q_gain) * (dh ** -0.5)
    k_c, v_c, k_s, v_s, k_w, v_w = (kv[:, :, i] for i in range(6))
    k_s = rmsnorm(k_s, k_gain[1])
    k_w = rmsnorm(k_w, k_gain[2])

    n_cmp = (S - CMP_BLOCK) // CMP_STRIDE + 1
    cmp_idx = np.arange(n_cmp)[:, None] * CMP_STRIDE + np.arange(CMP_BLOCK)[None, :]

    def compress(t, j):
        blk = t[:, cmp_idx] + cmp_pe[j][None, None, :, None, :]
        blk = blk.transpose(0, 1, 3, 2, 4).reshape(B, n_cmp, Hkv, CMP_BLOCK * dh)
        return jax.nn.gelu(blk @ cmp_w1[j]) @ cmp_w2[j]

    k_c = rmsnorm(compress(k_c, 0), k_gain[0])
    v_c = compress(v_c, 1)
    cmp_end = jnp.asarray(cmp_idx[:, -1], jnp.int32)

    n_sel = S // SEL_BLOCK
    sel_start = np.arange(n_sel) * SEL_BLOCK
    cmp_start = cmp_idx[:, 0]
    overlap = jnp.asarray((cmp_start[:, None] < sel_start[None, :] + SEL_BLOCK)
                          & (cmp_start[:, None] + CMP_BLOCK > sel_start[None, :]), jnp.float32)
    ks_blk = k_s.reshape(B, n_sel, SEL_BLOCK, Hkv, dh).transpose(0, 3, 1, 2, 4)
    vs_blk = v_s.reshape(B, n_sel, SEL_BLOCK, Hkv, dh).transpose(0, 3, 1, 2, 4)
    top_n = min(SEL_TOPN, n_sel)

    kw_p = jnp.pad(k_w, ((0, 0), (WINDOW, 0), (0, 0), (0, 0)))
    vw_p = jnp.pad(v_w, ((0, 0), (WINDOW, 0), (0, 0), (0, 0)))

    tbl = rel_tbl[:, :NSA_HEADS].reshape(N_BUCKETS, Hkv, G)
    b_idx = jnp.arange(B)[:, None, None]
    g_idx = jnp.arange(Hkv)[None, :, None]

    def block(i):
        q0 = i * Q_BLOCK
        t = q0 + jnp.arange(Q_BLOCK)
        qb = lax.dynamic_slice_in_dim(q, q0, Q_BLOCK, axis=1).reshape(B, Q_BLOCK, Hkv, G, dh)
        gb = lax.dynamic_slice_in_dim(gate_logits, q0, Q_BLOCK, axis=1)

        dist_c = t[:, None] - cmp_end[None, :]
        lc = jnp.einsum('btgrd,bngd->bgrtn', qb, k_c)
        lc = lc + tbl[rel_bucket(dist_c)].transpose(2, 3, 0, 1)[None]
        pc = masked_softmax(lc, dist_c >= 0, -1)
        o_c = jnp.einsum('bgrtn,bngd->btgrd', pc, v_c)

        imp = jnp.einsum('bgrtn,nj->bgtj', pc, overlap)
        cur = t // SEL_BLOCK
        j = jnp.arange(n_sel)
        forced = (j[None, :] == 0) | (j[None, :] == cur[:, None])
        imp = jnp.where(forced, FORCED_SCORE, imp)
        imp = jnp.where(j[None, :] <= cur[:, None], imp, NEG_INF)
        _, sel = lax.top_k(imp, top_n)
        sel_f = sel.reshape(B, Hkv, Q_BLOCK * top_n)
        k_sel = ks_blk[b_idx, g_idx, sel_f].reshape(B, Hkv, Q_BLOCK, top_n, SEL_BLOCK, dh)
        v_sel = vs_blk[b_idx, g_idx, sel_f].reshape(B, Hkv, Q_BLOCK, top_n, SEL_BLOCK, dh)
        pos_s = sel[..., None] * SEL_BLOCK + jnp.arange(SEL_BLOCK)
        dist_s = t[None, None, :, None, None] - pos_s
        ls = jnp.einsum('btgrd,bgtnkd->bgrtnk', qb, k_sel)
        bias_s = tbl[rel_bucket(dist_s), jnp.arange(Hkv)[None, :, None, None, None]]
        ls = ls + jnp.moveaxis(bias_s, -1, 2)
        ps = masked_softmax(ls, (dist_s >= 0)[:, :, None], (-2, -1))
        o_s = jnp.einsum('bgrtnk,bgtnkd->btgrd', ps, v_sel)

        kwb = lax.dynamic_slice_in_dim(kw_p, q0, Q_BLOCK + WINDOW, axis=1)
        vwb = lax.dynamic_slice_in_dim(vw_p, q0, Q_BLOCK + WINDOW, axis=1)
        pos_w = q0 - WINDOW + jnp.arange(Q_BLOCK + WINDOW)
        dist_w = t[:, None] - pos_w[None, :]
        mask_w = (dist_w >= 0) & (dist_w < WINDOW) & (pos_w[None, :] >= 0)
        lw = jnp.einsum('btgrd,bsgd->bgrts', qb, kwb)
        lw = lw + tbl[rel_bucket(dist_w)].transpose(2, 3, 0, 1)[None]
        pw = masked_softmax(lw, mask_w, -1)
        o_w = jnp.einsum('bgrts,bsgd->btgrd', pw, vwb)

        g = jax.nn.sigmoid(gb.astype(jnp.float32)).reshape(B, Q_BLOCK, 3, Hkv, G)[..., None]
        o = g[:, :, 0] * o_c + g[:, :, 1] * o_s + g[:, :, 2] * o_w
        return o.reshape(B, Q_BLOCK, H * dh).astype(q.dtype)

    out = lax.map(block, jnp.arange(S // Q_BLOCK))
    return out.transpose(1, 0, 2, 3).reshape(B, S, H * dh)


def sgu_mixer(uv, ln_gain, w_s, b_s):
    B, S, _ = uv.shape
    uv = jax.nn.gelu(uv)
    u, v = jnp.split(uv, 2, axis=-1)
    v = layernorm(v, ln_gain).reshape(B, S // SGU_CHUNK, SGU_CHUNK, SGU_GROUPS, MIX_W // SGU_GROUPS)
    tri = jnp.tril(jnp.ones((SGU_CHUNK, SGU_CHUNK), dtype=bool))
    w = jnp.where(tri[None], w_s, 0)
    s = jnp.einsum('gts,bcsgd->bctgd', w, v) + b_s.T[:, :, None]
    return u * s.reshape(B, S, MIX_W)


def dsa_mixer(q, k, v, q_idx, k_idx, w_idx, q_gain, k_gain, rel_tbl):
    B, S = q.shape[:2]
    H, dh = DSA_HEADS, HEAD_DIM
    q = rmsnorm(q, q_gain) * (dh ** -0.5)
    k = rmsnorm(k, k_gain)
    w_idx = w_idx * (IDX_HEADS ** -0.5)
    top_k = min(DSA_TOPK_MAX, S // 4)
    tbl = rel_tbl[:, NSA_HEADS:]
    b_idx = jnp.arange(B)[:, None, None]
    key_pos = jnp.arange(S)

    def block(i):
        q0 = i * Q_BLOCK
        t = q0 + jnp.arange(Q_BLOCK)
        qb = lax.dynamic_slice_in_dim(q, q0, Q_BLOCK, axis=1)
        qib = lax.dynamic_slice_in_dim(q_idx, q0, Q_BLOCK, axis=1)
        wib = lax.dynamic_slice_in_dim(w_idx, q0, Q_BLOCK, axis=1)
        score = jnp.einsum('bth,bths->bts', wib,
                           jax.nn.relu(jnp.einsum('bthd,bsd->bths', qib, k_idx)))
        score = jnp.where(key_pos[None, None, :] <= t[None, :, None],
                          score.astype(jnp.float32), NEG_INF)
        _, sel = lax.top_k(score, top_k)
        k_sel = k[b_idx, sel]
        v_sel = v[b_idx, sel]
        dist = t[None, :, None] - sel
        logits = jnp.einsum('bthd,btkd->bhtk', qb, k_sel)
        logits = logits + jnp.moveaxis(tbl[rel_bucket(dist)], -1, 1)
        p = masked_softmax(logits, (dist >= 0)[:, None], -1)
        o = jnp.einsum('bhtk,btkd->bthd', p, v_sel)
        return o.reshape(B, Q_BLOCK, H * dh).astype(q.dtype)

    out = lax.map(block, jnp.arange(S // Q_BLOCK))
    return out.transpose(1, 0, 2, 3).reshape(B, S, H * dh)


def _layer(x2, p2, B, S, rel_bias, g_mix, w_in, conv_w, nsa_cmp_pe, nsa_cmp_w1, nsa_cmp_w2,
           nsa_q_gain, nsa_k_gain, sgu_ln_gain, sgu_w, sgu_b, dsa_q_gain, dsa_k_gain,
           w_gate, w_branch, w_out, g_ffn, peer_wq, peer_subkeys, peer_u, peer_v,
           g_ple, w_ple_gate, w_ple_proj):
    T = B * S
    w_in_p = jnp.pad(w_in, ((0, 0), (0, IN_WIDTH_PAD - IN_WIDTH))).astype(BF16)
    proj = rms_matmul(x2, g_mix[None], w_in_p, tm=512, tn=1024)
    proj = proj[:, :IN_WIDTH].reshape(B, S, IN_WIDTH)
    split_points = [int(c) for c in np.cumsum(SPLIT_WIDTHS)[:-1]]
    (a_b, a_c, a_x, n_q, n_kv, n_g, c_uv,
     d_q, d_k, d_v, d_qi, d_ki, d_wi) = jnp.split(proj, split_points, axis=-1)

    y_a = short_conv_mixer(a_b, a_c, a_x, conv_w)
    y_b = nsa_mixer(n_q.reshape(B, S, NSA_HEADS, HEAD_DIM),
                    n_kv.reshape(B, S, 6, NSA_KV_HEADS, HEAD_DIM),
                    n_g.reshape(B, S, 3, NSA_HEADS),
                    nsa_cmp_pe, nsa_cmp_w1, nsa_cmp_w2, nsa_q_gain, nsa_k_gain, rel_bias)
    y_c = sgu_mixer(c_uv, sgu_ln_gain, sgu_w, sgu_b)
    y_d = dsa_mixer(d_q.reshape(B, S, DSA_HEADS, HEAD_DIM), d_k, d_v,
                    d_qi.reshape(B, S, IDX_HEADS, IDX_DIM), d_ki, d_wi,
                    dsa_q_gain, dsa_k_gain, rel_bias)
    ys = [y.reshape(T, MIX_W) for y in (y_a, y_b, y_c, y_d)]

    x2 = merge_mixers(x2, g_mix[None], ys, w_gate.astype(BF16),
                      w_branch.reshape(N_MIXERS, MIX_W, D_MODEL).astype(BF16),
                      w_out.astype(BF16), tm=256)

    sk = peer_subkeys.reshape(2 * PEER_HEADS, PEER_KEYS, PEER_QDIM // 2).astype(BF16)
    x2 = peer_ffn_update(x2, g_ffn[None], peer_wq.astype(BF16), sk,
                         peer_u.astype(BF16), peer_v.astype(BF16).T)

    x2 = ple_update(x2, g_ple[None], p2, w_ple_gate.astype(BF16), w_ple_proj.astype(BF16), tm=512)
    return x2


def kernel(x, p, rel_bias, g_mix, w_in, conv_w, nsa_cmp_pe, nsa_cmp_w1, nsa_cmp_w2, nsa_q_gain,
           nsa_k_gain, sgu_ln_gain, sgu_w, sgu_b, dsa_q_gain, dsa_k_gain, w_gate, w_branch, w_out,
           g_ffn, peer_wq, peer_subkeys, peer_u, peer_v, g_ple, w_ple_gate, w_ple_proj):
    B, S, D = x.shape
    depth = p.shape[0]
    x2 = x.reshape(B * S, D)
    for l in range(depth):
        x2 = _layer(x2, p[l].reshape(B * S, PLE_DIM), B, S, rel_bias, g_mix[l], w_in[l], conv_w[l],
                    nsa_cmp_pe[l], nsa_cmp_w1[l], nsa_cmp_w2[l], nsa_q_gain[l], nsa_k_gain[l],
                    sgu_ln_gain[l], sgu_w[l], sgu_b[l], dsa_q_gain[l], dsa_k_gain[l],
                    w_gate[l], w_branch[l], w_out[l], g_ffn[l], peer_wq[l], peer_subkeys[l],
                    peer_u[l], peer_v[l], g_ple[l], w_ple_gate[l], w_ple_proj[l])
    return x2.reshape(B, S, D)
```

```python
import functools
import math

import jax
import jax.numpy as jnp
import numpy as np
from jax import lax
from jax.experimental import pallas as pl
from jax.experimental.pallas import tpu as pltpu

F32 = jnp.float32
BF16 = jnp.bfloat16

D_MODEL = 1024
HEAD_DIM = 64
N_MIXERS = 4
MIX_W = D_MODEL // 2
Q_BLOCK = 128
EPS = 1e-6
NEG_INF = -1e30
CONV_K = 3
NSA_HEADS = MIX_W // HEAD_DIM
NSA_KV_HEADS = 2
CMP_BLOCK = 32
CMP_STRIDE = 16
CMP_HIDDEN = 256
SEL_BLOCK = 64
SEL_TOPN = 8
WINDOW = 512
FORCED_SCORE = 1e4
SGU_CHUNK = 128
SGU_GROUPS = 8
DSA_HEADS = MIX_W // HEAD_DIM
IDX_HEADS = 8
IDX_DIM = 64
DSA_TOPK_MAX = 256
N_BUCKETS = 32
MAX_DISTANCE = 1024
PEER_HEADS = 8
PEER_KEYS = 128
PEER_QDIM = 128
PEER_TOPK = 16
N_EXPERTS = PEER_KEYS * PEER_KEYS
PLE_DIM = 256

SPLIT_WIDTHS = (
    MIX_W, MIX_W, MIX_W,
    NSA_HEADS * HEAD_DIM,
    6 * NSA_KV_HEADS * HEAD_DIM,
    3 * NSA_HEADS,
    2 * MIX_W,
    DSA_HEADS * HEAD_DIM, HEAD_DIM, HEAD_DIM,
    IDX_HEADS * IDX_DIM, IDX_DIM, IDX_HEADS,
)
IN_WIDTH = sum(SPLIT_WIDTHS)

COL = dict(a_b=0, a_c=512, a_x=1024, n_q=1536, d_q=2048, d_qi=2560, c_uv=3072, n_kv=4096,
           n_g=4864, d_kv=4992, d_kiw=5120)
N_PACK = 5376


def pack_w_in(w):
    o = dict(zip(("a_b", "a_c", "a_x", "n_q", "n_kv", "n_g", "c_uv", "d_q", "d_k", "d_v", "d_qi", "d_ki",
                  "d_wi", "end"), [0] + [int(c) for c in np.cumsum(SPLIT_WIDTHS)]))
    z = lambda n: jnp.zeros((w.shape[0], n), w.dtype)
    cols = [w[:, o["a_b"]:o["n_kv"]], w[:, o["d_q"]:o["d_k"]], w[:, o["d_qi"]:o["d_ki"]],
            w[:, o["c_uv"]:o["d_q"]], w[:, o["n_kv"]:o["n_g"]],
            w[:, o["n_g"]:o["c_uv"]], z(128 - 3 * NSA_HEADS),
            w[:, o["d_k"]:o["d_qi"]],
            w[:, o["d_ki"]:o["end"]], z(128 - IDX_DIM - IDX_HEADS)]
    packed = jnp.concatenate(cols, axis=1)
    return jnp.pad(packed, ((0, 0), (0, N_PACK - packed.shape[1]))).astype(BF16)

VMEM_LIMIT_BYTES = 56 * 1024 * 1024


def _cparams(sem):
    return pltpu.CompilerParams(dimension_semantics=sem, vmem_limit_bytes=VMEM_LIMIT_BYTES)


def _rms(x, g):
    return x * lax.rsqrt(jnp.mean(x * x, axis=-1, keepdims=True) + EPS) * g


def _rms_matmul_body(x_ref, g_ref, w_ref, o_ref, h_ref):
    @pl.when(pl.program_id(1) == 0)
    def _():
        h_ref[...] = _rms(x_ref[...], g_ref[...]).astype(BF16)

    o_ref[...] = jnp.dot(h_ref[...], w_ref[...], preferred_element_type=F32)


def rms_matmul(x, g, w, tm, tn):
    T, D = x.shape
    N = w.shape[1]
    return pl.pallas_call(
        _rms_matmul_body,
        grid=(T // tm, N // tn),
        in_specs=[pl.BlockSpec((tm, D), lambda i, j: (i, 0)),
                  pl.BlockSpec((1, D), lambda i, j: (0, 0)),
                  pl.BlockSpec((D, tn), lambda i, j: (0, j))],
        out_specs=pl.BlockSpec((tm, tn), lambda i, j: (i, j)),
        out_shape=jax.ShapeDtypeStruct((T, N), F32),
        scratch_shapes=[pltpu.VMEM((tm, D), BF16)],
        compiler_params=_cparams(("parallel", "arbitrary")),
        name="rms_matmul",
    )(x, g, w)


def _merge_body(x_ref, g_ref, ya_ref, yb_ref, yc_ref, yd_ref, wg_ref, wb_ref, wo_ref, o_ref):
    x = x_ref[...]
    h = _rms(x, g_ref[...]).astype(BF16)
    merged = jnp.zeros(x.shape, F32)
    for m, y_ref in enumerate((ya_ref, yb_ref, yc_ref, yd_ref)):
        z = jnp.dot(y_ref[...].astype(BF16), wb_ref[m], preferred_element_type=F32)
        gate = jax.nn.sigmoid(jnp.dot(h, wg_ref[:, m * D_MODEL:(m + 1) * D_MODEL],
                                      preferred_element_type=F32))
        merged = merged + gate * z
    o_ref[...] = x + jnp.dot(merged.astype(BF16), wo_ref[...], preferred_element_type=F32)


def merge_mixers(x, g, ys, w_gate, w_branch, w_out, tm):
    T, D = x.shape
    const = lambda *shape: pl.BlockSpec(shape, lambda i: (0,) * len(shape), pipeline_mode=pl.Buffered(1))
    return pl.pallas_call(
        _merge_body,
        grid=(T // tm,),
        in_specs=[pl.BlockSpec((tm, D), lambda i: (i, 0)),
                  const(1, D)]
                 + [pl.BlockSpec((tm, MIX_W), lambda i: (i, 0)) for _ in range(N_MIXERS)]
                 + [const(D, N_MIXERS * D), const(N_MIXERS, MIX_W, D), const(D, D)],
        out_specs=pl.BlockSpec((tm, D), lambda i: (i, 0)),
        out_shape=jax.ShapeDtypeStruct((T, D), F32),
        compiler_params=_cparams(("parallel",)),
        name="merge_mixers",
    )(x, g, *ys, w_gate, w_branch, w_out)


def _ple_body(x_ref, g_ref, p_ref, wg_ref, wp_ref, o_ref):
    x = x_ref[...]
    h = _rms(x, g_ref[...]).astype(BF16)
    gate = jax.nn.sigmoid(jnp.dot(h, wg_ref[...], preferred_element_type=F32))
    proj = jnp.dot(p_ref[...].astype(BF16), wp_ref[...], preferred_element_type=F32)
    o_ref[...] = x + gate * proj


def ple_update(x, g, p, w_gate, w_proj, tm):
    T, D = x.shape
    const = lambda *shape: pl.BlockSpec(shape, lambda i: (0,) * len(shape), pipeline_mode=pl.Buffered(1))
    return pl.pallas_call(
        _ple_body,
        grid=(T // tm,),
        in_specs=[pl.BlockSpec((tm, D), lambda i: (i, 0)), const(1, D),
                  pl.BlockSpec((tm, PLE_DIM), lambda i: (i, 0)),
                  const(D, D), const(PLE_DIM, D)],
        out_specs=pl.BlockSpec((tm, D), lambda i: (i, 0)),
        out_shape=jax.ShapeDtypeStruct((T, D), F32),
        compiler_params=_cparams(("parallel",)),
        name="ple_update",
    )(x, g, p, w_gate, w_proj)


def _extract_top16(cur, tops_ref):
    m = None
    for r in range(PEER_TOPK):
        m = jnp.max(cur, axis=0, keepdims=True)
        if tops_ref is not None:
            tops_ref[r:r + 1, :] = m
        cur = jnp.where(cur == m, NEG_INF, cur)
    return m


_CAND_ROWS_J = (16, 8, 5, 4, 3, 2, 2, 2)


def _peer_topk_body(x_ref, g_ref, wq_ref, sk_ref, hn_ref, s1_ref, s2_ref, thr_ref, ln_ref,
                    t1_ref, t2_ref):
    hn = _rms(x_ref[...], g_ref[...]).astype(BF16)
    hn_ref[...] = hn
    q = jnp.dot(hn, wq_ref[...], preferred_element_type=F32).astype(BF16)
    half = PEER_QDIM // 2
    tn = x_ref.shape[0]
    j_iota = lax.broadcasted_iota(jnp.int32, (8, tn), 0)
    for h in range(PEER_HEADS):
        s = []
        for p in range(2):
            c0 = (2 * h + p) * half
            s.append(lax.dot_general(sk_ref[2 * h + p], q[:, c0:c0 + half],
                                     (((1,), (1,)), ((), ())), preferred_element_type=F32))
        th1 = _extract_top16(s[0], t1_ref)
        th2 = _extract_top16(s[1], t2_ref)
        a = t1_ref[...]
        b = t2_ref[...]
        b8 = b[0:8]
        pieces = [a[0:1] + b]
        for i in range(1, 8):
            c = a[i:i + 1] + b8
            if _CAND_ROWS_J[i] < 8:
                c = jnp.where(j_iota < _CAND_ROWS_J[i], c, NEG_INF)
            pieces.append(c)
        pieces.append(a[8:16] + b[0:1])
        cand = jnp.concatenate(pieces, axis=0)
        top = a[0:1] + b[0:1]
        theta = _extract_top16(cand, None)
        z = jnp.sum(jnp.where(cand >= theta, jnp.exp(cand - top), 0.0), axis=0, keepdims=True)
        s1_ref[h] = jnp.where(s[0] >= th1, s[0], NEG_INF)
        s2_ref[h] = jnp.where(s[1] >= th2, s[1], NEG_INF)
        thr_ref[h:h + 1, :] = theta
        ln_ref[h:h + 1, :] = top + jnp.log(z)


def peer_topk(x, g, wq, subkeys, tn):
    T, D = x.shape
    const = lambda *shape: pl.BlockSpec(shape, lambda i: (0,) * len(shape), pipeline_mode=pl.Buffered(1))
    return pl.pallas_call(
        _peer_topk_body,
        grid=(T // tn,),
        in_specs=[pl.BlockSpec((tn, D), lambda i: (i, 0)), const(1, D), const(D, D),
                  const(2 * PEER_HEADS, PEER_KEYS, PEER_QDIM // 2)],
        out_specs=[pl.BlockSpec((tn, D), lambda i: (i, 0)),
                   pl.BlockSpec((PEER_HEADS, PEER_KEYS, tn), lambda i: (0, 0, i)),
                   pl.BlockSpec((PEER_HEADS, PEER_KEYS, tn), lambda i: (0, 0, i)),
                   pl.BlockSpec((PEER_HEADS, tn), lambda i: (0, i)),
                   pl.BlockSpec((PEER_HEADS, tn), lambda i: (0, i))],
        out_shape=[jax.ShapeDtypeStruct((T, D), BF16),
                   jax.ShapeDtypeStruct((PEER_HEADS, PEER_KEYS, T), F32),
                   jax.ShapeDtypeStruct((PEER_HEADS, PEER_KEYS, T), F32),
                   jax.ShapeDtypeStruct((PEER_HEADS, T), F32),
                   jax.ShapeDtypeStruct((PEER_HEADS, T), F32)],
        scratch_shapes=[pltpu.VMEM((PEER_TOPK, tn), F32), pltpu.VMEM((PEER_TOPK, tn), F32)],
        compiler_params=_cparams(("parallel",)),
        name="peer_topk",
    )(x, g, wq, subkeys)


LANE_CHUNK = 128


def _peer_main_body(hn_ref, s1_ref, s2_ref, thr_ref, ln_ref, u_ref, vt_ref, x_ref, o_ref,
                    acc_ref, p_ref, *, ab):
    j = pl.program_id(1)
    tn = hn_ref.shape[0]

    @pl.when(j == 0)
    def _():
        acc_ref[...] = jnp.zeros(acc_ref.shape, F32)

    hn = hn_ref[...]
    for k in range(ab):
        at = lax.dot_general(u_ref[k * PEER_KEYS:(k + 1) * PEER_KEYS, :], hn,
                             (((1,), (1,)), ((), ())), preferred_element_type=F32)
        for c in range(tn // LANE_CHUNK):
            ls = slice(c * LANE_CHUNK, (c + 1) * LANE_CHUNK)
            w = jnp.zeros((PEER_KEYS, LANE_CHUNK), F32)
            for h in range(PEER_HEADS):
                t0 = s2_ref[h, :, ls] + s1_ref[h, k, :, ls]
                w = w + jnp.where(t0 >= thr_ref[h:h + 1, ls], jnp.exp(t0 - ln_ref[h:h + 1, ls]), 0.0)
            p_ref[k * PEER_KEYS:(k + 1) * PEER_KEYS, ls] = (w * jax.nn.gelu(at[:, ls])).astype(BF16)
    acc_ref[...] += jnp.dot(vt_ref[...], p_ref[...], preferred_element_type=F32)

    @pl.when(j == pl.num_programs(1) - 1)
    def _():
        o_ref[...] = x_ref[...] + acc_ref[...].T


def peer_main(hn, s1, s2, thr, ln, u, vt, x, tn, ab):
    T, D = x.shape
    e_blk = ab * PEER_KEYS
    return pl.pallas_call(
        functools.partial(_peer_main_body, ab=ab),
        grid=(T // tn, PEER_KEYS // ab),
        in_specs=[pl.BlockSpec((tn, D), lambda i, j: (i, 0)),
                  pl.BlockSpec((PEER_HEADS, ab, 1, tn), lambda i, j: (0, j, 0, i)),
                  pl.BlockSpec((PEER_HEADS, PEER_KEYS, tn), lambda i, j: (0, 0, i)),
                  pl.BlockSpec((PEER_HEADS, tn), lambda i, j: (0, i)),
                  pl.BlockSpec((PEER_HEADS, tn), lambda i, j: (0, i)),
                  pl.BlockSpec((e_blk, D), lambda i, j: (j, 0)),
                  pl.BlockSpec((D, e_blk), lambda i, j: (0, j)),
                  pl.BlockSpec((tn, D), lambda i, j: (i, 0))],
        out_specs=pl.BlockSpec((tn, D), lambda i, j: (i, 0)),
        out_shape=jax.ShapeDtypeStruct((T, D), F32),
        scratch_shapes=[pltpu.VMEM((D, tn), F32), pltpu.VMEM((e_blk, tn), BF16)],
        compiler_params=_cparams(("parallel", "arbitrary")),
        name="peer_main",
    )(hn, s1, s2, thr, ln, u, vt, x)


def peer_ffn_update(x, g, wq, subkeys, u, vt):
    hn, s1, s2, thr, ln = peer_topk(x, g, wq, subkeys, tn=256)
    s1 = s1.reshape(PEER_HEADS, PEER_KEYS, 1, x.shape[0])
    return peer_main(hn, s1, s2, thr, ln, u, vt, x, tn=512, ab=4)


BIAS_TILES = 9


def rel_bias_tiles(tbl):
    H = tbl.shape[1]
    d = jnp.arange(BIAS_TILES)[:, None, None]
    s = jnp.arange(Q_BLOCK)[None, :, None]
    t = jnp.arange(Q_BLOCK)[None, None, :]
    bias = tbl[rel_bucket(d * Q_BLOCK + t - s)]
    return jnp.moveaxis(bias, -1, 2).reshape(BIAS_TILES, Q_BLOCK, H * Q_BLOCK)


INT_MIN = -2 ** 31
NEG_INF_KEY = -1900671691


def _sortable_key(x):
    bits = pltpu.bitcast(x, jnp.int32)
    return jnp.where(bits < 0, bits ^ jnp.int32(0x7FFFFFFF), bits)


def _head_rms(x, g, scale):
    return x * lax.rsqrt(jnp.mean(x * x, axis=-1, keepdims=True) + EPS) * g * scale


def _dsa_body(kv_ref, kiw_ref, q_ref, qi_ref, wq_ref, qg_ref, kg_ref, bias_ref, o_ref,
              kn_scr, ki_scr, vt_scr, qall_scr, qiall_scr, sc_scr, th_scr, cnt_scr,
              m_scr, l_scr, acc_scr, *, top_k):
    i = pl.program_id(1)
    dh = HEAD_DIM
    QB = Q_BLOCK

    @pl.when(i == 0)
    def _prep():
        kv = kv_ref[...]
        kn_scr[...] = _head_rms(kv[:, :dh], kg_ref[...], 1.0).astype(BF16)
        vt_scr[...] = kv[:, dh:].T.astype(BF16)
        ki_scr[...] = kiw_ref[:, :dh].astype(BF16)

    q = q_ref[...]
    qi = qi_ref[...]
    for h in range(DSA_HEADS):
        qh = _head_rms(q[:, h * dh:(h + 1) * dh], qg_ref[...], dh ** -0.5)
        qall_scr[h * QB:(h + 1) * QB, :] = qh.astype(BF16)
        qiall_scr[h * QB:(h + 1) * QB, :] = qi[:, h * dh:(h + 1) * dh].astype(BF16)
    w_t = (wq_ref[:, dh:dh + IDX_HEADS] * (IDX_HEADS ** -0.5)).T

    s_loc = lax.broadcasted_iota(jnp.int32, (QB, QB), 0)
    t_loc = lax.broadcasted_iota(jnp.int32, (QB, QB), 1)
    n_tiles = i + 1
    nt_dims = (((1,), (1,)), ((), ()))

    def score_tile(kt, carry):
        kit = ki_scr[pl.ds(pl.multiple_of(kt * QB, QB), QB), :]
        sc = jnp.zeros((QB, QB), F32)
        for h in range(IDX_HEADS):
            r = lax.dot_general(kit, qiall_scr[h * QB:(h + 1) * QB, :], nt_dims,
                                preferred_element_type=F32)
            sc = sc + w_t[h:h + 1, :] * jnp.maximum(r, 0.0)
        sc = jnp.where(sc == 0.0, 0.0, sc)
        sc = jnp.where(s_loc + kt * QB <= t_loc + i * QB, sc, NEG_INF)
        sc_scr[pl.ds(pl.multiple_of(kt * QB, QB), QB), :] = _sortable_key(sc)
        return carry

    lax.fori_loop(0, n_tiles, score_tile, 0)

    def count_ge(cand):
        def body(kt, c):
            tile = sc_scr[pl.ds(pl.multiple_of(kt * QB, QB), QB), :]
            return c + jnp.sum((tile >= cand).astype(jnp.int32), axis=0, keepdims=True)
        return lax.fori_loop(0, n_tiles, body, jnp.zeros((1, QB), jnp.int32))

    th_scr[...] = jnp.full((1, QB), NEG_INF_KEY + 1, jnp.int32)

    @pl.when(n_tiles * QB > top_k)
    def _select():
        def bit_body(it, carry):
            ans_u, cnt_ans = carry
            cand_u = ans_u | (jnp.int32(1) << (31 - it))
            c = count_ge(cand_u ^ jnp.int32(INT_MIN))
            ok = c >= top_k
            return jnp.where(ok, cand_u, ans_u), jnp.where(ok, c, cnt_ans)

        ans_u, cnt = lax.fori_loop(
            0, 32, bit_body,
            (jnp.zeros((1, QB), jnp.int32), jnp.full((1, QB), 1, jnp.int32) * (n_tiles * QB)))
        th_scr[...] = ans_u ^ jnp.int32(INT_MIN)
        cnt_scr[...] = cnt

        @pl.when(jnp.max(cnt) > top_k)
        def _ties():
            theta = th_scr[...]
            need = (top_k - count_ge(theta + 1)).astype(F32)
            tri = (s_loc >= t_loc).astype(BF16)

            def body(kt, seen):
                rows = pl.ds(pl.multiple_of(kt * QB, QB), QB)
                tile = sc_scr[rows, :]
                eq = tile == theta
                rank = seen + jnp.dot(tri, eq.astype(BF16), preferred_element_type=F32)
                sc_scr[rows, :] = jnp.where(eq & (rank > need), theta - 1, tile)
                return seen + jnp.sum(eq.astype(F32), axis=0, keepdims=True)

            lax.fori_loop(0, n_tiles, body, jnp.zeros((1, QB), F32))

    theta = th_scr[...]
    m_scr[...] = jnp.full(m_scr.shape, NEG_INF, F32)
    l_scr[...] = jnp.zeros(l_scr.shape, F32)
    acc_scr[...] = jnp.zeros(acc_scr.shape, F32)
    CH = 2 * QB

    def att_tile(kt, carry):
        rows = pl.ds(pl.multiple_of(kt * QB, QB), QB)
        k_t = kn_scr[rows, :]
        v_t = vt_scr[:, rows]
        mask1 = sc_scr[rows, :] >= theta
        mask = jnp.concatenate([mask1, mask1], axis=1)
        d = jnp.minimum(i - kt, BIAS_TILES - 1)
        for c in range(DSA_HEADS * QB // CH):
            ls = slice(c * CH, (c + 1) * CH)
            lg = lax.dot_general(k_t, qall_scr[ls, :], nt_dims, preferred_element_type=F32)
            lg = jnp.where(mask, lg + bias_ref[d, :, ls], NEG_INF)
            m_old = m_scr[:, ls]
            m_new = jnp.maximum(m_old, jnp.max(lg, axis=0, keepdims=True))
            p = jnp.where(mask, jnp.exp(lg - m_new), 0.0)
            alpha = jnp.exp(m_old - m_new)
            l_scr[:, ls] = alpha * l_scr[:, ls] + jnp.sum(p, axis=0, keepdims=True)
            acc_scr[:, ls] = alpha * acc_scr[:, ls] + jnp.dot(v_t, p.astype(BF16),
                                                              preferred_element_type=F32)
            m_scr[:, ls] = m_new
        return carry

    lax.fori_loop(0, n_tiles, att_tile, 0)

    o_t = acc_scr[...] / jnp.maximum(l_scr[...], 1e-30)
    o_ref[...] = jnp.concatenate([o_t[:, h * QB:(h + 1) * QB].T for h in range(DSA_HEADS)], axis=1)


def dsa_attention(proj, col, q_gain, k_gain, bias, B, S):
    nq = S // Q_BLOCK
    top_k = min(DSA_TOPK_MAX, S // 4)
    HQ = DSA_HEADS * Q_BLOCK
    const = lambda *shape: pl.BlockSpec(shape, lambda b, i: (0,) * len(shape), pipeline_mode=pl.Buffered(1))
    return pl.pallas_call(
        functools.partial(_dsa_body, top_k=top_k),
        grid=(B, nq),
        in_specs=[pl.BlockSpec((S, 128), lambda b, i: (b, col["d_kv"] // 128)),
                  pl.BlockSpec((S, 128), lambda b, i: (b, col["d_kiw"] // 128)),
                  pl.BlockSpec((Q_BLOCK, MIX_W), lambda b, i: (b * nq + i, col["d_q"] // MIX_W)),
                  pl.BlockSpec((Q_BLOCK, MIX_W), lambda b, i: (b * nq + i, col["d_qi"] // MIX_W)),
                  pl.BlockSpec((Q_BLOCK, 128), lambda b, i: (b * nq + i, col["d_kiw"] // 128)),
                  const(1, HEAD_DIM), const(1, HEAD_DIM), const(BIAS_TILES, Q_BLOCK, HQ)],
        out_specs=pl.BlockSpec((Q_BLOCK, MIX_W), lambda b, i: (b * nq + i, 0)),
        out_shape=jax.ShapeDtypeStruct((B * S, MIX_W), F32),
        scratch_shapes=[pltpu.VMEM((S, HEAD_DIM), BF16), pltpu.VMEM((S, HEAD_DIM), BF16),
                        pltpu.VMEM((HEAD_DIM, S), BF16),
                        pltpu.VMEM((HQ, HEAD_DIM), BF16), pltpu.VMEM((HQ, HEAD_DIM), BF16),
                        pltpu.VMEM((S, Q_BLOCK), jnp.int32),
                        pltpu.VMEM((1, Q_BLOCK), jnp.int32), pltpu.VMEM((1, Q_BLOCK), jnp.int32),
                        pltpu.VMEM((1, HQ), F32), pltpu.VMEM((1, HQ), F32), pltpu.VMEM((HEAD_DIM, HQ), F32)],
        compiler_params=_cparams(("parallel", "arbitrary")),
        name="dsa_attention",
    )(proj, proj, proj, proj, proj, q_gain, k_gain, bias)


def rmsnorm(x, g):
    xf = x.astype(jnp.float32)
    xf = xf * lax.rsqrt(jnp.mean(xf * xf, axis=-1, keepdims=True) + EPS)
    return (xf * g.astype(jnp.float32)).astype(x.dtype)


def layernorm(x, g):
    xf = x.astype(jnp.float32)
    xf = xf - jnp.mean(xf, axis=-1, keepdims=True)
    xf = xf * lax.rsqrt(jnp.mean(xf * xf, axis=-1, keepdims=True) + EPS)
    return (xf * g.astype(jnp.float32)).astype(x.dtype)


def rel_bucket(dist):
    n = jnp.maximum(dist, 0)
    max_exact = N_BUCKETS // 2
    nf = jnp.maximum(n, 1).astype(jnp.float32)
    large = max_exact + (jnp.log(nf / max_exact) / math.log(MAX_DISTANCE / max_exact)
                         * (N_BUCKETS - max_exact)).astype(jnp.int32)
    return jnp.where(n < max_exact, n, jnp.minimum(large, N_BUCKETS - 1))


def masked_softmax(logits, mask, axis):
    lf = jnp.where(mask, logits.astype(jnp.float32), NEG_INF)
    m = jnp.max(lf, axis=axis, keepdims=True)
    e = jnp.where(mask, jnp.exp(lf - m), 0.0)
    return e / jnp.maximum(jnp.sum(e, axis=axis, keepdims=True), 1e-30)


def short_conv_mixer(b_gate, c_gate, xv, conv_w):
    S = xv.shape[1]
    bx = c_gate * xv
    xp = jnp.pad(bx, ((0, 0), (CONV_K - 1, 0), (0, 0)))
    conv = sum(conv_w[k] * xp[:, k:k + S] for k in range(CONV_K))
    return b_gate * conv


def nsa_mixer(q, kv, gate_logits, cmp_pe, cmp_w1, cmp_w2, q_gain, k_gain, rel_tbl):
    B, S = q.shape[:2]
    H, Hkv, dh = NSA_HEADS, NSA_KV_HEADS, HEAD_DIM
    G = H // Hkv
    q = rmsnorm(q, q_gain) * (dh ** -0.5)
    k_c, v_c, k_s, v_s, k_w, v_w = (kv[:, :, i] for i in range(6))
    k_s = rmsnorm(k_s, k_gain[1])
    k_w = rmsnorm(k_w, k_gain[2])

    n_cmp = (S - CMP_BLOCK) // CMP_STRIDE + 1
    cmp_idx = np.arange(n_cmp)[:, None] * CMP_STRIDE + np.arange(CMP_BLOCK)[None, :]

    def compress(t, j):
        blk = t[:, cmp_idx] + cmp_pe[j][None, None, :, None, :]
        blk = blk.transpose(0, 1, 3, 2, 4).reshape(B, n_cmp, Hkv, CMP_BLOCK * dh)
        return jax.nn.gelu(blk @ cmp_w1[j]) @ cmp_w2[j]

    k_c = rmsnorm(compress(k_c, 0), k_gain[0])
    v_c = compress(v_c, 1)
    cmp_end = jnp.asarray(cmp_idx[:, -1], jnp.int32)

    n_sel = S // SEL_BLOCK
    sel_start = np.arange(n_sel) * SEL_BLOCK
    cmp_start = cmp_idx[:, 0]
    overlap = jnp.asarray((cmp_start[:, None] < sel_start[None, :] + SEL_BLOCK)
                          & (cmp_start[:, None] + CMP_BLOCK > sel_start[None, :]), jnp.float32)
    ks_blk = k_s.reshape(B, n_sel, SEL_BLOCK, Hkv, dh).transpose(0, 3, 1, 2, 4)
    vs_blk = v_s.reshape(B, n_sel, SEL_BLOCK, Hkv, dh).transpose(0, 3, 1, 2, 4)
    top_n = min(SEL_TOPN, n_sel)

    kw_p = jnp.pad(k_w, ((0, 0), (WINDOW, 0), (0, 0), (0, 0)))
    vw_p = jnp.pad(v_w, ((0, 0), (WINDOW, 0), (0, 0), (0, 0)))

    tbl = rel_tbl[:, :NSA_HEADS].reshape(N_BUCKETS, Hkv, G)
    b_idx = jnp.arange(B)[:, None, None]
    g_idx = jnp.arange(Hkv)[None, :, None]

    def block(i):
        q0 = i * Q_BLOCK
        t = q0 + jnp.arange(Q_BLOCK)
        qb = lax.dynamic_slice_in_dim(q, q0, Q_BLOCK, axis=1).reshape(B, Q_BLOCK, Hkv, G, dh)
        gb = lax.dynamic_slice_in_dim(gate_logits, q0, Q_BLOCK, axis=1)

        dist_c = t[:, None] - cmp_end[None, :]
        lc = jnp.einsum('btgrd,bngd->bgrtn', qb, k_c)
        lc = lc + tbl[rel_bucket(dist_c)].transpose(2, 3, 0, 1)[None]
        pc = masked_softmax(lc, dist_c >= 0, -1)
        o_c = jnp.einsum('bgrtn,bngd->btgrd', pc, v_c)

        imp = jnp.einsum('bgrtn,nj->bgtj', pc, overlap)
        cur = t // SEL_BLOCK
        j = jnp.arange(n_sel)
        forced = (j[None, :] == 0) | (j[None, :] == cur[:, None])
        imp = jnp.where(forced, FORCED_SCORE, imp)
        imp = jnp.where(j[None, :] <= cur[:, None], imp, NEG_INF)
        _, sel = lax.top_k(imp, top_n)
        sel_f = sel.reshape(B, Hkv, Q_BLOCK * top_n)
        k_sel = ks_blk[b_idx, g_idx, sel_f].reshape(B, Hkv, Q_BLOCK, top_n, SEL_BLOCK, dh)
        v_sel = vs_blk[b_idx, g_idx, sel_f].reshape(B, Hkv, Q_BLOCK, top_n, SEL_BLOCK, dh)
        pos_s = sel[..., None] * SEL_BLOCK + jnp.arange(SEL_BLOCK)
        dist_s = t[None, None, :, None, None] - pos_s
        ls = jnp.einsum('btgrd,bgtnkd->bgrtnk', qb, k_sel)
        bias_s = tbl[rel_bucket(dist_s), jnp.arange(Hkv)[None, :, None, None, None]]
        ls = ls + jnp.moveaxis(bias_s, -1, 2)
        ps = masked_softmax(ls, (dist_s >= 0)[:, :, None], (-2, -1))
        o_s = jnp.einsum('bgrtnk,bgtnkd->btgrd', ps, v_sel)

        kwb = lax.dynamic_slice_in_dim(kw_p, q0, Q_BLOCK + WINDOW, axis=1)
        vwb = lax.dynamic_slice_in_dim(vw_p, q0, Q_BLOCK + WINDOW, axis=1)
        pos_w = q0 - WINDOW + jnp.arange(Q_BLOCK + WINDOW)
        dist_w = t[:, None] - pos_w[None, :]
        mask_w = (dist_w >= 0) & (dist_w < WINDOW) & (pos_w[None, :] >= 0)
        lw = jnp.einsum('btgrd,bsgd->bgrts', qb, kwb)
        lw = lw + tbl[rel_bucket(dist_w)].transpose(2, 3, 0, 1)[None]
        pw = masked_softmax(lw, mask_w, -1)
        o_w = jnp.einsum('bgrts,bsgd->btgrd', pw, vwb)

        g = jax.nn.sigmoid(gb.astype(jnp.float32)).reshape(B, Q_BLOCK, 3, Hkv, G)[..., None]
        o = g[:, :, 0] * o_c + g[:, :, 1] * o_s + g[:, :, 2] * o_w
        return o.reshape(B, Q_BLOCK, H * dh).astype(q.dtype)

    out = lax.map(block, jnp.arange(S // Q_BLOCK))
    return out.transpose(1, 0, 2, 3).reshape(B, S, H * dh)


def sgu_mixer(uv, ln_gain, w_s, b_s):
    B, S, _ = uv.shape
    uv = jax.nn.gelu(uv)
    u, v = jnp.split(uv, 2, axis=-1)
    v = layernorm(v, ln_gain).reshape(B, S // SGU_CHUNK, SGU_CHUNK, SGU_GROUPS, MIX_W // SGU_GROUPS)
    tri = jnp.tril(jnp.ones((SGU_CHUNK, SGU_CHUNK), dtype=bool))
    w = jnp.where(tri[None], w_s, 0)
    s = jnp.einsum('gts,bcsgd->bctgd', w, v) + b_s.T[:, :, None]
    return u * s.reshape(B, S, MIX_W)


def dsa_mixer(q, k, v, q_idx, k_idx, w_idx, q_gain, k_gain, rel_tbl):
    B, S = q.shape[:2]
    H, dh = DSA_HEADS, HEAD_DIM
    q = rmsnorm(q, q_gain) * (dh ** -0.5)
    k = rmsnorm(k, k_gain)
    w_idx = w_idx * (IDX_HEADS ** -0.5)
    top_k = min(DSA_TOPK_MAX, S // 4)
    tbl = rel_tbl[:, NSA_HEADS:]
    b_idx = jnp.arange(B)[:, None, None]
    key_pos = jnp.arange(S)

    def block(i):
        q0 = i * Q_BLOCK
        t = q0 + jnp.arange(Q_BLOCK)
        qb = lax.dynamic_slice_in_dim(q, q0, Q_BLOCK, axis=1)
        qib = lax.dynamic_slice_in_dim(q_idx, q0, Q_BLOCK, axis=1)
        wib = lax.dynamic_slice_in_dim(w_idx, q0, Q_BLOCK, axis=1)
        score = jnp.einsum('bth,bths->bts', wib,
                           jax.nn.relu(jnp.einsum('bthd,bsd->bths', qib, k_idx)))
        score = jnp.where(key_pos[None, None, :] <= t[None, :, None],
                          score.astype(jnp.float32), NEG_INF)
        _, sel = lax.top_k(score, top_k)
        k_sel = k[b_idx, sel]
        v_sel = v[b_idx, sel]
        dist = t[None, :, None] - sel
        logits = jnp.einsum('bthd,btkd->bhtk', qb, k_sel)
        logits = logits + jnp.moveaxis(tbl[rel_bucket(dist)], -1, 1)
        p = masked_softmax(logits, (dist >= 0)[:, None], -1)
        o = jnp.einsum('bhtk,btkd->bthd', p, v_sel)
        return o.reshape(B, Q_BLOCK, H * dh).astype(q.dtype)

    out = lax.map(block, jnp.arange(S // Q_BLOCK))
    return out.transpose(1, 0, 2, 3).reshape(B, S, H * dh)


def _layer(x2, p2, B, S, rel_bias, g_mix, w_in, conv_w, nsa_cmp_pe, nsa_cmp_w1, nsa_cmp_w2,
           nsa_q_gain, nsa_k_gain, sgu_ln_gain, sgu_w, sgu_b, dsa_q_gain, dsa_k_gain,
           w_gate, w_branch, w_out, g_ffn, peer_wq, peer_subkeys, peer_u, peer_v,
           g_ple, w_ple_gate, w_ple_proj):
    T = B * S
    proj = rms_matmul(x2, g_mix[None], pack_w_in(w_in), tm=512, tn=768)
    grp = lambda name, width: proj[:, COL[name]:COL[name] + width].reshape(B, S, width)

    y_a = short_conv_mixer(grp("a_b", MIX_W), grp("a_c", MIX_W), grp("a_x", MIX_W), conv_w)
    y_b = nsa_mixer(grp("n_q", MIX_W).reshape(B, S, NSA_HEADS, HEAD_DIM),
                    grp("n_kv", 768).reshape(B, S, 6, NSA_KV_HEADS, HEAD_DIM),
                    grp("n_g", 24).reshape(B, S, 3, NSA_HEADS),
                    nsa_cmp_pe, nsa_cmp_w1, nsa_cmp_w2, nsa_q_gain, nsa_k_gain, rel_bias)
    y_c = sgu_mixer(grp("c_uv", 2 * MIX_W), sgu_ln_gain, sgu_w, sgu_b)
    y_d = dsa_attention(proj, COL, dsa_q_gain[None], dsa_k_gain[None],
                        rel_bias_tiles(rel_bias[:, NSA_HEADS:]), B, S)
    ys = [y.reshape(T, MIX_W) for y in (y_a, y_b, y_c, y_d)]

    x2 = merge_mixers(x2, g_mix[None], ys, w_gate.astype(BF16),
                      w_branch.reshape(N_MIXERS, MIX_W, D_MODEL).astype(BF16),
                      w_out.astype(BF16), tm=256)

    sk = peer_subkeys.reshape(2 * PEER_HEADS, PEER_KEYS, PEER_QDIM // 2).astype(BF16)
    x2 = peer_ffn_update(x2, g_ffn[None], peer_wq.astype(BF16), sk,
                         peer_u.astype(BF16), peer_v.astype(BF16).T)

    x2 = ple_update(x2, g_ple[None], p2, w_ple_gate.astype(BF16), w_ple_proj.astype(BF16), tm=512)
    return x2


def kernel(x, p, rel_bias, g_mix, w_in, conv_w, nsa_cmp_pe, nsa_cmp_w1, nsa_cmp_w2, nsa_q_gain,
           nsa_k_gain, sgu_ln_gain, sgu_w, sgu_b, dsa_q_gain, dsa_k_gain, w_gate, w_branch, w_out,
           g_ffn, peer_wq, peer_subkeys, peer_u, peer_v, g_ple, w_ple_gate, w_ple_proj):
    B, S, D = x.shape
    depth = p.shape[0]
    x2 = x.reshape(B * S, D)
    for l in range(depth):
        x2 = _layer(x2, p[l].reshape(B * S, PLE_DIM), B, S, rel_bias, g_mix[l], w_in[l], conv_w[l],
                    nsa_cmp_pe[l], nsa_cmp_w1[l], nsa_cmp_w2[l], nsa_q_gain[l], nsa_k_gain[l],
                    sgu_ln_gain[l], sgu_w[l], sgu_b[l], dsa_q_gain[l], dsa_k_gain[l],
                    w_gate[l], w_branch[l], w_out[l], g_ffn[l], peer_wq[l], peer_subkeys[l],
                    peer_u[l], peer_v[l], g_ple[l], w_ple_gate[l], w_ple_proj[l])
    return x2.reshape(B, S, D)
```

```python
import functools
import math

import jax
import jax.numpy as jnp
import numpy as np
from jax import lax
from jax.experimental import pallas as pl
from jax.experimental.pallas import tpu as pltpu

F32 = jnp.float32
BF16 = jnp.bfloat16

D_MODEL = 1024
HEAD_DIM = 64
N_MIXERS = 4
MIX_W = D_MODEL // 2
Q_BLOCK = 128
EPS = 1e-6
NEG_INF = -1e30
CONV_K = 3
NSA_HEADS = MIX_W // HEAD_DIM
NSA_KV_HEADS = 2
CMP_BLOCK = 32
CMP_STRIDE = 16
CMP_HIDDEN = 256
SEL_BLOCK = 64
SEL_TOPN = 8
WINDOW = 512
FORCED_SCORE = 1e4
SGU_CHUNK = 128
SGU_GROUPS = 8
DSA_HEADS = MIX_W // HEAD_DIM
IDX_HEADS = 8
IDX_DIM = 64
DSA_TOPK_MAX = 256
N_BUCKETS = 32
MAX_DISTANCE = 1024
PEER_HEADS = 8
PEER_KEYS = 128
PEER_QDIM = 128
PEER_TOPK = 16
N_EXPERTS = PEER_KEYS * PEER_KEYS
PLE_DIM = 256

SPLIT_WIDTHS = (
    MIX_W, MIX_W, MIX_W,
    NSA_HEADS * HEAD_DIM,
    6 * NSA_KV_HEADS * HEAD_DIM,
    3 * NSA_HEADS,
    2 * MIX_W,
    DSA_HEADS * HEAD_DIM, HEAD_DIM, HEAD_DIM,
    IDX_HEADS * IDX_DIM, IDX_DIM, IDX_HEADS,
)
IN_WIDTH = sum(SPLIT_WIDTHS)

COL = dict(a_b=0, a_c=512, a_x=1024, n_q=1536, d_q=2048, d_qi=2560, c_uv=3072, n_kv=4096,
           n_g=4864, d_kv=4992, d_kiw=5120)
N_PACK = 5376


def pack_w_in(w):
    o = dict(zip(("a_b", "a_c", "a_x", "n_q", "n_kv", "n_g", "c_uv", "d_q", "d_k", "d_v", "d_qi", "d_ki",
                  "d_wi", "end"), [0] + [int(c) for c in np.cumsum(SPLIT_WIDTHS)]))
    z = lambda n: jnp.zeros((w.shape[0], n), w.dtype)
    cols = [w[:, o["a_b"]:o["n_kv"]], w[:, o["d_q"]:o["d_k"]], w[:, o["d_qi"]:o["d_ki"]],
            w[:, o["c_uv"]:o["d_q"]], w[:, o["n_kv"]:o["n_g"]],
            w[:, o["n_g"]:o["c_uv"]], z(128 - 3 * NSA_HEADS),
            w[:, o["d_k"]:o["d_qi"]],
            w[:, o["d_ki"]:o["end"]], z(128 - IDX_DIM - IDX_HEADS)]
    packed = jnp.concatenate(cols, axis=1)
    return jnp.pad(packed, ((0, 0), (0, N_PACK - packed.shape[1]))).astype(BF16)

VMEM_LIMIT_BYTES = 56 * 1024 * 1024


def _cparams(sem):
    return pltpu.CompilerParams(dimension_semantics=sem, vmem_limit_bytes=VMEM_LIMIT_BYTES)


def _rms(x, g):
    return x * lax.rsqrt(jnp.mean(x * x, axis=-1, keepdims=True) + EPS) * g


def _rms_matmul_body(x_ref, g_ref, w_ref, o_ref, h_ref):
    @pl.when(pl.program_id(1) == 0)
    def _():
        h_ref[...] = _rms(x_ref[...], g_ref[...]).astype(BF16)

    o_ref[...] = jnp.dot(h_ref[...], w_ref[...], preferred_element_type=F32)


def rms_matmul(x, g, w, tm, tn):
    T, D = x.shape
    N = w.shape[1]
    return pl.pallas_call(
        _rms_matmul_body,
        grid=(T // tm, N // tn),
        in_specs=[pl.BlockSpec((tm, D), lambda i, j: (i, 0)),
                  pl.BlockSpec((1, D), lambda i, j: (0, 0)),
                  pl.BlockSpec((D, tn), lambda i, j: (0, j))],
        out_specs=pl.BlockSpec((tm, tn), lambda i, j: (i, j)),
        out_shape=jax.ShapeDtypeStruct((T, N), F32),
        scratch_shapes=[pltpu.VMEM((tm, D), BF16)],
        compiler_params=_cparams(("parallel", "arbitrary")),
        name="rms_matmul",
    )(x, g, w)


def _merge_body(x_ref, g_ref, ya_ref, yb_ref, yc_ref, yd_ref, wg_ref, wb_ref, wo_ref, o_ref):
    x = x_ref[...]
    h = _rms(x, g_ref[...]).astype(BF16)
    merged = jnp.zeros(x.shape, F32)
    for m, y_ref in enumerate((ya_ref, yb_ref, yc_ref, yd_ref)):
        z = jnp.dot(y_ref[...].astype(BF16), wb_ref[m], preferred_element_type=F32)
        gate = jax.nn.sigmoid(jnp.dot(h, wg_ref[:, m * D_MODEL:(m + 1) * D_MODEL],
                                      preferred_element_type=F32))
        merged = merged + gate * z
    o_ref[...] = x + jnp.dot(merged.astype(BF16), wo_ref[...], preferred_element_type=F32)


def merge_mixers(x, g, ys, w_gate, w_branch, w_out, tm):
    T, D = x.shape
    const = lambda *shape: pl.BlockSpec(shape, lambda i: (0,) * len(shape), pipeline_mode=pl.Buffered(1))
    return pl.pallas_call(
        _merge_body,
        grid=(T // tm,),
        in_specs=[pl.BlockSpec((tm, D), lambda i: (i, 0)),
                  const(1, D)]
                 + [pl.BlockSpec((tm, MIX_W), lambda i: (i, 0)) for _ in range(N_MIXERS)]
                 + [const(D, N_MIXERS * D), const(N_MIXERS, MIX_W, D), const(D, D)],
        out_specs=pl.BlockSpec((tm, D), lambda i: (i, 0)),
        out_shape=jax.ShapeDtypeStruct((T, D), F32),
        compiler_params=_cparams(("parallel",)),
        name="merge_mixers",
    )(x, g, *ys, w_gate, w_branch, w_out)


def _ple_body(x_ref, g_ref, p_ref, wg_ref, wp_ref, o_ref):
    x = x_ref[...]
    h = _rms(x, g_ref[...]).astype(BF16)
    gate = jax.nn.sigmoid(jnp.dot(h, wg_ref[...], preferred_element_type=F32))
    proj = jnp.dot(p_ref[...].astype(BF16), wp_ref[...], preferred_element_type=F32)
    o_ref[...] = x + gate * proj


def ple_update(x, g, p, w_gate, w_proj, tm):
    T, D = x.shape
    const = lambda *shape: pl.BlockSpec(shape, lambda i: (0,) * len(shape), pipeline_mode=pl.Buffered(1))
    return pl.pallas_call(
        _ple_body,
        grid=(T // tm,),
        in_specs=[pl.BlockSpec((tm, D), lambda i: (i, 0)), const(1, D),
                  pl.BlockSpec((tm, PLE_DIM), lambda i: (i, 0)),
                  const(D, D), const(PLE_DIM, D)],
        out_specs=pl.BlockSpec((tm, D), lambda i: (i, 0)),
        out_shape=jax.ShapeDtypeStruct((T, D), F32),
        compiler_params=_cparams(("parallel",)),
        name="ple_update",
    )(x, g, p, w_gate, w_proj)


def _extract_top16(cur, tops_ref):
    m = None
    for r in range(PEER_TOPK):
        m = jnp.max(cur, axis=0, keepdims=True)
        if tops_ref is not None:
            tops_ref[r:r + 1, :] = m
        cur = jnp.where(cur == m, NEG_INF, cur)
    return m


_CAND_ROWS_J = (16, 8, 5, 4, 3, 2, 2, 2)


def _peer_topk_body(x_ref, g_ref, wq_ref, sk_ref, hn_ref, s1_ref, s2_ref, thr_ref, ln_ref,
                    t1_ref, t2_ref):
    hn = _rms(x_ref[...], g_ref[...]).astype(BF16)
    hn_ref[...] = hn
    q = jnp.dot(hn, wq_ref[...], preferred_element_type=F32).astype(BF16)
    half = PEER_QDIM // 2
    tn = x_ref.shape[0]
    j_iota = lax.broadcasted_iota(jnp.int32, (8, tn), 0)
    for h in range(PEER_HEADS):
        s = []
        for p in range(2):
            c0 = (2 * h + p) * half
            s.append(lax.dot_general(sk_ref[2 * h + p], q[:, c0:c0 + half],
                                     (((1,), (1,)), ((), ())), preferred_element_type=F32))
        th1 = _extract_top16(s[0], t1_ref)
        th2 = _extract_top16(s[1], t2_ref)
        a = t1_ref[...]
        b = t2_ref[...]
        b8 = b[0:8]
        pieces = [a[0:1] + b]
        for i in range(1, 8):
            c = a[i:i + 1] + b8
            if _CAND_ROWS_J[i] < 8:
                c = jnp.where(j_iota < _CAND_ROWS_J[i], c, NEG_INF)
            pieces.append(c)
        pieces.append(a[8:16] + b[0:1])
        cand = jnp.concatenate(pieces, axis=0)
        top = a[0:1] + b[0:1]
        theta = _extract_top16(cand, None)
        z = jnp.sum(jnp.where(cand >= theta, jnp.exp(cand - top), 0.0), axis=0, keepdims=True)
        s1_ref[h] = jnp.where(s[0] >= th1, s[0], NEG_INF)
        s2_ref[h] = jnp.where(s[1] >= th2, s[1], NEG_INF)
        thr_ref[h:h + 1, :] = theta
        ln_ref[h:h + 1, :] = top + jnp.log(z)


def peer_topk(x, g, wq, subkeys, tn):
    T, D = x.shape
    const = lambda *shape: pl.BlockSpec(shape, lambda i: (0,) * len(shape), pipeline_mode=pl.Buffered(1))
    return pl.pallas_call(
        _peer_topk_body,
        grid=(T // tn,),
        in_specs=[pl.BlockSpec((tn, D), lambda i: (i, 0)), const(1, D), const(D, D),
                  const(2 * PEER_HEADS, PEER_KEYS, PEER_QDIM // 2)],
        out_specs=[pl.BlockSpec((tn, D), lambda i: (i, 0)),
                   pl.BlockSpec((PEER_HEADS, PEER_KEYS, tn), lambda i: (0, 0, i)),
                   pl.BlockSpec((PEER_HEADS, PEER_KEYS, tn), lambda i: (0, 0, i)),
                   pl.BlockSpec((PEER_HEADS, tn), lambda i: (0, i)),
                   pl.BlockSpec((PEER_HEADS, tn), lambda i: (0, i))],
        out_shape=[jax.ShapeDtypeStruct((T, D), BF16),
                   jax.ShapeDtypeStruct((PEER_HEADS, PEER_KEYS, T), F32),
                   jax.ShapeDtypeStruct((PEER_HEADS, PEER_KEYS, T), F32),
                   jax.ShapeDtypeStruct((PEER_HEADS, T), F32),
                   jax.ShapeDtypeStruct((PEER_HEADS, T), F32)],
        scratch_shapes=[pltpu.VMEM((PEER_TOPK, tn), F32), pltpu.VMEM((PEER_TOPK, tn), F32)],
        compiler_params=_cparams(("parallel",)),
        name="peer_topk",
    )(x, g, wq, subkeys)


LANE_CHUNK = 128


def _peer_main_body(hn_ref, s1_ref, s2_ref, thr_ref, ln_ref, u_ref, vt_ref, x_ref, o_ref,
                    acc_ref, p_ref, *, ab):
    j = pl.program_id(1)
    tn = hn_ref.shape[0]

    @pl.when(j == 0)
    def _():
        acc_ref[...] = jnp.zeros(acc_ref.shape, F32)

    hn = hn_ref[...]
    for k in range(ab):
        at = lax.dot_general(u_ref[k * PEER_KEYS:(k + 1) * PEER_KEYS, :], hn,
                             (((1,), (1,)), ((), ())), preferred_element_type=F32)
        for c in range(tn // LANE_CHUNK):
            ls = slice(c * LANE_CHUNK, (c + 1) * LANE_CHUNK)
            w = jnp.zeros((PEER_KEYS, LANE_CHUNK), F32)
            for h in range(PEER_HEADS):
                t0 = s2_ref[h, :, ls] + s1_ref[h, k, :, ls]
                w = w + jnp.where(t0 >= thr_ref[h:h + 1, ls], jnp.exp(t0 - ln_ref[h:h + 1, ls]), 0.0)
            p_ref[k * PEER_KEYS:(k + 1) * PEER_KEYS, ls] = (w * jax.nn.gelu(at[:, ls])).astype(BF16)
    acc_ref[...] += jnp.dot(vt_ref[...], p_ref[...], preferred_element_type=F32)

    @pl.when(j == pl.num_programs(1) - 1)
    def _():
        o_ref[...] = x_ref[...] + acc_ref[...].T


def peer_main(hn, s1, s2, thr, ln, u, vt, x, tn, ab):
    T, D = x.shape
    e_blk = ab * PEER_KEYS
    return pl.pallas_call(
        functools.partial(_peer_main_body, ab=ab),
        grid=(T // tn, PEER_KEYS // ab),
        in_specs=[pl.BlockSpec((tn, D), lambda i, j: (i, 0)),
                  pl.BlockSpec((PEER_HEADS, ab, 1, tn), lambda i, j: (0, j, 0, i)),
                  pl.BlockSpec((PEER_HEADS, PEER_KEYS, tn), lambda i, j: (0, 0, i)),
                  pl.BlockSpec((PEER_HEADS, tn), lambda i, j: (0, i)),
                  pl.BlockSpec((PEER_HEADS, tn), lambda i, j: (0, i)),
                  pl.BlockSpec((e_blk, D), lambda i, j: (j, 0)),
                  pl.BlockSpec((D, e_blk), lambda i, j: (0, j)),
                  pl.BlockSpec((tn, D), lambda i, j: (i, 0))],
        out_specs=pl.BlockSpec((tn, D), lambda i, j: (i, 0)),
        out_shape=jax.ShapeDtypeStruct((T, D), F32),
        scratch_shapes=[pltpu.VMEM((D, tn), F32), pltpu.VMEM((e_blk, tn), BF16)],
        compiler_params=_cparams(("parallel", "arbitrary")),
        name="peer_main",
    )(hn, s1, s2, thr, ln, u, vt, x)


def peer_ffn_update(x, g, wq, subkeys, u, vt):
    hn, s1, s2, thr, ln = peer_topk(x, g, wq, subkeys, tn=256)
    s1 = s1.reshape(PEER_HEADS, PEER_KEYS, 1, x.shape[0])
    return peer_main(hn, s1, s2, thr, ln, u, vt, x, tn=512, ab=4)


BIAS_TILES = 9


def rel_bias_tiles(tbl):
    H = tbl.shape[1]
    d = jnp.arange(BIAS_TILES)[:, None, None]
    s = jnp.arange(Q_BLOCK)[None, :, None]
    t = jnp.arange(Q_BLOCK)[None, None, :]
    bias = tbl[rel_bucket(d * Q_BLOCK + t - s)]
    return jnp.moveaxis(bias, -1, 2).reshape(BIAS_TILES, Q_BLOCK, H * Q_BLOCK)


INT_MIN = -2 ** 31
NEG_INF_KEY = -1900671691


def _sortable_key(x):
    bits = pltpu.bitcast(x, jnp.int32)
    return jnp.where(bits < 0, bits ^ jnp.int32(0x7FFFFFFF), bits)


def _head_rms(x, g, scale):
    return x * lax.rsqrt(jnp.mean(x * x, axis=-1, keepdims=True) + EPS) * g * scale


def _dsa_body(kv_ref, kiw_ref, q_ref, qi_ref, wq_ref, qg_ref, kg_ref, bias_ref, o_ref,
              kn_scr, ki_scr, vt_scr, qall_scr, qiall_scr, sc_scr, th_scr, cnt_scr,
              m_scr, l_scr, acc_scr, *, top_k):
    i = pl.program_id(1)
    dh = HEAD_DIM
    QB = Q_BLOCK

    @pl.when(i == 0)
    def _prep():
        kv = kv_ref[...]
        kn_scr[...] = _head_rms(kv[:, :dh], kg_ref[...], 1.0).astype(BF16)
        vt_scr[...] = kv[:, dh:].T.astype(BF16)
        ki_scr[...] = kiw_ref[:, :dh].astype(BF16)

    q = q_ref[...]
    qi = qi_ref[...]
    for h in range(DSA_HEADS):
        qh = _head_rms(q[:, h * dh:(h + 1) * dh], qg_ref[...], dh ** -0.5)
        qall_scr[h * QB:(h + 1) * QB, :] = qh.astype(BF16)
        qiall_scr[h * QB:(h + 1) * QB, :] = qi[:, h * dh:(h + 1) * dh].astype(BF16)
    w_t = (wq_ref[:, dh:dh + IDX_HEADS] * (IDX_HEADS ** -0.5)).T

    s_loc = lax.broadcasted_iota(jnp.int32, (QB, QB), 0)
    t_loc = lax.broadcasted_iota(jnp.int32, (QB, QB), 1)
    n_tiles = i + 1
    nt_dims = (((1,), (1,)), ((), ()))

    def score_tile(kt, carry):
        kit = ki_scr[pl.ds(pl.multiple_of(kt * QB, QB), QB), :]
        sc = jnp.zeros((QB, QB), F32)
        for h in range(IDX_HEADS):
            r = lax.dot_general(kit, qiall_scr[h * QB:(h + 1) * QB, :], nt_dims,
                                preferred_element_type=F32)
            sc = sc + w_t[h:h + 1, :] * jnp.maximum(r, 0.0)
        sc = jnp.where(sc == 0.0, 0.0, sc)
        sc = jnp.where(s_loc + kt * QB <= t_loc + i * QB, sc, NEG_INF)
        sc_scr[pl.ds(pl.multiple_of(kt * QB, QB), QB), :] = _sortable_key(sc)
        return carry

    lax.fori_loop(0, n_tiles, score_tile, 0)

    def count_ge(cand):
        def body(kt, c):
            tile = sc_scr[pl.ds(pl.multiple_of(kt * QB, QB), QB), :]
            return c + jnp.sum((tile >= cand).astype(jnp.int32), axis=0, keepdims=True)
        return lax.fori_loop(0, n_tiles, body, jnp.zeros((1, QB), jnp.int32))

    th_scr[...] = jnp.full((1, QB), NEG_INF_KEY + 1, jnp.int32)

    @pl.when(n_tiles * QB > top_k)
    def _select():
        def bit_body(it, carry):
            ans_u, cnt_ans = carry
            cand_u = ans_u | (jnp.int32(1) << (31 - it))
            c = count_ge(cand_u ^ jnp.int32(INT_MIN))
            ok = c >= top_k
            return jnp.where(ok, cand_u, ans_u), jnp.where(ok, c, cnt_ans)

        ans_u, cnt = lax.fori_loop(
            0, 32, bit_body,
            (jnp.zeros((1, QB), jnp.int32), jnp.full((1, QB), 1, jnp.int32) * (n_tiles * QB)))
        th_scr[...] = ans_u ^ jnp.int32(INT_MIN)
        cnt_scr[...] = cnt

        @pl.when(jnp.max(cnt) > top_k)
        def _ties():
            theta = th_scr[...]
            need = (top_k - count_ge(theta + 1)).astype(F32)
            tri = (s_loc >= t_loc).astype(BF16)

            def body(kt, seen):
                rows = pl.ds(pl.multiple_of(kt * QB, QB), QB)
                tile = sc_scr[rows, :]
                eq = tile == theta
                rank = seen + jnp.dot(tri, eq.astype(BF16), preferred_element_type=F32)
                sc_scr[rows, :] = jnp.where(eq & (rank > need), theta - 1, tile)
                return seen + jnp.sum(eq.astype(F32), axis=0, keepdims=True)

            lax.fori_loop(0, n_tiles, body, jnp.zeros((1, QB), F32))

    theta = th_scr[...]
    m_scr[...] = jnp.full(m_scr.shape, NEG_INF, F32)
    l_scr[...] = jnp.zeros(l_scr.shape, F32)
    acc_scr[...] = jnp.zeros(acc_scr.shape, F32)

    def att_tile(kt, carry):
        rows = pl.ds(pl.multiple_of(kt * QB, QB), QB)
        _attend_tile(kn_scr[rows, :], vt_scr[:, rows], qall_scr, bias_ref,
                     jnp.minimum(i - kt, BIAS_TILES - 1), sc_scr[rows, :] >= theta,
                     m_scr, l_scr, acc_scr)
        return carry

    lax.fori_loop(0, n_tiles, att_tile, 0)

    o_t = acc_scr[...] / jnp.maximum(l_scr[...], 1e-30)
    o_ref[...] = jnp.concatenate([o_t[:, h * QB:(h + 1) * QB].T for h in range(DSA_HEADS)], axis=1)


def dsa_attention(proj, col, q_gain, k_gain, bias, B, S):
    nq = S // Q_BLOCK
    top_k = min(DSA_TOPK_MAX, S // 4)
    HQ = DSA_HEADS * Q_BLOCK
    const = lambda *shape: pl.BlockSpec(shape, lambda b, i: (0,) * len(shape), pipeline_mode=pl.Buffered(1))
    return pl.pallas_call(
        functools.partial(_dsa_body, top_k=top_k),
        grid=(B, nq),
        in_specs=[pl.BlockSpec((S, 128), lambda b, i: (b, col["d_kv"] // 128)),
                  pl.BlockSpec((S, 128), lambda b, i: (b, col["d_kiw"] // 128)),
                  pl.BlockSpec((Q_BLOCK, MIX_W), lambda b, i: (b * nq + i, col["d_q"] // MIX_W)),
                  pl.BlockSpec((Q_BLOCK, MIX_W), lambda b, i: (b * nq + i, col["d_qi"] // MIX_W)),
                  pl.BlockSpec((Q_BLOCK, 128), lambda b, i: (b * nq + i, col["d_kiw"] // 128)),
                  const(1, HEAD_DIM), const(1, HEAD_DIM), const(BIAS_TILES, Q_BLOCK, HQ)],
        out_specs=pl.BlockSpec((Q_BLOCK, MIX_W), lambda b, i: (b * nq + i, 0)),
        out_shape=jax.ShapeDtypeStruct((B * S, MIX_W), F32),
        scratch_shapes=[pltpu.VMEM((S, HEAD_DIM), BF16), pltpu.VMEM((S, HEAD_DIM), BF16),
                        pltpu.VMEM((HEAD_DIM, S), BF16),
                        pltpu.VMEM((HQ, HEAD_DIM), BF16), pltpu.VMEM((HQ, HEAD_DIM), BF16),
                        pltpu.VMEM((S, Q_BLOCK), jnp.int32),
                        pltpu.VMEM((1, Q_BLOCK), jnp.int32), pltpu.VMEM((1, Q_BLOCK), jnp.int32),
                        pltpu.VMEM((1, HQ), F32), pltpu.VMEM((1, HQ), F32), pltpu.VMEM((HEAD_DIM, HQ), F32)],
        compiler_params=_cparams(("parallel", "arbitrary")),
        name="dsa_attention",
    )(proj, proj, proj, proj, proj, q_gain, k_gain, bias)


NSA_G = NSA_HEADS // NSA_KV_HEADS
CMP_PAD = 56
CMP_NEAR = 64
SEL_TAKEN = -3e38


def _cmp_rows(S):
    return -(-(CMP_PAD + S // CMP_STRIDE) // 64) * 64


def _nsa_prep_body(ks_ref, vs_ref, kw_ref, vw_ref, kg_ref, kso_ref, vso_ref, kwo_ref, vwo_ref):
    dh = HEAD_DIM
    for g in range(NSA_KV_HEADS):
        cs = slice(g * dh, (g + 1) * dh)
        kso_ref[0, g] = _head_rms(ks_ref[:, cs], kg_ref[1:2, :], 1.0).astype(BF16)
        kwo_ref[0, g] = _head_rms(kw_ref[:, cs], kg_ref[2:3, :], 1.0).astype(BF16)
        vso_ref[0, g] = vs_ref[:, cs].T.astype(BF16)
        vwo_ref[0, g] = vw_ref[:, cs].T.astype(BF16)


def nsa_prep(proj, k_gain, B, S, ts=512):
    nt = S // ts
    c0 = COL["n_kv"] // 128
    col = lambda j: pl.BlockSpec((ts, 128), lambda b, t: (b * nt + t, c0 + j))
    k_out = pl.BlockSpec((1, NSA_KV_HEADS, ts, HEAD_DIM), lambda b, t: (b, 0, t, 0))
    v_out = pl.BlockSpec((1, NSA_KV_HEADS, HEAD_DIM, ts), lambda b, t: (b, 0, 0, t))
    k_sds = jax.ShapeDtypeStruct((B, NSA_KV_HEADS, S, HEAD_DIM), BF16)
    v_sds = jax.ShapeDtypeStruct((B, NSA_KV_HEADS, HEAD_DIM, S), BF16)
    return pl.pallas_call(
        _nsa_prep_body,
        grid=(B, nt),
        in_specs=[col(2), col(3), col(4), col(5), pl.BlockSpec((3, HEAD_DIM), lambda b, t: (0, 0))],
        out_specs=[k_out, v_out, k_out, v_out],
        out_shape=[k_sds, v_sds, k_sds, v_sds],
        compiler_params=_cparams(("parallel", "parallel")),
        name="nsa_prep",
    )(proj, proj, proj, proj, k_gain)


def _nsa_compress_body(xk_ref, xv_ref, pe_ref, w1_ref, w2_ref, kg_ref, kc_ref, vct_ref):
    R = xk_ref.shape[3]
    ncp = kc_ref.shape[2]
    half = CMP_STRIDE * HEAD_DIM
    row = lax.broadcasted_iota(jnp.int32, (R, HEAD_DIM), 0)
    for j, x_ref in enumerate((xk_ref, xv_ref)):
        x = x_ref[0, 0, 0]
        pe = pe_ref[j]
        lo = jnp.dot((x + pe[:, :half]).astype(BF16), w1_ref[j, :half, :], preferred_element_type=F32)
        hi = jnp.dot((x + pe[:, half:]).astype(BF16), w1_ref[j, half:, :], preferred_element_type=F32)
        hid = jax.nn.gelu(lo + pltpu.roll(hi, R - 1, 0))
        c = jnp.dot(hid.astype(BF16), w2_ref[j], preferred_element_type=F32)
        if j == 0:
            c = _head_rms(c, kg_ref[0:1, :], 1.0)
        c = jnp.where(row < R - 1, c, 0.0)
        c = jnp.concatenate([jnp.zeros((CMP_PAD, HEAD_DIM), F32), c,
                             jnp.zeros((ncp - CMP_PAD - R, HEAD_DIM), F32)], axis=0)
        if j == 0:
            kc_ref[0, 0] = c.astype(BF16)
        else:
            vct_ref[0, 0] = c.T.astype(BF16)


def nsa_compress(xc, pe, w1, w2, k_gain, S):
    B = xc.shape[0]
    R = S // CMP_STRIDE
    ncp = _cmp_rows(S)
    const = lambda *shape: pl.BlockSpec(shape, lambda b, g: (0,) * len(shape), pipeline_mode=pl.Buffered(1))
    return pl.pallas_call(
        _nsa_compress_body,
        grid=(B, NSA_KV_HEADS),
        in_specs=[pl.BlockSpec((1, 1, 1, R, CMP_STRIDE * HEAD_DIM), lambda b, g: (b, 0, g, 0, 0)),
                  pl.BlockSpec((1, 1, 1, R, CMP_STRIDE * HEAD_DIM), lambda b, g: (b, 1, g, 0, 0)),
                  const(2, 1, CMP_BLOCK * HEAD_DIM), const(2, CMP_BLOCK * HEAD_DIM, CMP_HIDDEN),
                  const(2, CMP_HIDDEN, HEAD_DIM), const(3, HEAD_DIM)],
        out_specs=[pl.BlockSpec((1, 1, ncp, HEAD_DIM), lambda b, g: (b, g, 0, 0)),
                   pl.BlockSpec((1, 1, HEAD_DIM, ncp), lambda b, g: (b, g, 0, 0))],
        out_shape=[jax.ShapeDtypeStruct((B, NSA_KV_HEADS, ncp, HEAD_DIM), BF16),
                   jax.ShapeDtypeStruct((B, NSA_KV_HEADS, HEAD_DIM, ncp), BF16)],
        compiler_params=_cparams(("parallel", "parallel")),
        name="nsa_compress",
    )(xc, xc, pe, w1, w2, k_gain)


def _attend_tile(k_t, v_t, q_scr, bias_ref, d, mask, m_scr, l_scr, acc_scr):
    QB = Q_BLOCK
    CH = 2 * QB
    mask2 = jnp.concatenate([mask, mask], axis=1)
    for c in range(q_scr.shape[0] // CH):
        ls = slice(c * CH, (c + 1) * CH)
        lg = lax.dot_general(k_t, q_scr[ls, :], (((1,), (1,)), ((), ())), preferred_element_type=F32)
        lg = jnp.where(mask2, lg + bias_ref[d, :, ls], NEG_INF)
        m_old = m_scr[:, ls]
        m_new = jnp.maximum(m_old, jnp.max(lg, axis=0, keepdims=True))
        p = jnp.where(mask2, jnp.exp(lg - m_new), 0.0)
        alpha = jnp.exp(m_old - m_new)
        l_scr[:, ls] = alpha * l_scr[:, ls] + jnp.sum(p, axis=0, keepdims=True)
        acc_scr[:, ls] = alpha * acc_scr[:, ls] + jnp.dot(v_t, p.astype(BF16), preferred_element_type=F32)
        m_scr[:, ls] = m_new


def _nsa_body(q_ref, gl_ref, kc_ref, vct_ref, ks_ref, vst_ref, kw_ref, vwt_ref, qg_ref, bias_ref,
              gcd_ref, c31_ref, ovl_ref, o_ref,
              qall_scr, lc_scr, sel_scr, gt_scr, ms_scr, ls_scr, as_scr, mw_scr, lw_scr, aw_scr,
              *, n_sel, top_n):
    g = pl.program_id(1)
    i = pl.program_id(2)
    dh = HEAD_DIM
    QB = Q_BLOCK
    GQ = NSA_G * QB
    ncp = kc_ref.shape[2]

    q = q_ref[...]
    for r in range(NSA_G):
        qall_scr[r * QB:(r + 1) * QB, :] = _head_rms(q[:, r * dh:(r + 1) * dh], qg_ref[...],
                                                     dh ** -0.5).astype(BF16)
    gt_scr[...] = jax.nn.sigmoid(gl_ref[...]).T

    lc_scr[...] = lax.dot_general(kc_ref[0, 0], qall_scr[...], (((1,), (1,)), ((), ())),
                                  preferred_element_type=F32) + c31_ref[0]
    near = pl.ds(pl.multiple_of(i * (QB // CMP_STRIDE), 8), CMP_NEAR)
    lc_scr[near, :] = lc_scr[near, :] + gcd_ref[0]
    n_p = lax.broadcasted_iota(jnp.int32, (ncp, GQ), 0)
    t_c = i * QB + (lax.broadcasted_iota(jnp.int32, (ncp, GQ), 1) & (QB - 1))
    cmp_end = jnp.where(n_p >= CMP_PAD, (n_p - CMP_PAD) * CMP_STRIDE + (CMP_BLOCK - 1), 1 << 30)
    mask_c = cmp_end <= t_c
    lc = jnp.where(mask_c, lc_scr[...], NEG_INF)
    e = jnp.where(mask_c, jnp.exp(lc - jnp.max(lc, axis=0, keepdims=True)), 0.0)
    pc = (e / jnp.maximum(jnp.sum(e, axis=0, keepdims=True), 1e-30)).astype(BF16)
    o_c = jnp.dot(vct_ref[0, 0], pc, preferred_element_type=F32)

    imp = jnp.zeros((n_sel, QB), F32)
    for r in range(NSA_G):
        imp = imp + jnp.dot(ovl_ref[...], pc[:, r * QB:(r + 1) * QB], preferred_element_type=F32)
    j_io = lax.broadcasted_iota(jnp.int32, (n_sel, QB), 0).astype(F32)
    cur = ((i * QB + lax.broadcasted_iota(jnp.int32, (n_sel, QB), 1)) >> 6).astype(F32)
    imp = jnp.where((j_io == 0) | (j_io == cur), FORCED_SCORE, imp)
    imp = jnp.where(j_io <= cur, imp, NEG_INF)
    sel = jnp.zeros((n_sel, QB), F32)
    for _ in range(top_n):
        mx = jnp.max(imp, axis=0, keepdims=True)
        first = jnp.min(jnp.where(imp == mx, j_io, float(n_sel)), axis=0, keepdims=True)
        hit = j_io == first
        sel = jnp.where(hit, 1.0, sel)
        imp = jnp.where(hit, SEL_TAKEN, imp)
    sel_scr[...] = sel

    for m_scr, l_scr, a_scr in ((ms_scr, ls_scr, as_scr), (mw_scr, lw_scr, aw_scr)):
        m_scr[...] = jnp.full(m_scr.shape, NEG_INF, F32)
        l_scr[...] = jnp.zeros(l_scr.shape, F32)
        a_scr[...] = jnp.zeros(a_scr.shape, F32)
    s_loc = lax.broadcasted_iota(jnp.int32, (QB, QB), 0)
    t_loc = lax.broadcasted_iota(jnp.int32, (QB, QB), 1)
    blocks_per_tile = QB // SEL_BLOCK

    def slc_tile(kt, carry):
        rows = pl.ds(pl.multiple_of(kt * QB, QB), QB)
        picked = jnp.where(s_loc < SEL_BLOCK, sel_scr[pl.ds(kt * blocks_per_tile, 1), :],
                           sel_scr[pl.ds(kt * blocks_per_tile + 1, 1), :])
        dist = (t_loc + i * QB) - (s_loc + kt * QB)
        mask = jnp.where(dist >= 0, picked, 0.0) > 0.5
        _attend_tile(ks_ref[0, 0, rows, :], vst_ref[0, 0, :, rows], qall_scr, bias_ref,
                     jnp.minimum(i - kt, BIAS_TILES - 1), mask, ms_scr, ls_scr, as_scr)
        return carry

    lax.fori_loop(0, i + 1, slc_tile, 0)

    def win_tile(kt, carry):
        rows = pl.ds(pl.multiple_of(kt * QB, QB), QB)
        dist = (t_loc + i * QB) - (s_loc + kt * QB)
        mask = (dist >= 0) & (dist < WINDOW)
        _attend_tile(kw_ref[0, 0, rows, :], vwt_ref[0, 0, :, rows], qall_scr, bias_ref,
                     i - kt, mask, mw_scr, lw_scr, aw_scr)
        return carry

    lax.fori_loop(jnp.maximum(i - WINDOW // QB, 0), i + 1, win_tile, 0)

    o_s = as_scr[...] / jnp.maximum(ls_scr[...], 1e-30)
    o_w = aw_scr[...] / jnp.maximum(lw_scr[...], 1e-30)
    outs = []
    for r in range(NSA_G):
        ls = slice(r * QB, (r + 1) * QB)
        gate = lambda j: gt_scr[pl.ds(j * NSA_HEADS + g * NSA_G + r, 1), :]
        o_r = gate(0) * o_c[:, ls] + gate(1) * o_s[:, ls] + gate(2) * o_w[:, ls]
        outs.append(o_r.T)
    o_ref[...] = jnp.concatenate(outs, axis=1)


def nsa_attention(proj, kc, vct, ks, vst, kw, vwt, q_gain, bias, gcd, c31, ovl, B, S):
    nq = S // Q_BLOCK
    GQ = NSA_G * Q_BLOCK
    GW = NSA_G * HEAD_DIM
    ncp = kc.shape[2]
    n_sel = S // SEL_BLOCK
    const = lambda *shape: pl.BlockSpec(shape, lambda b, g, i: (0,) * len(shape), pipeline_mode=pl.Buffered(1))
    per_group = lambda *shape: pl.BlockSpec((1, 1) + shape, lambda b, g, i: (b, g, 0, 0))
    return pl.pallas_call(
        functools.partial(_nsa_body, n_sel=n_sel, top_n=min(SEL_TOPN, n_sel)),
        grid=(B, NSA_KV_HEADS, nq),
        in_specs=[pl.BlockSpec((Q_BLOCK, GW), lambda b, g, i: (b * nq + i, COL["n_q"] // GW + g)),
                  pl.BlockSpec((Q_BLOCK, 128), lambda b, g, i: (b * nq + i, COL["n_g"] // 128)),
                  per_group(ncp, HEAD_DIM), per_group(HEAD_DIM, ncp),
                  per_group(S, HEAD_DIM), per_group(HEAD_DIM, S),
                  per_group(S, HEAD_DIM), per_group(HEAD_DIM, S),
                  const(1, HEAD_DIM),
                  pl.BlockSpec((BIAS_TILES, Q_BLOCK, GQ), lambda b, g, i: (0, 0, g)),
                  pl.BlockSpec((1, CMP_NEAR, GQ), lambda b, g, i: (g, 0, 0)),
                  pl.BlockSpec((1, 1, GQ), lambda b, g, i: (g, 0, 0)),
                  const(n_sel, ncp)],
        out_specs=pl.BlockSpec((Q_BLOCK, GW), lambda b, g, i: (b * nq + i, g)),
        out_shape=jax.ShapeDtypeStruct((B * S, MIX_W), F32),
        scratch_shapes=[pltpu.VMEM((GQ, HEAD_DIM), BF16), pltpu.VMEM((ncp, GQ), F32),
                        pltpu.VMEM((n_sel, Q_BLOCK), F32), pltpu.VMEM((128, Q_BLOCK), F32)]
                       + [pltpu.VMEM((1, GQ), F32), pltpu.VMEM((1, GQ), F32), pltpu.VMEM((HEAD_DIM, GQ), F32)] * 2,
        compiler_params=_cparams(("parallel", "parallel", "arbitrary")),
        name="nsa_attention",
    )(proj, proj, kc, vct, ks, vst, kw, vwt, q_gain, bias, gcd, c31, ovl)


def nsa_tables(tbl, S):
    ncp = _cmp_rows(S)
    n_rel = jnp.arange(CMP_NEAR)[:, None] - CMP_PAD
    t_loc = jnp.arange(Q_BLOCK)[None, :]
    near = tbl[rel_bucket(t_loc - CMP_STRIDE * n_rel - (CMP_BLOCK - 1))]
    far = tbl[N_BUCKETS - 1]
    gcd = jnp.moveaxis(near - far, -1, 0).reshape(NSA_KV_HEADS, NSA_G, CMP_NEAR, Q_BLOCK)
    gcd = jnp.moveaxis(gcd, 1, 2).reshape(NSA_KV_HEADS, CMP_NEAR, NSA_G * Q_BLOCK)
    c31 = jnp.repeat(far, Q_BLOCK).reshape(NSA_KV_HEADS, 1, NSA_G * Q_BLOCK)
    n = np.arange(ncp) - CMP_PAD
    sel_start = np.arange(S // SEL_BLOCK) * SEL_BLOCK
    start = n * CMP_STRIDE
    ovl = ((start[None, :] < sel_start[:, None] + SEL_BLOCK) & (start[None, :] + CMP_BLOCK > sel_start[:, None])
           & (n[None, :] >= 0) & (n[None, :] < (S - CMP_BLOCK) // CMP_STRIDE + 1))
    return gcd, c31, jnp.asarray(ovl, BF16)


def nsa_mixer_pallas(proj, cmp_pe, cmp_w1, cmp_w2, q_gain, k_gain, rel_tbl, bias, B, S):
    dh = HEAD_DIM
    c0 = COL["n_kv"]
    xc = proj[:, c0:c0 + 2 * NSA_KV_HEADS * dh].reshape(B, S // CMP_STRIDE, CMP_STRIDE, 2, NSA_KV_HEADS, dh)
    xc = xc.transpose(0, 3, 4, 1, 2, 5).reshape(B, 2, NSA_KV_HEADS, S // CMP_STRIDE, CMP_STRIDE * dh)
    kc, vct = nsa_compress(xc, cmp_pe.reshape(2, 1, CMP_BLOCK * dh), cmp_w1.astype(BF16),
                           cmp_w2.astype(BF16), k_gain, S)
    ks, vst, kw, vwt = nsa_prep(proj, k_gain, B, S)
    gcd, c31, ovl = nsa_tables(rel_tbl, S)
    return nsa_attention(proj, kc, vct, ks, vst, kw, vwt, q_gain[None], bias, gcd, c31, ovl, B, S)


def rmsnorm(x, g):
    xf = x.astype(jnp.float32)
    xf = xf * lax.rsqrt(jnp.mean(xf * xf, axis=-1, keepdims=True) + EPS)
    return (xf * g.astype(jnp.float32)).astype(x.dtype)


def layernorm(x, g):
    xf = x.astype(jnp.float32)
    xf = xf - jnp.mean(xf, axis=-1, keepdims=True)
    xf = xf * lax.rsqrt(jnp.mean(xf * xf, axis=-1, keepdims=True) + EPS)
    return (xf * g.astype(jnp.float32)).astype(x.dtype)


def rel_bucket(dist):
    n = jnp.maximum(dist, 0)
    max_exact = N_BUCKETS // 2
    nf = jnp.maximum(n, 1).astype(jnp.float32)
    large = max_exact + (jnp.log(nf / max_exact) / math.log(MAX_DISTANCE / max_exact)
                         * (N_BUCKETS - max_exact)).astype(jnp.int32)
    return jnp.where(n < max_exact, n, jnp.minimum(large, N_BUCKETS - 1))


def masked_softmax(logits, mask, axis):
    lf = jnp.where(mask, logits.astype(jnp.float32), NEG_INF)
    m = jnp.max(lf, axis=axis, keepdims=True)
    e = jnp.where(mask, jnp.exp(lf - m), 0.0)
    return e / jnp.maximum(jnp.sum(e, axis=axis, keepdims=True), 1e-30)


def short_conv_mixer(b_gate, c_gate, xv, conv_w):
    S = xv.shape[1]
    bx = c_gate * xv
    xp = jnp.pad(bx, ((0, 0), (CONV_K - 1, 0), (0, 0)))
    conv = sum(conv_w[k] * xp[:, k:k + S] for k in range(CONV_K))
    return b_gate * conv


def nsa_mixer(q, kv, gate_logits, cmp_pe, cmp_w1, cmp_w2, q_gain, k_gain, rel_tbl):
    B, S = q.shape[:2]
    H, Hkv, dh = NSA_HEADS, NSA_KV_HEADS, HEAD_DIM
    G = H // Hkv
    q = rmsnorm(q, q_gain) * (dh ** -0.5)
    k_c, v_c, k_s, v_s, k_w, v_w = (kv[:, :, i] for i in range(6))
    k_s = rmsnorm(k_s, k_gain[1])
    k_w = rmsnorm(k_w, k_gain[2])

    n_cmp = (S - CMP_BLOCK) // CMP_STRIDE + 1
    cmp_idx = np.arange(n_cmp)[:, None] * CMP_STRIDE + np.arange(CMP_BLOCK)[None, :]

    def compress(t, j):
        blk = t[:, cmp_idx] + cmp_pe[j][None, None, :, None, :]
        blk = blk.transpose(0, 1, 3, 2, 4).reshape(B, n_cmp, Hkv, CMP_BLOCK * dh)
        return jax.nn.gelu(blk @ cmp_w1[j]) @ cmp_w2[j]

    k_c = rmsnorm(compress(k_c, 0), k_gain[0])
    v_c = compress(v_c, 1)
    cmp_end = jnp.asarray(cmp_idx[:, -1], jnp.int32)

    n_sel = S // SEL_BLOCK
    sel_start = np.arange(n_sel) * SEL_BLOCK
    cmp_start = cmp_idx[:, 0]
    overlap = jnp.asarray((cmp_start[:, None] < sel_start[None, :] + SEL_BLOCK)
                          & (cmp_start[:, None] + CMP_BLOCK > sel_start[None, :]), jnp.float32)
    ks_blk = k_s.reshape(B, n_sel, SEL_BLOCK, Hkv, dh).transpose(0, 3, 1, 2, 4)
    vs_blk = v_s.reshape(B, n_sel, SEL_BLOCK, Hkv, dh).transpose(0, 3, 1, 2, 4)
    top_n = min(SEL_TOPN, n_sel)

    kw_p = jnp.pad(k_w, ((0, 0), (WINDOW, 0), (0, 0), (0, 0)))
    vw_p = jnp.pad(v_w, ((0, 0), (WINDOW, 0), (0, 0), (0, 0)))

    tbl = rel_tbl[:, :NSA_HEADS].reshape(N_BUCKETS, Hkv, G)
    b_idx = jnp.arange(B)[:, None, None]
    g_idx = jnp.arange(Hkv)[None, :, None]

    def block(i):
        q0 = i * Q_BLOCK
        t = q0 + jnp.arange(Q_BLOCK)
        qb = lax.dynamic_slice_in_dim(q, q0, Q_BLOCK, axis=1).reshape(B, Q_BLOCK, Hkv, G, dh)
        gb = lax.dynamic_slice_in_dim(gate_logits, q0, Q_BLOCK, axis=1)

        dist_c = t[:, None] - cmp_end[None, :]
        lc = jnp.einsum('btgrd,bngd->bgrtn', qb, k_c)
        lc = lc + tbl[rel_bucket(dist_c)].transpose(2, 3, 0, 1)[None]
        pc = masked_softmax(lc, dist_c >= 0, -1)
        o_c = jnp.einsum('bgrtn,bngd->btgrd', pc, v_c)

        imp = jnp.einsum('bgrtn,nj->bgtj', pc, overlap)
        cur = t // SEL_BLOCK
        j = jnp.arange(n_sel)
        forced = (j[None, :] == 0) | (j[None, :] == cur[:, None])
        imp = jnp.where(forced, FORCED_SCORE, imp)
        imp = jnp.where(j[None, :] <= cur[:, None], imp, NEG_INF)
        _, sel = lax.top_k(imp, top_n)
        sel_f = sel.reshape(B, Hkv, Q_BLOCK * top_n)
        k_sel = ks_blk[b_idx, g_idx, sel_f].reshape(B, Hkv, Q_BLOCK, top_n, SEL_BLOCK, dh)
        v_sel = vs_blk[b_idx, g_idx, sel_f].reshape(B, Hkv, Q_BLOCK, top_n, SEL_BLOCK, dh)
        pos_s = sel[..., None] * SEL_BLOCK + jnp.arange(SEL_BLOCK)
        dist_s = t[None, None, :, None, None] - pos_s
        ls = jnp.einsum('btgrd,bgtnkd->bgrtnk', qb, k_sel)
        bias_s = tbl[rel_bucket(dist_s), jnp.arange(Hkv)[None, :, None, None, None]]
        ls = ls + jnp.moveaxis(bias_s, -1, 2)
        ps = masked_softmax(ls, (dist_s >= 0)[:, :, None], (-2, -1))
        o_s = jnp.einsum('bgrtnk,bgtnkd->btgrd', ps, v_sel)

        kwb = lax.dynamic_slice_in_dim(kw_p, q0, Q_BLOCK + WINDOW, axis=1)
        vwb = lax.dynamic_slice_in_dim(vw_p, q0, Q_BLOCK + WINDOW, axis=1)
        pos_w = q0 - WINDOW + jnp.arange(Q_BLOCK + WINDOW)
        dist_w = t[:, None] - pos_w[None, :]
        mask_w = (dist_w >= 0) & (dist_w < WINDOW) & (pos_w[None, :] >= 0)
        lw = jnp.einsum('btgrd,bsgd->bgrts', qb, kwb)
        lw = lw + tbl[rel_bucket(dist_w)].transpose(2, 3, 0, 1)[None]
        pw = masked_softmax(lw, mask_w, -1)
        o_w = jnp.einsum('bgrts,bsgd->btgrd', pw, vwb)

        g = jax.nn.sigmoid(gb.astype(jnp.float32)).reshape(B, Q_BLOCK, 3, Hkv, G)[..., None]
        o = g[:, :, 0] * o_c + g[:, :, 1] * o_s + g[:, :, 2] * o_w
        return o.reshape(B, Q_BLOCK, H * dh).astype(q.dtype)

    out = lax.map(block, jnp.arange(S // Q_BLOCK))
    return out.transpose(1, 0, 2, 3).reshape(B, S, H * dh)


def sgu_mixer(uv, ln_gain, w_s, b_s):
    B, S, _ = uv.shape
    uv = jax.nn.gelu(uv)
    u, v = jnp.split(uv, 2, axis=-1)
    v = layernorm(v, ln_gain).reshape(B, S // SGU_CHUNK, SGU_CHUNK, SGU_GROUPS, MIX_W // SGU_GROUPS)
    tri = jnp.tril(jnp.ones((SGU_CHUNK, SGU_CHUNK), dtype=bool))
    w = jnp.where(tri[None], w_s, 0)
    s = jnp.einsum('gts,bcsgd->bctgd', w, v) + b_s.T[:, :, None]
    return u * s.reshape(B, S, MIX_W)


def dsa_mixer(q, k, v, q_idx, k_idx, w_idx, q_gain, k_gain, rel_tbl):
    B, S = q.shape[:2]
    H, dh = DSA_HEADS, HEAD_DIM
    q = rmsnorm(q, q_gain) * (dh ** -0.5)
    k = rmsnorm(k, k_gain)
    w_idx = w_idx * (IDX_HEADS ** -0.5)
    top_k = min(DSA_TOPK_MAX, S // 4)
    tbl = rel_tbl[:, NSA_HEADS:]
    b_idx = jnp.arange(B)[:, None, None]
    key_pos = jnp.arange(S)

    def block(i):
        q0 = i * Q_BLOCK
        t = q0 + jnp.arange(Q_BLOCK)
        qb = lax.dynamic_slice_in_dim(q, q0, Q_BLOCK, axis=1)
        qib = lax.dynamic_slice_in_dim(q_idx, q0, Q_BLOCK, axis=1)
        wib = lax.dynamic_slice_in_dim(w_idx, q0, Q_BLOCK, axis=1)
        score = jnp.einsum('bth,bths->bts', wib,
                           jax.nn.relu(jnp.einsum('bthd,bsd->bths', qib, k_idx)))
        score = jnp.where(key_pos[None, None, :] <= t[None, :, None],
                          score.astype(jnp.float32), NEG_INF)
        _, sel = lax.top_k(score, top_k)
        k_sel = k[b_idx, sel]
        v_sel = v[b_idx, sel]
        dist = t[None, :, None] - sel
        logits = jnp.einsum('bthd,btkd->bhtk', qb, k_sel)
        logits = logits + jnp.moveaxis(tbl[rel_bucket(dist)], -1, 1)
        p = masked_softmax(logits, (dist >= 0)[:, None], -1)
        o = jnp.einsum('bhtk,btkd->bthd', p, v_sel)
        return o.reshape(B, Q_BLOCK, H * dh).astype(q.dtype)

    out = lax.map(block, jnp.arange(S // Q_BLOCK))
    return out.transpose(1, 0, 2, 3).reshape(B, S, H * dh)


CONV_HALO = 8


def _conv_body(b_ref, c_ref, x_ref, w_ref, o_ref, prev_scr):
    @pl.when(pl.program_id(1) == 0)
    def _():
        prev_scr[...] = jnp.zeros(prev_scr.shape, F32)

    bx = c_ref[...] * x_ref[...]
    ts = bx.shape[0]
    row = lax.broadcasted_iota(jnp.int32, bx.shape, 0)
    prev = prev_scr[...]
    last1 = prev[CONV_HALO - 1:CONV_HALO]
    last2 = prev[CONV_HALO - 2:CONV_HALO - 1]
    back1 = jnp.where(row == 0, last1, pltpu.roll(bx, 1, 0))
    back2 = jnp.where(row == 0, last2, jnp.where(row == 1, last1, pltpu.roll(bx, 2, 0)))
    w = w_ref[...]
    o_ref[...] = b_ref[...] * (w[0:1] * back2 + w[1:2] * back1 + w[2:3] * bx)
    prev_scr[...] = bx[ts - CONV_HALO:ts]


def conv_mixer(proj, conv_w, B, S, ts=512):
    nt = S // ts
    col = lambda name: pl.BlockSpec((ts, MIX_W), lambda b, t: (b * nt + t, COL[name] // MIX_W))
    return pl.pallas_call(
        _conv_body,
        grid=(B, nt),
        in_specs=[col("a_b"), col("a_c"), col("a_x"), pl.BlockSpec((CONV_K, MIX_W), lambda b, t: (0, 0))],
        out_specs=pl.BlockSpec((ts, MIX_W), lambda b, t: (b * nt + t, 0)),
        out_shape=jax.ShapeDtypeStruct((B * S, MIX_W), F32),
        scratch_shapes=[pltpu.VMEM((CONV_HALO, MIX_W), F32)],
        compiler_params=_cparams(("parallel", "arbitrary")),
        name="conv_mixer",
    )(proj, proj, proj, conv_w)


def _sgu_body(uv_ref, g_ref, w_ref, b_ref, o_ref):
    uv = jax.nn.gelu(uv_ref[...])
    u = uv[:, :MIX_W]
    v = uv[:, MIX_W:]
    v = v - jnp.mean(v, axis=-1, keepdims=True)
    v = (v * lax.rsqrt(jnp.mean(v * v, axis=-1, keepdims=True) + EPS) * g_ref[...]).astype(BF16)
    gw = MIX_W // SGU_GROUPS
    group = lax.broadcasted_iota(jnp.int32, (SGU_CHUNK, MIX_W), 1) >> (gw.bit_length() - 1)
    for c in range(uv.shape[0] // SGU_CHUNK):
        rows = slice(c * SGU_CHUNK, (c + 1) * SGU_CHUNK)
        vc = v[rows]
        s = b_ref[...]
        for g in range(SGU_GROUPS):
            s = s + jnp.dot(w_ref[g], jnp.where(group == g, vc, jnp.zeros_like(vc)),
                            preferred_element_type=F32)
        o_ref[rows, :] = u[rows] * s


def sgu_mixer_pallas(proj, ln_gain, w_s, b_s, T, tm=512):
    tri = jnp.tril(jnp.ones((SGU_CHUNK, SGU_CHUNK), dtype=bool))
    w = jnp.where(tri[None], w_s, 0).astype(BF16)
    b = jnp.repeat(b_s.T, MIX_W // SGU_GROUPS, axis=1)
    const = lambda *shape: pl.BlockSpec(shape, lambda i: (0,) * len(shape), pipeline_mode=pl.Buffered(1))
    return pl.pallas_call(
        _sgu_body,
        grid=(T // tm,),
        in_specs=[pl.BlockSpec((tm, 2 * MIX_W), lambda i: (i, COL["c_uv"] // (2 * MIX_W))),
                  const(1, MIX_W), const(SGU_GROUPS, SGU_CHUNK, SGU_CHUNK), const(SGU_CHUNK, MIX_W)],
        out_specs=pl.BlockSpec((tm, MIX_W), lambda i: (i, 0)),
        out_shape=jax.ShapeDtypeStruct((T, MIX_W), F32),
        compiler_params=_cparams(("parallel",)),
        name="sgu_mixer",
    )(proj, ln_gain[None], w, b)


def _layer(x2, p2, B, S, rel_bias, g_mix, w_in, conv_w, nsa_cmp_pe, nsa_cmp_w1, nsa_cmp_w2,
           nsa_q_gain, nsa_k_gain, sgu_ln_gain, sgu_w, sgu_b, dsa_q_gain, dsa_k_gain,
           w_gate, w_branch, w_out, g_ffn, peer_wq, peer_subkeys, peer_u, peer_v,
           g_ple, w_ple_gate, w_ple_proj):
    T = B * S
    proj = rms_matmul(x2, g_mix[None], pack_w_in(w_in), tm=512, tn=768)

    y_a = conv_mixer(proj, conv_w, B, S)
    y_b = nsa_mixer_pallas(proj, nsa_cmp_pe, nsa_cmp_w1, nsa_cmp_w2, nsa_q_gain, nsa_k_gain,
                           rel_bias[:, :NSA_HEADS], rel_bias_tiles(rel_bias[:, :NSA_HEADS]), B, S)
    y_c = sgu_mixer_pallas(proj, sgu_ln_gain, sgu_w, sgu_b, T)
    y_d = dsa_attention(proj, COL, dsa_q_gain[None], dsa_k_gain[None],
                        rel_bias_tiles(rel_bias[:, NSA_HEADS:]), B, S)
    ys = [y_a, y_b, y_c, y_d]

    x2 = merge_mixers(x2, g_mix[None], ys, w_gate.astype(BF16),
                      w_branch.reshape(N_MIXERS, MIX_W, D_MODEL).astype(BF16),
                      w_out.astype(BF16), tm=256)

    sk = peer_subkeys.reshape(2 * PEER_HEADS, PEER_KEYS, PEER_QDIM // 2).astype(BF16)
    x2 = peer_ffn_update(x2, g_ffn[None], peer_wq.astype(BF16), sk,
                         peer_u.astype(BF16), peer_v.astype(BF16).T)

    x2 = ple_update(x2, g_ple[None], p2, w_ple_gate.astype(BF16), w_ple_proj.astype(BF16), tm=512)
    return x2


def kernel(x, p, rel_bias, g_mix, w_in, conv_w, nsa_cmp_pe, nsa_cmp_w1, nsa_cmp_w2, nsa_q_gain,
           nsa_k_gain, sgu_ln_gain, sgu_w, sgu_b, dsa_q_gain, dsa_k_gain, w_gate, w_branch, w_out,
           g_ffn, peer_wq, peer_subkeys, peer_u, peer_v, g_ple, w_ple_gate, w_ple_proj):
    B, S, D = x.shape
    depth = p.shape[0]
    x2 = x.reshape(B * S, D)
    for l in range(depth):
        x2 = _layer(x2, p[l].reshape(B * S, PLE_DIM), B, S, rel_bias, g_mix[l], w_in[l], conv_w[l],
                    nsa_cmp_pe[l], nsa_cmp_w1[l], nsa_cmp_w2[l], nsa_q_gain[l], nsa_k_gain[l],
                    sgu_ln_gain[l], sgu_w[l], sgu_b[l], dsa_q_gain[l], dsa_k_gain[l],
                    w_gate[l], w_branch[l], w_out[l], g_ffn[l], peer_wq[l], peer_subkeys[l],
                    peer_u[l], peer_v[l], g_ple[l], w_ple_gate[l], w_ple_proj[l])
    return x2.reshape(B, S, D)
```

```python
import functools
import math

import jax
import jax.numpy as jnp
import numpy as np
from jax import lax
from jax.experimental import pallas as pl
from jax.experimental.pallas import tpu as pltpu

F32 = jnp.float32
BF16 = jnp.bfloat16

D_MODEL = 1024
HEAD_DIM = 64
N_MIXERS = 4
MIX_W = D_MODEL // 2
Q_BLOCK = 128
EPS = 1e-6
NEG_INF = -1e30
CONV_K = 3
NSA_HEADS = MIX_W // HEAD_DIM
NSA_KV_HEADS = 2
CMP_BLOCK = 32
CMP_STRIDE = 16
CMP_HIDDEN = 256
SEL_BLOCK = 64
SEL_TOPN = 8
WINDOW = 512
FORCED_SCORE = 1e4
SGU_CHUNK = 128
SGU_GROUPS = 8
DSA_HEADS = MIX_W // HEAD_DIM
IDX_HEADS = 8
IDX_DIM = 64
DSA_TOPK_MAX = 256
N_BUCKETS = 32
MAX_DISTANCE = 1024
PEER_HEADS = 8
PEER_KEYS = 128
PEER_QDIM = 128
PEER_TOPK = 16
N_EXPERTS = PEER_KEYS * PEER_KEYS
PLE_DIM = 256

SPLIT_WIDTHS = (
    MIX_W, MIX_W, MIX_W,
    NSA_HEADS * HEAD_DIM,
    6 * NSA_KV_HEADS * HEAD_DIM,
    3 * NSA_HEADS,
    2 * MIX_W,
    DSA_HEADS * HEAD_DIM, HEAD_DIM, HEAD_DIM,
    IDX_HEADS * IDX_DIM, IDX_DIM, IDX_HEADS,
)
IN_WIDTH = sum(SPLIT_WIDTHS)

COL = dict(a_b=0, a_c=512, a_x=1024, n_q=1536, d_q=2048, d_qi=2560, c_uv=3072, n_kv=4096,
           n_g=4864, d_kv=4992, d_kiw=5120)
N_PACK = 5376


def pack_w_in(w):
    o = dict(zip(("a_b", "a_c", "a_x", "n_q", "n_kv", "n_g", "c_uv", "d_q", "d_k", "d_v", "d_qi", "d_ki",
                  "d_wi", "end"), [0] + [int(c) for c in np.cumsum(SPLIT_WIDTHS)]))
    z = lambda n: jnp.zeros((w.shape[0], n), w.dtype)
    cols = [w[:, o["a_b"]:o["n_kv"]], w[:, o["d_q"]:o["d_k"]], w[:, o["d_qi"]:o["d_ki"]],
            w[:, o["c_uv"]:o["d_q"]], w[:, o["n_kv"]:o["n_g"]],
            w[:, o["n_g"]:o["c_uv"]], z(128 - 3 * NSA_HEADS),
            w[:, o["d_k"]:o["d_qi"]],
            w[:, o["d_ki"]:o["end"]], z(128 - IDX_DIM - IDX_HEADS)]
    packed = jnp.concatenate(cols, axis=1)
    return jnp.pad(packed, ((0, 0), (0, N_PACK - packed.shape[1]))).astype(BF16)

VMEM_LIMIT_BYTES = 56 * 1024 * 1024


def _cparams(sem):
    return pltpu.CompilerParams(dimension_semantics=sem, vmem_limit_bytes=VMEM_LIMIT_BYTES)


def _rms(x, g):
    return x * lax.rsqrt(jnp.mean(x * x, axis=-1, keepdims=True) + EPS) * g


def _rms_matmul_body(x_ref, g_ref, w_ref, o_ref, h_ref):
    @pl.when(pl.program_id(1) == 0)
    def _():
        h_ref[...] = _rms(x_ref[...], g_ref[...]).astype(BF16)

    o_ref[...] = jnp.dot(h_ref[...], w_ref[...], preferred_element_type=F32)


def rms_matmul(x, g, w, tm, tn):
    T, D = x.shape
    N = w.shape[1]
    return pl.pallas_call(
        _rms_matmul_body,
        grid=(T // tm, N // tn),
        in_specs=[pl.BlockSpec((tm, D), lambda i, j: (i, 0)),
                  pl.BlockSpec((1, D), lambda i, j: (0, 0)),
                  pl.BlockSpec((D, tn), lambda i, j: (0, j))],
        out_specs=pl.BlockSpec((tm, tn), lambda i, j: (i, j)),
        out_shape=jax.ShapeDtypeStruct((T, N), F32),
        scratch_shapes=[pltpu.VMEM((tm, D), BF16)],
        compiler_params=_cparams(("parallel", "arbitrary")),
        name="rms_matmul",
    )(x, g, w)


def _merge_body(x_ref, g_ref, ya_ref, yb_ref, yc_ref, yd_ref, wg_ref, wb_ref, wo_ref, o_ref):
    x = x_ref[...]
    h = _rms(x, g_ref[...]).astype(BF16)
    merged = jnp.zeros(x.shape, F32)
    for m, y_ref in enumerate((ya_ref, yb_ref, yc_ref, yd_ref)):
        z = jnp.dot(y_ref[...].astype(BF16), wb_ref[m], preferred_element_type=F32)
        gate = jax.nn.sigmoid(jnp.dot(h, wg_ref[:, m * D_MODEL:(m + 1) * D_MODEL],
                                      preferred_element_type=F32))
        merged = merged + gate * z
    o_ref[...] = x + jnp.dot(merged.astype(BF16), wo_ref[...], preferred_element_type=F32)


def merge_mixers(x, g, ys, w_gate, w_branch, w_out, tm):
    T, D = x.shape
    const = lambda *shape: pl.BlockSpec(shape, lambda i: (0,) * len(shape), pipeline_mode=pl.Buffered(1))
    return pl.pallas_call(
        _merge_body,
        grid=(T // tm,),
        in_specs=[pl.BlockSpec((tm, D), lambda i: (i, 0)),
                  const(1, D)]
                 + [pl.BlockSpec((tm, MIX_W), lambda i: (i, 0)) for _ in range(N_MIXERS)]
                 + [const(D, N_MIXERS * D), const(N_MIXERS, MIX_W, D), const(D, D)],
        out_specs=pl.BlockSpec((tm, D), lambda i: (i, 0)),
        out_shape=jax.ShapeDtypeStruct((T, D), F32),
        compiler_params=_cparams(("parallel",)),
        name="merge_mixers",
    )(x, g, *ys, w_gate, w_branch, w_out)


def _ple_body(x_ref, g_ref, p_ref, wg_ref, wp_ref, o_ref):
    x = x_ref[...]
    h = _rms(x, g_ref[...]).astype(BF16)
    gate = jax.nn.sigmoid(jnp.dot(h, wg_ref[...], preferred_element_type=F32))
    proj = jnp.dot(p_ref[...].astype(BF16), wp_ref[...], preferred_element_type=F32)
    o_ref[...] = x + gate * proj


def ple_update(x, g, p, w_gate, w_proj, tm):
    T, D = x.shape
    const = lambda *shape: pl.BlockSpec(shape, lambda i: (0,) * len(shape), pipeline_mode=pl.Buffered(1))
    return pl.pallas_call(
        _ple_body,
        grid=(T // tm,),
        in_specs=[pl.BlockSpec((tm, D), lambda i: (i, 0)), const(1, D),
                  pl.BlockSpec((tm, PLE_DIM), lambda i: (i, 0)),
                  const(D, D), const(PLE_DIM, D)],
        out_specs=pl.BlockSpec((tm, D), lambda i: (i, 0)),
        out_shape=jax.ShapeDtypeStruct((T, D), F32),
        compiler_params=_cparams(("parallel",)),
        name="ple_update",
    )(x, g, p, w_gate, w_proj)


def _extract_top16(cur, tops_ref):
    m = None
    for r in range(PEER_TOPK):
        m = jnp.max(cur, axis=0, keepdims=True)
        if tops_ref is not None:
            tops_ref[r:r + 1, :] = m
        cur = jnp.where(cur == m, NEG_INF, cur)
    return m


_CAND_ROWS_J = (16, 8, 5, 4, 3, 2, 2, 2)


def _peer_topk_body(x_ref, g_ref, wq_ref, sk_ref, hn_ref, cut_ref, e1_ref, s2_ref, e2_ref,
                    t1_ref, t2_ref):
    hn = _rms(x_ref[...], g_ref[...]).astype(BF16)
    hn_ref[...] = hn
    q = jnp.dot(hn, wq_ref[...], preferred_element_type=F32).astype(BF16)
    half = PEER_QDIM // 2
    tn = x_ref.shape[0]
    j_iota = lax.broadcasted_iota(jnp.int32, (8, tn), 0)
    for h in range(PEER_HEADS):
        s = []
        for p in range(2):
            c0 = (2 * h + p) * half
            s.append(lax.dot_general(sk_ref[2 * h + p], q[:, c0:c0 + half],
                                     (((1,), (1,)), ((), ())), preferred_element_type=F32))
        th1 = _extract_top16(s[0], t1_ref)
        th2 = _extract_top16(s[1], t2_ref)
        a = t1_ref[...]
        b = t2_ref[...]
        b8 = b[0:8]
        pieces = [a[0:1] + b]
        for i in range(1, 8):
            c = a[i:i + 1] + b8
            if _CAND_ROWS_J[i] < 8:
                c = jnp.where(j_iota < _CAND_ROWS_J[i], c, NEG_INF)
            pieces.append(c)
        pieces.append(a[8:16] + b[0:1])
        cand = jnp.concatenate(pieces, axis=0)
        top = a[0:1] + b[0:1]
        theta = _extract_top16(cand, None)
        z = jnp.sum(jnp.where(cand >= theta, jnp.exp(cand - top), 0.0), axis=0, keepdims=True)
        s1 = jnp.where(s[0] >= th1, s[0], NEG_INF)
        s2 = jnp.where(s[1] >= th2, s[1], NEG_INF)
        cut = jnp.full(s1.shape, -NEG_INF, F32)
        for j in range(PEER_TOPK):
            bj = b[j:j + 1]
            cut = jnp.minimum(cut, jnp.where(s1 + bj >= theta, bj, -NEG_INF))
        cut_ref[h] = cut
        s2_ref[h] = s2
        e1_ref[h] = jnp.exp(s1 - (a[0:1] + jnp.log(z)))
        e2_ref[h] = jnp.exp(s2 - b[0:1])


def peer_topk(x, g, wq, subkeys, tn):
    T, D = x.shape
    const = lambda *shape: pl.BlockSpec(shape, lambda i: (0,) * len(shape), pipeline_mode=pl.Buffered(1))
    return pl.pallas_call(
        _peer_topk_body,
        grid=(T // tn,),
        in_specs=[pl.BlockSpec((tn, D), lambda i: (i, 0)), const(1, D), const(D, D),
                  const(2 * PEER_HEADS, PEER_KEYS, PEER_QDIM // 2)],
        out_specs=[pl.BlockSpec((tn, D), lambda i: (i, 0))]
                  + [pl.BlockSpec((PEER_HEADS, PEER_KEYS, tn), lambda i: (0, 0, i))] * 4,
        out_shape=[jax.ShapeDtypeStruct((T, D), BF16)]
                  + [jax.ShapeDtypeStruct((PEER_HEADS, PEER_KEYS, T), F32)] * 4,
        scratch_shapes=[pltpu.VMEM((PEER_TOPK, tn), F32), pltpu.VMEM((PEER_TOPK, tn), F32)],
        compiler_params=_cparams(("parallel",)),
        name="peer_topk",
    )(x, g, wq, subkeys)


LANE_CHUNK = 128


PEER_I1_STEP = 8


def _peer_main_body(hn_ref, cut_ref, e1_ref, s2_ref, e2_ref, u_ref, vt_ref, x_ref, o_ref,
                    acc_ref, p_ref):
    j = pl.program_id(1)
    tn = hn_ref.shape[0]
    slot = j % 2

    @pl.when(j == 0)
    def _():
        acc_ref[...] = jnp.zeros(acc_ref.shape, F32)
        p_ref[1] = jnp.zeros(p_ref.shape[1:], BF16)

    acc_ref[...] += jnp.dot(vt_ref[...], p_ref[1 - slot], preferred_element_type=F32)

    hn = hn_ref[...]
    for k in range(PEER_I1_STEP):
        rows = slice(k * PEER_KEYS, (k + 1) * PEER_KEYS)
        at = lax.dot_general(u_ref[rows, :], hn, (((1,), (1,)), ((), ())),
                             preferred_element_type=F32)
        for c in range(tn // LANE_CHUNK):
            ls = slice(c * LANE_CHUNK, (c + 1) * LANE_CHUNK)
            w = jnp.zeros((PEER_KEYS, LANE_CHUNK), F32)
            for h in range(PEER_HEADS):
                picked = jnp.where(s2_ref[h, :, ls] >= cut_ref[h, k:k + 1, ls], e2_ref[h, :, ls], 0.0)
                w = w + picked * e1_ref[h, k:k + 1, ls]
            p_ref[slot, rows, ls] = (w * jax.nn.gelu(at[:, ls])).astype(BF16)

    @pl.when(j == pl.num_programs(1) - 1)
    def _():
        o_ref[...] = x_ref[...] + acc_ref[...].T


def peer_main(hn, cut, e1, s2, e2, u, vt, x, tn):
    T, D = x.shape
    e_blk = PEER_I1_STEP * PEER_KEYS
    n_blk = PEER_KEYS // PEER_I1_STEP
    cur = lambda j: jnp.minimum(j, n_blk - 1)
    prev = lambda j: jnp.maximum(j - 1, 0)
    per_i1 = pl.BlockSpec((PEER_HEADS, PEER_I1_STEP, tn), lambda i, j: (0, cur(j), i))
    per_i2 = pl.BlockSpec((PEER_HEADS, PEER_KEYS, tn), lambda i, j: (0, 0, i))
    return pl.pallas_call(
        _peer_main_body,
        grid=(T // tn, n_blk + 1),
        in_specs=[pl.BlockSpec((tn, D), lambda i, j: (i, 0)),
                  per_i1, per_i1, per_i2, per_i2,
                  pl.BlockSpec((e_blk, D), lambda i, j: (cur(j), 0)),
                  pl.BlockSpec((D, e_blk), lambda i, j: (0, prev(j))),
                  pl.BlockSpec((tn, D), lambda i, j: (i, 0))],
        out_specs=pl.BlockSpec((tn, D), lambda i, j: (i, 0)),
        out_shape=jax.ShapeDtypeStruct((T, D), F32),
        scratch_shapes=[pltpu.VMEM((D, tn), F32), pltpu.VMEM((2, e_blk, tn), BF16)],
        compiler_params=_cparams(("parallel", "arbitrary")),
        name="peer_main",
    )(hn, cut, e1, s2, e2, u, vt, x)


def peer_ffn_update(x, g, wq, subkeys, u, vt):
    hn, cut, e1, s2, e2 = peer_topk(x, g, wq, subkeys, tn=256)
    return peer_main(hn, cut, e1, s2, e2, u, vt, x, tn=512)


BIAS_TILES = 9
KEY_TILES = 4
KEY_BLOCK = KEY_TILES * Q_BLOCK
WIN_TILES = WINDOW // Q_BLOCK + 1


def rel_bias_tiles(tbl):
    H = tbl.shape[1]
    d = jnp.arange(BIAS_TILES)[:, None, None]
    s = jnp.arange(Q_BLOCK)[None, :, None]
    t = jnp.arange(Q_BLOCK)[None, None, :]
    bias = tbl[rel_bucket(d * Q_BLOCK + t - s)]
    return jnp.moveaxis(bias, -1, 2).reshape(BIAS_TILES, Q_BLOCK, H * Q_BLOCK)


INT_MIN = -2 ** 31
NEG_INF_KEY = -1900671691


def _sortable_key(x):
    bits = pltpu.bitcast(x, jnp.int32)
    return jnp.where(bits < 0, bits ^ jnp.int32(0x7FFFFFFF), bits)


def _head_rms(x, g, scale):
    return x * lax.rsqrt(jnp.mean(x * x, axis=-1, keepdims=True) + EPS) * g * scale


def _dsa_body(kv_ref, kiw_ref, q_ref, qi_ref, wq_ref, qg_ref, kg_ref, bias_ref, o_ref,
              kn_scr, ki_scr, vt_scr, qall_scr, qiall_scr, sc_scr, th_scr, cnt_scr,
              m_scr, l_scr, acc_scr, *, top_k):
    i = pl.program_id(1)
    dh = HEAD_DIM
    QB = Q_BLOCK

    @pl.when(i == 0)
    def _prep():
        kv = kv_ref[...]
        kn_scr[...] = _head_rms(kv[:, :dh], kg_ref[...], 1.0).astype(BF16)
        vt_scr[...] = kv[:, dh:].T.astype(BF16)
        ki_scr[...] = kiw_ref[:, :dh].astype(BF16)

    q = q_ref[...]
    qi = qi_ref[...]
    for h in range(DSA_HEADS):
        qh = _head_rms(q[:, h * dh:(h + 1) * dh], qg_ref[...], dh ** -0.5)
        qall_scr[h * QB:(h + 1) * QB, :] = qh.astype(BF16)
        qiall_scr[h * QB:(h + 1) * QB, :] = qi[:, h * dh:(h + 1) * dh].astype(BF16)
    w_t = (wq_ref[:, dh:dh + IDX_HEADS] * (IDX_HEADS ** -0.5)).T

    s_loc = lax.broadcasted_iota(jnp.int32, (QB, QB), 0)
    t_loc = lax.broadcasted_iota(jnp.int32, (QB, QB), 1)
    n_tiles = i + 1
    n_blk = (i + KEY_TILES) // KEY_TILES
    nt_dims = (((1,), (1,)), ((), ()))

    def score_block(kb, carry):
        rows = pl.ds(pl.multiple_of(kb * KEY_BLOCK, KEY_BLOCK), KEY_BLOCK)
        kib = ki_scr[rows, :]
        sc = jnp.zeros((KEY_BLOCK, QB), F32)
        for h in range(IDX_HEADS):
            r = lax.dot_general(kib, qiall_scr[h * QB:(h + 1) * QB, :], nt_dims,
                                preferred_element_type=F32)
            sc = sc + w_t[h:h + 1, :] * jnp.maximum(r, 0.0)
        sc = jnp.where(sc == 0.0, 0.0, sc)
        s_pos = kb * KEY_BLOCK + lax.broadcasted_iota(jnp.int32, (KEY_BLOCK, QB), 0)
        t_pos = i * QB + lax.broadcasted_iota(jnp.int32, (KEY_BLOCK, QB), 1)
        sc_scr[rows, :] = _sortable_key(jnp.where(s_pos <= t_pos, sc, NEG_INF))
        return carry

    lax.fori_loop(0, n_blk, score_block, 0)

    def count_ge(cand):
        def body(kb, c):
            blk = sc_scr[pl.ds(pl.multiple_of(kb * KEY_BLOCK, KEY_BLOCK), KEY_BLOCK), :]
            return c + jnp.sum((blk >= cand).astype(jnp.int32), axis=0, keepdims=True)
        return lax.fori_loop(0, n_blk, body, jnp.zeros((1, QB), jnp.int32))

    th_scr[...] = jnp.full((1, QB), NEG_INF_KEY + 1, jnp.int32)

    @pl.when(n_tiles * QB > top_k)
    def _select():
        def bit_body(it, carry):
            ans_u, cnt_ans = carry
            cand_u = ans_u | (jnp.int32(1) << (31 - it))
            c = count_ge(cand_u ^ jnp.int32(INT_MIN))
            ok = c >= top_k
            return jnp.where(ok, cand_u, ans_u), jnp.where(ok, c, cnt_ans)

        ans_u, cnt = lax.fori_loop(
            0, 32, bit_body,
            (jnp.zeros((1, QB), jnp.int32), jnp.full((1, QB), 1, jnp.int32) * (n_blk * KEY_BLOCK)))
        th_scr[...] = ans_u ^ jnp.int32(INT_MIN)
        cnt_scr[...] = cnt

        @pl.when(jnp.max(cnt) > top_k)
        def _ties():
            theta = th_scr[...]
            need = (top_k - count_ge(theta + 1)).astype(F32)
            tri = (s_loc >= t_loc).astype(BF16)

            def body(kt, seen):
                rows = pl.ds(pl.multiple_of(kt * QB, QB), QB)
                tile = sc_scr[rows, :]
                eq = tile == theta
                rank = seen + jnp.dot(tri, eq.astype(BF16), preferred_element_type=F32)
                sc_scr[rows, :] = jnp.where(eq & (rank > need), theta - 1, tile)
                return seen + jnp.sum(eq.astype(F32), axis=0, keepdims=True)

            lax.fori_loop(0, n_tiles, body, jnp.zeros((1, QB), F32))

    theta = th_scr[...]
    m_scr[...] = jnp.full(m_scr.shape, NEG_INF, F32)
    l_scr[...] = jnp.zeros(l_scr.shape, F32)
    acc_scr[...] = jnp.zeros(acc_scr.shape, F32)

    def att_block(kb, carry):
        row0 = pl.multiple_of(kb * KEY_BLOCK, KEY_BLOCK)
        rows = pl.ds(row0, KEY_BLOCK)
        masks = [sc_scr[pl.ds(row0 + j * QB, QB), :] >= theta for j in range(KEY_TILES)]
        _attend_block(kn_scr[rows, :], vt_scr[:, rows], qall_scr, bias_ref, i - kb * KEY_TILES,
                      masks, m_scr, l_scr, acc_scr)
        return carry

    lax.fori_loop(0, n_blk, att_block, 0)

    o_t = acc_scr[...] / jnp.maximum(l_scr[...], 1e-30)
    o_ref[...] = jnp.concatenate([o_t[:, h * QB:(h + 1) * QB].T for h in range(DSA_HEADS)], axis=1)


def dsa_attention(proj, col, q_gain, k_gain, bias, B, S):
    nq = S // Q_BLOCK
    top_k = min(DSA_TOPK_MAX, S // 4)
    HQ = DSA_HEADS * Q_BLOCK
    const = lambda *shape: pl.BlockSpec(shape, lambda b, i: (0,) * len(shape), pipeline_mode=pl.Buffered(1))
    return pl.pallas_call(
        functools.partial(_dsa_body, top_k=top_k),
        grid=(B, nq),
        in_specs=[pl.BlockSpec((S, 128), lambda b, i: (b, col["d_kv"] // 128)),
                  pl.BlockSpec((S, 128), lambda b, i: (b, col["d_kiw"] // 128)),
                  pl.BlockSpec((Q_BLOCK, MIX_W), lambda b, i: (b * nq + i, col["d_q"] // MIX_W)),
                  pl.BlockSpec((Q_BLOCK, MIX_W), lambda b, i: (b * nq + i, col["d_qi"] // MIX_W)),
                  pl.BlockSpec((Q_BLOCK, 128), lambda b, i: (b * nq + i, col["d_kiw"] // 128)),
                  const(1, HEAD_DIM), const(1, HEAD_DIM), const(BIAS_TILES, Q_BLOCK, HQ)],
        out_specs=pl.BlockSpec((Q_BLOCK, MIX_W), lambda b, i: (b * nq + i, 0)),
        out_shape=jax.ShapeDtypeStruct((B * S, MIX_W), F32),
        scratch_shapes=[pltpu.VMEM((S, HEAD_DIM), BF16), pltpu.VMEM((S, HEAD_DIM), BF16),
                        pltpu.VMEM((HEAD_DIM, S), BF16),
                        pltpu.VMEM((HQ, HEAD_DIM), BF16), pltpu.VMEM((HQ, HEAD_DIM), BF16),
                        pltpu.VMEM((S, Q_BLOCK), jnp.int32),
                        pltpu.VMEM((1, Q_BLOCK), jnp.int32), pltpu.VMEM((1, Q_BLOCK), jnp.int32),
                        pltpu.VMEM((1, HQ), F32), pltpu.VMEM((1, HQ), F32), pltpu.VMEM((HEAD_DIM, HQ), F32)],
        compiler_params=_cparams(("parallel", "arbitrary")),
        name="dsa_attention",
    )(proj, proj, proj, proj, proj, q_gain, k_gain, bias)


NSA_G = NSA_HEADS // NSA_KV_HEADS
CMP_PAD = 56
CMP_NEAR = 64
SEL_TAKEN = -3e38


def _cmp_rows(S):
    return -(-(CMP_PAD + S // CMP_STRIDE) // 64) * 64


def _nsa_prep_body(ks_ref, vs_ref, kw_ref, vw_ref, kg_ref, kso_ref, vso_ref, kwo_ref, vwo_ref):
    dh = HEAD_DIM
    for g in range(NSA_KV_HEADS):
        cs = slice(g * dh, (g + 1) * dh)
        kso_ref[0, g] = _head_rms(ks_ref[:, cs], kg_ref[1:2, :], 1.0).astype(BF16)
        kwo_ref[0, g] = _head_rms(kw_ref[:, cs], kg_ref[2:3, :], 1.0).astype(BF16)
        vso_ref[0, g] = vs_ref[:, cs].T.astype(BF16)
        vwo_ref[0, g] = vw_ref[:, cs].T.astype(BF16)


def nsa_prep(proj, k_gain, B, S, ts=512):
    nt = S // ts
    c0 = COL["n_kv"] // 128
    col = lambda j: pl.BlockSpec((ts, 128), lambda b, t: (b * nt + t, c0 + j))
    k_out = pl.BlockSpec((1, NSA_KV_HEADS, ts, HEAD_DIM), lambda b, t: (b, 0, t, 0))
    v_out = pl.BlockSpec((1, NSA_KV_HEADS, HEAD_DIM, ts), lambda b, t: (b, 0, 0, t))
    k_sds = jax.ShapeDtypeStruct((B, NSA_KV_HEADS, S, HEAD_DIM), BF16)
    v_sds = jax.ShapeDtypeStruct((B, NSA_KV_HEADS, HEAD_DIM, S), BF16)
    return pl.pallas_call(
        _nsa_prep_body,
        grid=(B, nt),
        in_specs=[col(2), col(3), col(4), col(5), pl.BlockSpec((3, HEAD_DIM), lambda b, t: (0, 0))],
        out_specs=[k_out, v_out, k_out, v_out],
        out_shape=[k_sds, v_sds, k_sds, v_sds],
        compiler_params=_cparams(("parallel", "parallel")),
        name="nsa_prep",
    )(proj, proj, proj, proj, k_gain)


def _nsa_compress_body(xk_ref, xv_ref, pe_ref, w1_ref, w2_ref, kg_ref, kc_ref, vct_ref):
    R = xk_ref.shape[3]
    ncp = kc_ref.shape[2]
    half = CMP_STRIDE * HEAD_DIM
    row = lax.broadcasted_iota(jnp.int32, (R, HEAD_DIM), 0)
    for j, x_ref in enumerate((xk_ref, xv_ref)):
        x = x_ref[0, 0, 0]
        pe = pe_ref[j]
        lo = jnp.dot((x + pe[:, :half]).astype(BF16), w1_ref[j, :half, :], preferred_element_type=F32)
        hi = jnp.dot((x + pe[:, half:]).astype(BF16), w1_ref[j, half:, :], preferred_element_type=F32)
        hid = jax.nn.gelu(lo + pltpu.roll(hi, R - 1, 0))
        c = jnp.dot(hid.astype(BF16), w2_ref[j], preferred_element_type=F32)
        if j == 0:
            c = _head_rms(c, kg_ref[0:1, :], 1.0)
        c = jnp.where(row < R - 1, c, 0.0)
        c = jnp.concatenate([jnp.zeros((CMP_PAD, HEAD_DIM), F32), c,
                             jnp.zeros((ncp - CMP_PAD - R, HEAD_DIM), F32)], axis=0)
        if j == 0:
            kc_ref[0, 0] = c.astype(BF16)
        else:
            vct_ref[0, 0] = c.T.astype(BF16)


def nsa_compress(xc, pe, w1, w2, k_gain, S):
    B = xc.shape[0]
    R = S // CMP_STRIDE
    ncp = _cmp_rows(S)
    const = lambda *shape: pl.BlockSpec(shape, lambda b, g: (0,) * len(shape), pipeline_mode=pl.Buffered(1))
    return pl.pallas_call(
        _nsa_compress_body,
        grid=(B, NSA_KV_HEADS),
        in_specs=[pl.BlockSpec((1, 1, 1, R, CMP_STRIDE * HEAD_DIM), lambda b, g: (b, 0, g, 0, 0)),
                  pl.BlockSpec((1, 1, 1, R, CMP_STRIDE * HEAD_DIM), lambda b, g: (b, 1, g, 0, 0)),
                  const(2, 1, CMP_BLOCK * HEAD_DIM), const(2, CMP_BLOCK * HEAD_DIM, CMP_HIDDEN),
                  const(2, CMP_HIDDEN, HEAD_DIM), const(3, HEAD_DIM)],
        out_specs=[pl.BlockSpec((1, 1, ncp, HEAD_DIM), lambda b, g: (b, g, 0, 0)),
                   pl.BlockSpec((1, 1, HEAD_DIM, ncp), lambda b, g: (b, g, 0, 0))],
        out_shape=[jax.ShapeDtypeStruct((B, NSA_KV_HEADS, ncp, HEAD_DIM), BF16),
                   jax.ShapeDtypeStruct((B, NSA_KV_HEADS, HEAD_DIM, ncp), BF16)],
        compiler_params=_cparams(("parallel", "parallel")),
        name="nsa_compress",
    )(xc, xc, pe, w1, w2, k_gain)


def _attend_block(k_b, v_b, q_scr, bias_ref, d0, masks, m_scr, l_scr, acc_scr):
    QB = Q_BLOCK
    CH = 2 * QB
    n_sub = len(masks)
    masks2 = [jnp.concatenate([mk, mk], axis=1) for mk in masks]
    d = [jnp.clip(d0 - j, 0, BIAS_TILES - 1) for j in range(n_sub)]
    chunks = [slice(c * CH, (c + 1) * CH) for c in range(q_scr.shape[0] // CH)]
    logits = [lax.dot_general(k_b, q_scr[ls, :], (((1,), (1,)), ((), ())), preferred_element_type=F32)
              for ls in chunks]
    for ls, lg in zip(chunks, logits):
        parts = [jnp.where(masks2[j], lg[j * QB:(j + 1) * QB] + bias_ref[d[j], :, ls], NEG_INF)
                 for j in range(n_sub)]
        m_old = m_scr[:, ls]
        m_new = m_old
        for part in parts:
            m_new = jnp.maximum(m_new, jnp.max(part, axis=0, keepdims=True))
        probs = [jnp.exp(part - m_new) for part in parts]
        alpha = jnp.exp(m_old - m_new)
        l_new = alpha * l_scr[:, ls]
        for pr in probs:
            l_new = l_new + jnp.sum(pr, axis=0, keepdims=True)
        l_scr[:, ls] = l_new
        pb = jnp.concatenate([pr.astype(BF16) for pr in probs], axis=0)
        acc_scr[:, ls] = alpha * acc_scr[:, ls] + jnp.dot(v_b, pb, preferred_element_type=F32)
        m_scr[:, ls] = m_new


def _nsa_body(q_ref, gl_ref, kc_ref, vct_ref, ks_ref, vst_ref, kw_ref, vwt_ref, qg_ref, bias_ref,
              gcd_ref, c31_ref, ovl_ref, o_ref,
              qall_scr, lc_scr, sel_scr, gt_scr, ms_scr, ls_scr, as_scr, mw_scr, lw_scr, aw_scr,
              *, n_sel, top_n):
    g = pl.program_id(1)
    i = pl.program_id(2)
    dh = HEAD_DIM
    QB = Q_BLOCK
    GQ = NSA_G * QB
    ncp = kc_ref.shape[2]

    q = q_ref[...]
    for r in range(NSA_G):
        qall_scr[r * QB:(r + 1) * QB, :] = _head_rms(q[:, r * dh:(r + 1) * dh], qg_ref[...],
                                                     dh ** -0.5).astype(BF16)
    gt_scr[...] = jax.nn.sigmoid(gl_ref[...]).T

    lc_scr[...] = lax.dot_general(kc_ref[0, 0], qall_scr[...], (((1,), (1,)), ((), ())),
                                  preferred_element_type=F32) + c31_ref[0]
    near = pl.ds(pl.multiple_of(i * (QB // CMP_STRIDE), 8), CMP_NEAR)
    lc_scr[near, :] = lc_scr[near, :] + gcd_ref[0]
    n_p = lax.broadcasted_iota(jnp.int32, (ncp, GQ), 0)
    t_c = i * QB + (lax.broadcasted_iota(jnp.int32, (ncp, GQ), 1) & (QB - 1))
    cmp_end = jnp.where(n_p >= CMP_PAD, (n_p - CMP_PAD) * CMP_STRIDE + (CMP_BLOCK - 1), 1 << 30)
    mask_c = cmp_end <= t_c
    lc = jnp.where(mask_c, lc_scr[...], NEG_INF)
    e = jnp.where(mask_c, jnp.exp(lc - jnp.max(lc, axis=0, keepdims=True)), 0.0)
    pc = (e / jnp.maximum(jnp.sum(e, axis=0, keepdims=True), 1e-30)).astype(BF16)
    o_c = jnp.dot(vct_ref[0, 0], pc, preferred_element_type=F32)

    imp = jnp.zeros((n_sel, QB), F32)
    for r in range(NSA_G):
        imp = imp + jnp.dot(ovl_ref[...], pc[:, r * QB:(r + 1) * QB], preferred_element_type=F32)
    j_io = lax.broadcasted_iota(jnp.int32, (n_sel, QB), 0).astype(F32)
    cur = ((i * QB + lax.broadcasted_iota(jnp.int32, (n_sel, QB), 1)) >> 6).astype(F32)
    imp = jnp.where((j_io == 0) | (j_io == cur), FORCED_SCORE, imp)
    imp = jnp.where(j_io <= cur, imp, NEG_INF)
    sel = jnp.zeros((n_sel, QB), F32)
    for _ in range(top_n):
        mx = jnp.max(imp, axis=0, keepdims=True)
        first = jnp.min(jnp.where(imp == mx, j_io, float(n_sel)), axis=0, keepdims=True)
        hit = j_io == first
        sel = jnp.where(hit, 1.0, sel)
        imp = jnp.where(hit, SEL_TAKEN, imp)
    sel_scr[...] = sel

    for m_scr, l_scr, a_scr in ((ms_scr, ls_scr, as_scr), (mw_scr, lw_scr, aw_scr)):
        m_scr[...] = jnp.full(m_scr.shape, NEG_INF, F32)
        l_scr[...] = jnp.zeros(l_scr.shape, F32)
        a_scr[...] = jnp.zeros(a_scr.shape, F32)
    s_loc = lax.broadcasted_iota(jnp.int32, (QB, QB), 0)
    t_loc = lax.broadcasted_iota(jnp.int32, (QB, QB), 1)
    blocks_per_tile = QB // SEL_BLOCK

    def slc_block(kb, carry):
        rows = pl.ds(pl.multiple_of(kb * KEY_BLOCK, KEY_BLOCK), KEY_BLOCK)
        masks = []
        for j in range(KEY_TILES):
            kt = kb * KEY_TILES + j
            picked = jnp.where(s_loc < SEL_BLOCK, sel_scr[pl.ds(kt * blocks_per_tile, 1), :],
                               sel_scr[pl.ds(kt * blocks_per_tile + 1, 1), :])
            dist = (t_loc + i * QB) - (s_loc + kt * QB)
            masks.append(jnp.where(dist >= 0, picked, 0.0) > 0.5)
        _attend_block(ks_ref[0, 0, rows, :], vst_ref[0, 0, :, rows], qall_scr, bias_ref,
                      i - kb * KEY_TILES, masks, ms_scr, ls_scr, as_scr)
        return carry

    lax.fori_loop(0, (i + KEY_TILES) // KEY_TILES, slc_block, 0)

    kt0 = jnp.maximum(i - WINDOW // QB, 0)
    rows = pl.ds(pl.multiple_of(kt0 * QB, QB), WIN_TILES * QB)
    masks = []
    for j in range(WIN_TILES):
        dist = (t_loc + i * QB) - (s_loc + (kt0 + j) * QB)
        masks.append((dist >= 0) & (dist < WINDOW))
    _attend_block(kw_ref[0, 0, rows, :], vwt_ref[0, 0, :, rows], qall_scr, bias_ref, i - kt0, masks,
                  mw_scr, lw_scr, aw_scr)

    o_s = as_scr[...] / jnp.maximum(ls_scr[...], 1e-30)
    o_w = aw_scr[...] / jnp.maximum(lw_scr[...], 1e-30)
    outs = []
    for r in range(NSA_G):
        ls = slice(r * QB, (r + 1) * QB)
        gate = lambda j: gt_scr[pl.ds(j * NSA_HEADS + g * NSA_G + r, 1), :]
        o_r = gate(0) * o_c[:, ls] + gate(1) * o_s[:, ls] + gate(2) * o_w[:, ls]
        outs.append(o_r.T)
    o_ref[...] = jnp.concatenate(outs, axis=1)


def nsa_attention(proj, kc, vct, ks, vst, kw, vwt, q_gain, bias, gcd, c31, ovl, B, S):
    nq = S // Q_BLOCK
    GQ = NSA_G * Q_BLOCK
    GW = NSA_G * HEAD_DIM
    ncp = kc.shape[2]
    n_sel = S // SEL_BLOCK
    const = lambda *shape: pl.BlockSpec(shape, lambda b, g, i: (0,) * len(shape), pipeline_mode=pl.Buffered(1))
    per_group = lambda *shape: pl.BlockSpec((1, 1) + shape, lambda b, g, i: (b, g, 0, 0))
    return pl.pallas_call(
        functools.partial(_nsa_body, n_sel=n_sel, top_n=min(SEL_TOPN, n_sel)),
        grid=(B, NSA_KV_HEADS, nq),
        in_specs=[pl.BlockSpec((Q_BLOCK, GW), lambda b, g, i: (b * nq + i, COL["n_q"] // GW + g)),
                  pl.BlockSpec((Q_BLOCK, 128), lambda b, g, i: (b * nq + i, COL["n_g"] // 128)),
                  per_group(ncp, HEAD_DIM), per_group(HEAD_DIM, ncp),
                  per_group(S, HEAD_DIM), per_group(HEAD_DIM, S),
                  per_group(S, HEAD_DIM), per_group(HEAD_DIM, S),
                  const(1, HEAD_DIM),
                  pl.BlockSpec((BIAS_TILES, Q_BLOCK, GQ), lambda b, g, i: (0, 0, g)),
                  pl.BlockSpec((1, CMP_NEAR, GQ), lambda b, g, i: (g, 0, 0)),
                  pl.BlockSpec((1, 1, GQ), lambda b, g, i: (g, 0, 0)),
                  const(n_sel, ncp)],
        out_specs=pl.BlockSpec((Q_BLOCK, GW), lambda b, g, i: (b * nq + i, g)),
        out_shape=jax.ShapeDtypeStruct((B * S, MIX_W), F32),
        scratch_shapes=[pltpu.VMEM((GQ, HEAD_DIM), BF16), pltpu.VMEM((ncp, GQ), F32),
                        pltpu.VMEM((n_sel, Q_BLOCK), F32), pltpu.VMEM((128, Q_BLOCK), F32)]
                       + [pltpu.VMEM((1, GQ), F32), pltpu.VMEM((1, GQ), F32), pltpu.VMEM((HEAD_DIM, GQ), F32)] * 2,
        compiler_params=_cparams(("parallel", "parallel", "arbitrary")),
        name="nsa_attention",
    )(proj, proj, kc, vct, ks, vst, kw, vwt, q_gain, bias, gcd, c31, ovl)


def nsa_tables(tbl, S):
    ncp = _cmp_rows(S)
    n_rel = jnp.arange(CMP_NEAR)[:, None] - CMP_PAD
    t_loc = jnp.arange(Q_BLOCK)[None, :]
    near = tbl[rel_bucket(t_loc - CMP_STRIDE * n_rel - (CMP_BLOCK - 1))]
    far = tbl[N_BUCKETS - 1]
    gcd = jnp.moveaxis(near - far, -1, 0).reshape(NSA_KV_HEADS, NSA_G, CMP_NEAR, Q_BLOCK)
    gcd = jnp.moveaxis(gcd, 1, 2).reshape(NSA_KV_HEADS, CMP_NEAR, NSA_G * Q_BLOCK)
    c31 = jnp.repeat(far, Q_BLOCK).reshape(NSA_KV_HEADS, 1, NSA_G * Q_BLOCK)
    n = np.arange(ncp) - CMP_PAD
    sel_start = np.arange(S // SEL_BLOCK) * SEL_BLOCK
    start = n * CMP_STRIDE
    ovl = ((start[None, :] < sel_start[:, None] + SEL_BLOCK) & (start[None, :] + CMP_BLOCK > sel_start[:, None])
           & (n[None, :] >= 0) & (n[None, :] < (S - CMP_BLOCK) // CMP_STRIDE + 1))
    return gcd, c31, jnp.asarray(ovl, BF16)


def nsa_mixer_pallas(proj, cmp_pe, cmp_w1, cmp_w2, q_gain, k_gain, rel_tbl, bias, B, S):
    dh = HEAD_DIM
    c0 = COL["n_kv"]
    xc = proj[:, c0:c0 + 2 * NSA_KV_HEADS * dh].reshape(B, S // CMP_STRIDE, CMP_STRIDE, 2, NSA_KV_HEADS, dh)
    xc = xc.transpose(0, 3, 4, 1, 2, 5).reshape(B, 2, NSA_KV_HEADS, S // CMP_STRIDE, CMP_STRIDE * dh)
    kc, vct = nsa_compress(xc, cmp_pe.reshape(2, 1, CMP_BLOCK * dh), cmp_w1.astype(BF16),
                           cmp_w2.astype(BF16), k_gain, S)
    ks, vst, kw, vwt = nsa_prep(proj, k_gain, B, S)
    gcd, c31, ovl = nsa_tables(rel_tbl, S)
    return nsa_attention(proj, kc, vct, ks, vst, kw, vwt, q_gain[None], bias, gcd, c31, ovl, B, S)


def rmsnorm(x, g):
    xf = x.astype(jnp.float32)
    xf = xf * lax.rsqrt(jnp.mean(xf * xf, axis=-1, keepdims=True) + EPS)
    return (xf * g.astype(jnp.float32)).astype(x.dtype)


def layernorm(x, g):
    xf = x.astype(jnp.float32)
    xf = xf - jnp.mean(xf, axis=-1, keepdims=True)
    xf = xf * lax.rsqrt(jnp.mean(xf * xf, axis=-1, keepdims=True) + EPS)
    return (xf * g.astype(jnp.float32)).astype(x.dtype)


def rel_bucket(dist):
    n = jnp.maximum(dist, 0)
    max_exact = N_BUCKETS // 2
    nf = jnp.maximum(n, 1).astype(jnp.float32)
    large = max_exact + (jnp.log(nf / max_exact) / math.log(MAX_DISTANCE / max_exact)
                         * (N_BUCKETS - max_exact)).astype(jnp.int32)
    return jnp.where(n < max_exact, n, jnp.minimum(large, N_BUCKETS - 1))


def masked_softmax(logits, mask, axis):
    lf = jnp.where(mask, logits.astype(jnp.float32), NEG_INF)
    m = jnp.max(lf, axis=axis, keepdims=True)
    e = jnp.where(mask, jnp.exp(lf - m), 0.0)
    return e / jnp.maximum(jnp.sum(e, axis=axis, keepdims=True), 1e-30)


def short_conv_mixer(b_gate, c_gate, xv, conv_w):
    S = xv.shape[1]
    bx = c_gate * xv
    xp = jnp.pad(bx, ((0, 0), (CONV_K - 1, 0), (0, 0)))
    conv = sum(conv_w[k] * xp[:, k:k + S] for k in range(CONV_K))
    return b_gate * conv


def nsa_mixer(q, kv, gate_logits, cmp_pe, cmp_w1, cmp_w2, q_gain, k_gain, rel_tbl):
    B, S = q.shape[:2]
    H, Hkv, dh = NSA_HEADS, NSA_KV_HEADS, HEAD_DIM
    G = H // Hkv
    q = rmsnorm(q, q_gain) * (dh ** -0.5)
    k_c, v_c, k_s, v_s, k_w, v_w = (kv[:, :, i] for i in range(6))
    k_s = rmsnorm(k_s, k_gain[1])
    k_w = rmsnorm(k_w, k_gain[2])

    n_cmp = (S - CMP_BLOCK) // CMP_STRIDE + 1
    cmp_idx = np.arange(n_cmp)[:, None] * CMP_STRIDE + np.arange(CMP_BLOCK)[None, :]

    def compress(t, j):
        blk = t[:, cmp_idx] + cmp_pe[j][None, None, :, None, :]
        blk = blk.transpose(0, 1, 3, 2, 4).reshape(B, n_cmp, Hkv, CMP_BLOCK * dh)
        return jax.nn.gelu(blk @ cmp_w1[j]) @ cmp_w2[j]

    k_c = rmsnorm(compress(k_c, 0), k_gain[0])
    v_c = compress(v_c, 1)
    cmp_end = jnp.asarray(cmp_idx[:, -1], jnp.int32)

    n_sel = S // SEL_BLOCK
    sel_start = np.arange(n_sel) * SEL_BLOCK
    cmp_start = cmp_idx[:, 0]
    overlap = jnp.asarray((cmp_start[:, None] < sel_start[None, :] + SEL_BLOCK)
                          & (cmp_start[:, None] + CMP_BLOCK > sel_start[None, :]), jnp.float32)
    ks_blk = k_s.reshape(B, n_sel, SEL_BLOCK, Hkv, dh).transpose(0, 3, 1, 2, 4)
    vs_blk = v_s.reshape(B, n_sel, SEL_BLOCK, Hkv, dh).transpose(0, 3, 1, 2, 4)
    top_n = min(SEL_TOPN, n_sel)

    kw_p = jnp.pad(k_w, ((0, 0), (WINDOW, 0), (0, 0), (0, 0)))
    vw_p = jnp.pad(v_w, ((0, 0), (WINDOW, 0), (0, 0), (0, 0)))

    tbl = rel_tbl[:, :NSA_HEADS].reshape(N_BUCKETS, Hkv, G)
    b_idx = jnp.arange(B)[:, None, None]
    g_idx = jnp.arange(Hkv)[None, :, None]

    def block(i):
        q0 = i * Q_BLOCK
        t = q0 + jnp.arange(Q_BLOCK)
        qb = lax.dynamic_slice_in_dim(q, q0, Q_BLOCK, axis=1).reshape(B, Q_BLOCK, Hkv, G, dh)
        gb = lax.dynamic_slice_in_dim(gate_logits, q0, Q_BLOCK, axis=1)

        dist_c = t[:, None] - cmp_end[None, :]
        lc = jnp.einsum('btgrd,bngd->bgrtn', qb, k_c)
        lc = lc + tbl[rel_bucket(dist_c)].transpose(2, 3, 0, 1)[None]
        pc = masked_softmax(lc, dist_c >= 0, -1)
        o_c = jnp.einsum('bgrtn,bngd->btgrd', pc, v_c)

        imp = jnp.einsum('bgrtn,nj->bgtj', pc, overlap)
        cur = t // SEL_BLOCK
        j = jnp.arange(n_sel)
        forced = (j[None, :] == 0) | (j[None, :] == cur[:, None])
        imp = jnp.where(forced, FORCED_SCORE, imp)
        imp = jnp.where(j[None, :] <= cur[:, None], imp, NEG_INF)
        _, sel = lax.top_k(imp, top_n)
        sel_f = sel.reshape(B, Hkv, Q_BLOCK * top_n)
        k_sel = ks_blk[b_idx, g_idx, sel_f].reshape(B, Hkv, Q_BLOCK, top_n, SEL_BLOCK, dh)
        v_sel = vs_blk[b_idx, g_idx, sel_f].reshape(B, Hkv, Q_BLOCK, top_n, SEL_BLOCK, dh)
        pos_s = sel[..., None] * SEL_BLOCK + jnp.arange(SEL_BLOCK)
        dist_s = t[None, None, :, None, None] - pos_s
        ls = jnp.einsum('btgrd,bgtnkd->bgrtnk', qb, k_sel)
        bias_s = tbl[rel_bucket(dist_s), jnp.arange(Hkv)[None, :, None, None, None]]
        ls = ls + jnp.moveaxis(bias_s, -1, 2)
        ps = masked_softmax(ls, (dist_s >= 0)[:, :, None], (-2, -1))
        o_s = jnp.einsum('bgrtnk,bgtnkd->btgrd', ps, v_sel)

        kwb = lax.dynamic_slice_in_dim(kw_p, q0, Q_BLOCK + WINDOW, axis=1)
        vwb = lax.dynamic_slice_in_dim(vw_p, q0, Q_BLOCK + WINDOW, axis=1)
        pos_w = q0 - WINDOW + jnp.arange(Q_BLOCK + WINDOW)
        dist_w = t[:, None] - pos_w[None, :]
        mask_w = (dist_w >= 0) & (dist_w < WINDOW) & (pos_w[None, :] >= 0)
        lw = jnp.einsum('btgrd,bsgd->bgrts', qb, kwb)
        lw = lw + tbl[rel_bucket(dist_w)].transpose(2, 3, 0, 1)[None]
        pw = masked_softmax(lw, mask_w, -1)
        o_w = jnp.einsum('bgrts,bsgd->btgrd', pw, vwb)

        g = jax.nn.sigmoid(gb.astype(jnp.float32)).reshape(B, Q_BLOCK, 3, Hkv, G)[..., None]
        o = g[:, :, 0] * o_c + g[:, :, 1] * o_s + g[:, :, 2] * o_w
        return o.reshape(B, Q_BLOCK, H * dh).astype(q.dtype)

    out = lax.map(block, jnp.arange(S // Q_BLOCK))
    return out.transpose(1, 0, 2, 3).reshape(B, S, H * dh)


def sgu_mixer(uv, ln_gain, w_s, b_s):
    B, S, _ = uv.shape
    uv = jax.nn.gelu(uv)
    u, v = jnp.split(uv, 2, axis=-1)
    v = layernorm(v, ln_gain).reshape(B, S // SGU_CHUNK, SGU_CHUNK, SGU_GROUPS, MIX_W // SGU_GROUPS)
    tri = jnp.tril(jnp.ones((SGU_CHUNK, SGU_CHUNK), dtype=bool))
    w = jnp.where(tri[None], w_s, 0)
    s = jnp.einsum('gts,bcsgd->bctgd', w, v) + b_s.T[:, :, None]
    return u * s.reshape(B, S, MIX_W)


def dsa_mixer(q, k, v, q_idx, k_idx, w_idx, q_gain, k_gain, rel_tbl):
    B, S = q.shape[:2]
    H, dh = DSA_HEADS, HEAD_DIM
    q = rmsnorm(q, q_gain) * (dh ** -0.5)
    k = rmsnorm(k, k_gain)
    w_idx = w_idx * (IDX_HEADS ** -0.5)
    top_k = min(DSA_TOPK_MAX, S // 4)
    tbl = rel_tbl[:, NSA_HEADS:]
    b_idx = jnp.arange(B)[:, None, None]
    key_pos = jnp.arange(S)

    def block(i):
        q0 = i * Q_BLOCK
        t = q0 + jnp.arange(Q_BLOCK)
        qb = lax.dynamic_slice_in_dim(q, q0, Q_BLOCK, axis=1)
        qib = lax.dynamic_slice_in_dim(q_idx, q0, Q_BLOCK, axis=1)
        wib = lax.dynamic_slice_in_dim(w_idx, q0, Q_BLOCK, axis=1)
        score = jnp.einsum('bth,bths->bts', wib,
                           jax.nn.relu(jnp.einsum('bthd,bsd->bths', qib, k_idx)))
        score = jnp.where(key_pos[None, None, :] <= t[None, :, None],
                          score.astype(jnp.float32), NEG_INF)
        _, sel = lax.top_k(score, top_k)
        k_sel = k[b_idx, sel]
        v_sel = v[b_idx, sel]
        dist = t[None, :, None] - sel
        logits = jnp.einsum('bthd,btkd->bhtk', qb, k_sel)
        logits = logits + jnp.moveaxis(tbl[rel_bucket(dist)], -1, 1)
        p = masked_softmax(logits, (dist >= 0)[:, None], -1)
        o = jnp.einsum('bhtk,btkd->bthd', p, v_sel)
        return o.reshape(B, Q_BLOCK, H * dh).astype(q.dtype)

    out = lax.map(block, jnp.arange(S // Q_BLOCK))
    return out.transpose(1, 0, 2, 3).reshape(B, S, H * dh)


CONV_HALO = 8


def _conv_body(b_ref, c_ref, x_ref, w_ref, o_ref, prev_scr):
    @pl.when(pl.program_id(1) == 0)
    def _():
        prev_scr[...] = jnp.zeros(prev_scr.shape, F32)

    bx = c_ref[...] * x_ref[...]
    ts = bx.shape[0]
    row = lax.broadcasted_iota(jnp.int32, bx.shape, 0)
    prev = prev_scr[...]
    last1 = prev[CONV_HALO - 1:CONV_HALO]
    last2 = prev[CONV_HALO - 2:CONV_HALO - 1]
    back1 = jnp.where(row == 0, last1, pltpu.roll(bx, 1, 0))
    back2 = jnp.where(row == 0, last2, jnp.where(row == 1, last1, pltpu.roll(bx, 2, 0)))
    w = w_ref[...]
    o_ref[...] = b_ref[...] * (w[0:1] * back2 + w[1:2] * back1 + w[2:3] * bx)
    prev_scr[...] = bx[ts - CONV_HALO:ts]


def conv_mixer(proj, conv_w, B, S, ts=512):
    nt = S // ts
    col = lambda name: pl.BlockSpec((ts, MIX_W), lambda b, t: (b * nt + t, COL[name] // MIX_W))
    return pl.pallas_call(
        _conv_body,
        grid=(B, nt),
        in_specs=[col("a_b"), col("a_c"), col("a_x"), pl.BlockSpec((CONV_K, MIX_W), lambda b, t: (0, 0))],
        out_specs=pl.BlockSpec((ts, MIX_W), lambda b, t: (b * nt + t, 0)),
        out_shape=jax.ShapeDtypeStruct((B * S, MIX_W), F32),
        scratch_shapes=[pltpu.VMEM((CONV_HALO, MIX_W), F32)],
        compiler_params=_cparams(("parallel", "arbitrary")),
        name="conv_mixer",
    )(proj, proj, proj, conv_w)


def _sgu_body(uv_ref, g_ref, w_ref, b_ref, o_ref):
    uv = jax.nn.gelu(uv_ref[...])
    u = uv[:, :MIX_W]
    v = uv[:, MIX_W:]
    v = v - jnp.mean(v, axis=-1, keepdims=True)
    v = (v * lax.rsqrt(jnp.mean(v * v, axis=-1, keepdims=True) + EPS) * g_ref[...]).astype(BF16)
    gw = MIX_W // SGU_GROUPS
    group = lax.broadcasted_iota(jnp.int32, (SGU_CHUNK, MIX_W), 1) >> (gw.bit_length() - 1)
    for c in range(uv.shape[0] // SGU_CHUNK):
        rows = slice(c * SGU_CHUNK, (c + 1) * SGU_CHUNK)
        vc = v[rows]
        s = b_ref[...]
        for g in range(SGU_GROUPS):
            s = s + jnp.dot(w_ref[g], jnp.where(group == g, vc, jnp.zeros_like(vc)),
                            preferred_element_type=F32)
        o_ref[rows, :] = u[rows] * s


def sgu_mixer_pallas(proj, ln_gain, w_s, b_s, T, tm=512):
    tri = jnp.tril(jnp.ones((SGU_CHUNK, SGU_CHUNK), dtype=bool))
    w = jnp.where(tri[None], w_s, 0).astype(BF16)
    b = jnp.repeat(b_s.T, MIX_W // SGU_GROUPS, axis=1)
    const = lambda *shape: pl.BlockSpec(shape, lambda i: (0,) * len(shape), pipeline_mode=pl.Buffered(1))
    return pl.pallas_call(
        _sgu_body,
        grid=(T // tm,),
        in_specs=[pl.BlockSpec((tm, 2 * MIX_W), lambda i: (i, COL["c_uv"] // (2 * MIX_W))),
                  const(1, MIX_W), const(SGU_GROUPS, SGU_CHUNK, SGU_CHUNK), const(SGU_CHUNK, MIX_W)],
        out_specs=pl.BlockSpec((tm, MIX_W), lambda i: (i, 0)),
        out_shape=jax.ShapeDtypeStruct((T, MIX_W), F32),
        compiler_params=_cparams(("parallel",)),
        name="sgu_mixer",
    )(proj, ln_gain[None], w, b)


def _layer(x2, p2, B, S, rel_bias, g_mix, w_in, conv_w, nsa_cmp_pe, nsa_cmp_w1, nsa_cmp_w2,
           nsa_q_gain, nsa_k_gain, sgu_ln_gain, sgu_w, sgu_b, dsa_q_gain, dsa_k_gain,
           w_gate, w_branch, w_out, g_ffn, peer_wq, peer_subkeys, peer_u, peer_v,
           g_ple, w_ple_gate, w_ple_proj):
    T = B * S
    proj = rms_matmul(x2, g_mix[None], pack_w_in(w_in), tm=512, tn=768)

    y_a = conv_mixer(proj, conv_w, B, S)
    y_b = nsa_mixer_pallas(proj, nsa_cmp_pe, nsa_cmp_w1, nsa_cmp_w2, nsa_q_gain, nsa_k_gain,
                           rel_bias[:, :NSA_HEADS], rel_bias_tiles(rel_bias[:, :NSA_HEADS]), B, S)
    y_c = sgu_mixer_pallas(proj, sgu_ln_gain, sgu_w, sgu_b, T)
    y_d = dsa_attention(proj, COL, dsa_q_gain[None], dsa_k_gain[None],
                        rel_bias_tiles(rel_bias[:, NSA_HEADS:]), B, S)
    ys = [y_a, y_b, y_c, y_d]

    x2 = merge_mixers(x2, g_mix[None], ys, w_gate.astype(BF16),
                      w_branch.reshape(N_MIXERS, MIX_W, D_MODEL).astype(BF16),
                      w_out.astype(BF16), tm=256)

    sk = peer_subkeys.reshape(2 * PEER_HEADS, PEER_KEYS, PEER_QDIM // 2).astype(BF16)
    x2 = peer_ffn_update(x2, g_ffn[None], peer_wq.astype(BF16), sk,
                         peer_u.astype(BF16), peer_v.astype(BF16).T)

    x2 = ple_update(x2, g_ple[None], p2, w_ple_gate.astype(BF16), w_ple_proj.astype(BF16), tm=512)
    return x2


def kernel(x, p, rel_bias, g_mix, w_in, conv_w, nsa_cmp_pe, nsa_cmp_w1, nsa_cmp_w2, nsa_q_gain,
           nsa_k_gain, sgu_ln_gain, sgu_w, sgu_b, dsa_q_gain, dsa_k_gain, w_gate, w_branch, w_out,
           g_ffn, peer_wq, peer_subkeys, peer_u, peer_v, g_ple, w_ple_gate, w_ple_proj):
    B, S, D = x.shape
    depth = p.shape[0]
    x2 = x.reshape(B * S, D)
    for l in range(depth):
        x2 = _layer(x2, p[l].reshape(B * S, PLE_DIM), B, S, rel_bias, g_mix[l], w_in[l], conv_w[l],
                    nsa_cmp_pe[l], nsa_cmp_w1[l], nsa_cmp_w2[l], nsa_q_gain[l], nsa_k_gain[l],
                    sgu_ln_gain[l], sgu_w[l], sgu_b[l], dsa_q_gain[l], dsa_k_gain[l],
                    w_gate[l], w_branch[l], w_out[l], g_ffn[l], peer_wq[l], peer_subkeys[l],
                    peer_u[l], peer_v[l], g_ple[l], w_ple_gate[l], w_ple_proj[l])
    return x2.reshape(B, S, D)
```

```python
import functools
import math

import jax
import jax.numpy as jnp
import numpy as np
from jax import lax
from jax.experimental import pallas as pl
from jax.experimental.pallas import tpu as pltpu

F32 = jnp.float32
BF16 = jnp.bfloat16

D_MODEL = 1024
HEAD_DIM = 64
N_MIXERS = 4
MIX_W = D_MODEL // 2
Q_BLOCK = 128
EPS = 1e-6
NEG_INF = -1e30
CONV_K = 3
NSA_HEADS = MIX_W // HEAD_DIM
NSA_KV_HEADS = 2
CMP_BLOCK = 32
CMP_STRIDE = 16
CMP_HIDDEN = 256
SEL_BLOCK = 64
SEL_TOPN = 8
WINDOW = 512
FORCED_SCORE = 1e4
SGU_CHUNK = 128
SGU_GROUPS = 8
DSA_HEADS = MIX_W // HEAD_DIM
IDX_HEADS = 8
IDX_DIM = 64
DSA_TOPK_MAX = 256
N_BUCKETS = 32
MAX_DISTANCE = 1024
PEER_HEADS = 8
PEER_KEYS = 128
PEER_QDIM = 128
PEER_TOPK = 16
N_EXPERTS = PEER_KEYS * PEER_KEYS
PLE_DIM = 256

SPLIT_WIDTHS = (
    MIX_W, MIX_W, MIX_W,
    NSA_HEADS * HEAD_DIM,
    6 * NSA_KV_HEADS * HEAD_DIM,
    3 * NSA_HEADS,
    2 * MIX_W,
    DSA_HEADS * HEAD_DIM, HEAD_DIM, HEAD_DIM,
    IDX_HEADS * IDX_DIM, IDX_DIM, IDX_HEADS,
)
IN_WIDTH = sum(SPLIT_WIDTHS)

COL = dict(a_b=0, a_c=512, a_x=1024, n_q=1536, d_q=2048, d_qi=2560, c_uv=3072, n_kv=4096,
           n_g=4864, d_kv=4992, d_kiw=5120)
N_PACK = 5376


def pack_w_in(w):
    o = dict(zip(("a_b", "a_c", "a_x", "n_q", "n_kv", "n_g", "c_uv", "d_q", "d_k", "d_v", "d_qi", "d_ki",
                  "d_wi", "end"), [0] + [int(c) for c in np.cumsum(SPLIT_WIDTHS)]))
    z = lambda n: jnp.zeros((w.shape[0], n), w.dtype)
    cols = [w[:, o["a_b"]:o["n_kv"]], w[:, o["d_q"]:o["d_k"]], w[:, o["d_qi"]:o["d_ki"]],
            w[:, o["c_uv"]:o["d_q"]], w[:, o["n_kv"]:o["n_g"]],
            w[:, o["n_g"]:o["c_uv"]], z(128 - 3 * NSA_HEADS),
            w[:, o["d_k"]:o["d_qi"]],
            w[:, o["d_ki"]:o["end"]], z(128 - IDX_DIM - IDX_HEADS)]
    packed = jnp.concatenate(cols, axis=1)
    return jnp.pad(packed, ((0, 0), (0, N_PACK - packed.shape[1]))).astype(BF16)

VMEM_LIMIT_BYTES = 56 * 1024 * 1024


def _cparams(sem):
    return pltpu.CompilerParams(dimension_semantics=sem, vmem_limit_bytes=VMEM_LIMIT_BYTES)


def _rms(x, g):
    return x * lax.rsqrt(jnp.mean(x * x, axis=-1, keepdims=True) + EPS) * g


def _rms_matmul_body(x_ref, g_ref, w_ref, o_ref, h_ref):
    @pl.when(pl.program_id(1) == 0)
    def _():
        h_ref[...] = _rms(x_ref[...], g_ref[...]).astype(BF16)

    o_ref[...] = jnp.dot(h_ref[...], w_ref[...], preferred_element_type=F32)


def rms_matmul(x, g, w, tm, tn):
    T, D = x.shape
    N = w.shape[1]
    return pl.pallas_call(
        _rms_matmul_body,
        grid=(T // tm, N // tn),
        in_specs=[pl.BlockSpec((tm, D), lambda i, j: (i, 0)),
                  pl.BlockSpec((1, D), lambda i, j: (0, 0)),
                  pl.BlockSpec((D, tn), lambda i, j: (0, j))],
        out_specs=pl.BlockSpec((tm, tn), lambda i, j: (i, j)),
        out_shape=jax.ShapeDtypeStruct((T, N), F32),
        scratch_shapes=[pltpu.VMEM((tm, D), BF16)],
        compiler_params=_cparams(("parallel", "arbitrary")),
        name="rms_matmul",
    )(x, g, w)


def _merge_body(x_ref, g_ref, ya_ref, yb_ref, yc_ref, yd_ref, wg_ref, wb_ref, wo_ref, o_ref):
    x = x_ref[...]
    h = _rms(x, g_ref[...]).astype(BF16)
    merged = jnp.zeros(x.shape, F32)
    for m, y_ref in enumerate((ya_ref, yb_ref, yc_ref, yd_ref)):
        z = jnp.dot(y_ref[...].astype(BF16), wb_ref[m], preferred_element_type=F32)
        gate = jax.nn.sigmoid(jnp.dot(h, wg_ref[:, m * D_MODEL:(m + 1) * D_MODEL],
                                      preferred_element_type=F32))
        merged = merged + gate * z
    o_ref[...] = x + jnp.dot(merged.astype(BF16), wo_ref[...], preferred_element_type=F32)


def merge_mixers(x, g, ys, w_gate, w_branch, w_out, tm):
    T, D = x.shape
    const = lambda *shape: pl.BlockSpec(shape, lambda i: (0,) * len(shape), pipeline_mode=pl.Buffered(1))
    return pl.pallas_call(
        _merge_body,
        grid=(T // tm,),
        in_specs=[pl.BlockSpec((tm, D), lambda i: (i, 0)),
                  const(1, D)]
                 + [pl.BlockSpec((tm, MIX_W), lambda i: (i, 0)) for _ in range(N_MIXERS)]
                 + [const(D, N_MIXERS * D), const(N_MIXERS, MIX_W, D), const(D, D)],
        out_specs=pl.BlockSpec((tm, D), lambda i: (i, 0)),
        out_shape=jax.ShapeDtypeStruct((T, D), F32),
        compiler_params=_cparams(("parallel",)),
        name="merge_mixers",
    )(x, g, *ys, w_gate, w_branch, w_out)


def _ple_body(x_ref, g_ref, p_ref, wg_ref, wp_ref, o_ref):
    x = x_ref[...]
    h = _rms(x, g_ref[...]).astype(BF16)
    gate = jax.nn.sigmoid(jnp.dot(h, wg_ref[...], preferred_element_type=F32))
    proj = jnp.dot(p_ref[...].astype(BF16), wp_ref[...], preferred_element_type=F32)
    o_ref[...] = x + gate * proj


def ple_update(x, g, p, w_gate, w_proj, tm):
    T, D = x.shape
    const = lambda *shape: pl.BlockSpec(shape, lambda i: (0,) * len(shape), pipeline_mode=pl.Buffered(1))
    return pl.pallas_call(
        _ple_body,
        grid=(T // tm,),
        in_specs=[pl.BlockSpec((tm, D), lambda i: (i, 0)), const(1, D),
                  pl.BlockSpec((tm, PLE_DIM), lambda i: (i, 0)),
                  const(D, D), const(PLE_DIM, D)],
        out_specs=pl.BlockSpec((tm, D), lambda i: (i, 0)),
        out_shape=jax.ShapeDtypeStruct((T, D), F32),
        compiler_params=_cparams(("parallel",)),
        name="ple_update",
    )(x, g, p, w_gate, w_proj)


def _extract_top16(cur, tops_ref):
    m = None
    for r in range(PEER_TOPK):
        m = jnp.max(cur, axis=0, keepdims=True)
        if tops_ref is not None:
            tops_ref[r:r + 1, :] = m
        cur = jnp.where(cur == m, NEG_INF, cur)
    return m


_CAND_ROWS_J = (16, 8, 5, 4, 3, 2, 2, 2)


def _peer_topk_body(x_ref, g_ref, wq_ref, sk_ref, hn_ref, cut_ref, e1_ref, s2_ref, e2_ref,
                    t1_ref, t2_ref):
    hn = _rms(x_ref[...], g_ref[...]).astype(BF16)
    hn_ref[...] = hn
    q = jnp.dot(hn, wq_ref[...], preferred_element_type=F32).astype(BF16)
    half = PEER_QDIM // 2
    tn = x_ref.shape[0]
    j_iota = lax.broadcasted_iota(jnp.int32, (8, LANE_CHUNK), 0)
    for h in range(PEER_HEADS):
        scores = []
        for p in range(2):
            c0 = (2 * h + p) * half
            scores.append(lax.dot_general(sk_ref[2 * h + p], q[:, c0:c0 + half],
                                          (((1,), (1,)), ((), ())), preferred_element_type=F32))
        for c in range(tn // LANE_CHUNK):
            ls = slice(c * LANE_CHUNK, (c + 1) * LANE_CHUNK)
            s = [sc[:, ls] for sc in scores]
            th1 = _extract_top16(s[0], t1_ref)
            th2 = _extract_top16(s[1], t2_ref)
            a = t1_ref[...]
            b = t2_ref[...]
            b8 = b[0:8]
            pieces = [a[0:1] + b]
            for i in range(1, 8):
                cand_i = a[i:i + 1] + b8
                if _CAND_ROWS_J[i] < 8:
                    cand_i = jnp.where(j_iota < _CAND_ROWS_J[i], cand_i, NEG_INF)
                pieces.append(cand_i)
            pieces.append(a[8:16] + b[0:1])
            cand = jnp.concatenate(pieces, axis=0)
            top = a[0:1] + b[0:1]
            theta = _extract_top16(cand, None)
            z = jnp.sum(jnp.where(cand >= theta, jnp.exp(cand - top), 0.0), axis=0, keepdims=True)
            s1 = jnp.where(s[0] >= th1, s[0], NEG_INF)
            s2 = jnp.where(s[1] >= th2, s[1], NEG_INF)
            cut = jnp.full(s1.shape, -NEG_INF, F32)
            for j in range(PEER_TOPK):
                bj = b[j:j + 1]
                cut = jnp.minimum(cut, jnp.where(s1 + bj >= theta, bj, -NEG_INF))
            cut_ref[h, :, ls] = cut
            s2_ref[h, :, ls] = s2
            e1_ref[h, :, ls] = jnp.exp(s1 - (a[0:1] + jnp.log(z)))
            e2_ref[h, :, ls] = jnp.exp(s2 - b[0:1])


def peer_topk(x, g, wq, subkeys, tn):
    T, D = x.shape
    const = lambda *shape: pl.BlockSpec(shape, lambda i: (0,) * len(shape), pipeline_mode=pl.Buffered(1))
    return pl.pallas_call(
        _peer_topk_body,
        grid=(T // tn,),
        in_specs=[pl.BlockSpec((tn, D), lambda i: (i, 0)), const(1, D), const(D, D),
                  const(2 * PEER_HEADS, PEER_KEYS, PEER_QDIM // 2)],
        out_specs=[pl.BlockSpec((tn, D), lambda i: (i, 0))]
                  + [pl.BlockSpec((PEER_HEADS, PEER_KEYS, tn), lambda i: (0, 0, i))] * 4,
        out_shape=[jax.ShapeDtypeStruct((T, D), BF16)]
                  + [jax.ShapeDtypeStruct((PEER_HEADS, PEER_KEYS, T), F32)] * 4,
        scratch_shapes=[pltpu.VMEM((PEER_TOPK, LANE_CHUNK), F32), pltpu.VMEM((PEER_TOPK, LANE_CHUNK), F32)],
        compiler_params=_cparams(("parallel",)),
        name="peer_topk",
    )(x, g, wq, subkeys)


LANE_CHUNK = 128


PEER_I1_STEP = 8


def _peer_main_body(hn_ref, cut_ref, e1_ref, s2_ref, e2_ref, u_ref, vt_prev_ref, vt_ref, x_ref, o_ref,
                    acc_ref, p_ref):
    j = pl.program_id(1)
    tn = hn_ref.shape[0]
    slot = j % 2

    @pl.when(j == 0)
    def _():
        acc_ref[...] = jnp.zeros(acc_ref.shape, F32)
        p_ref[1] = jnp.zeros(p_ref.shape[1:], BF16)

    hn = hn_ref[...]
    for k in range(PEER_I1_STEP):
        rows = slice(k * PEER_KEYS, (k + 1) * PEER_KEYS)
        at = lax.dot_general(u_ref[rows, :], hn, (((1,), (1,)), ((), ())),
                             preferred_element_type=F32)
        for c in range(tn // LANE_CHUNK):
            ls = slice(c * LANE_CHUNK, (c + 1) * LANE_CHUNK)
            w = jnp.zeros((PEER_KEYS, LANE_CHUNK), F32)
            for h in range(PEER_HEADS):
                picked = jnp.where(s2_ref[h, :, ls] >= cut_ref[h, k:k + 1, ls], e2_ref[h, :, ls], 0.0)
                w = w + picked * e1_ref[h, k:k + 1, ls]
            p_ref[slot, rows, ls] = (w * jax.nn.gelu(at[:, ls])).astype(BF16)

    acc_ref[...] += jnp.dot(vt_prev_ref[...], p_ref[1 - slot], preferred_element_type=F32)

    @pl.when(j == pl.num_programs(1) - 1)
    def _():
        acc = acc_ref[...] + jnp.dot(vt_ref[...], p_ref[slot], preferred_element_type=F32)
        o_ref[...] = x_ref[...] + acc.T


def peer_main(hn, cut, e1, s2, e2, u, vt, x, tn):
    T, D = x.shape
    e_blk = PEER_I1_STEP * PEER_KEYS
    prev = lambda j: jnp.maximum(j - 1, 0)
    per_i1 = pl.BlockSpec((PEER_HEADS, PEER_I1_STEP, tn), lambda i, j: (0, j, i))
    per_i2 = pl.BlockSpec((PEER_HEADS, PEER_KEYS, tn), lambda i, j: (0, 0, i))
    return pl.pallas_call(
        _peer_main_body,
        grid=(T // tn, PEER_KEYS // PEER_I1_STEP),
        in_specs=[pl.BlockSpec((tn, D), lambda i, j: (i, 0)),
                  per_i1, per_i1, per_i2, per_i2,
                  pl.BlockSpec((e_blk, D), lambda i, j: (j, 0)),
                  pl.BlockSpec((D, e_blk), lambda i, j: (0, prev(j))),
                  pl.BlockSpec((D, e_blk), lambda i, j: (0, j)),
                  pl.BlockSpec((tn, D), lambda i, j: (i, 0))],
        out_specs=pl.BlockSpec((tn, D), lambda i, j: (i, 0)),
        out_shape=jax.ShapeDtypeStruct((T, D), F32),
        scratch_shapes=[pltpu.VMEM((D, tn), F32), pltpu.VMEM((2, e_blk, tn), BF16)],
        compiler_params=_cparams(("parallel", "arbitrary")),
        name="peer_main",
    )(hn, cut, e1, s2, e2, u, vt, vt, x)


def peer_ffn_update(x, g, wq, subkeys, u, vt):
    hn, cut, e1, s2, e2 = peer_topk(x, g, wq, subkeys, tn=256)
    return peer_main(hn, cut, e1, s2, e2, u, vt, x, tn=512)


BIAS_TILES = 9
LOG2E = math.log2(math.e)
KEY_TILES = 4
KEY_BLOCK = KEY_TILES * Q_BLOCK
WIN_TILES = WINDOW // Q_BLOCK + 1


def rel_bias_tiles(tbl):
    H = tbl.shape[1]
    d = jnp.arange(BIAS_TILES)[:, None, None]
    s = jnp.arange(Q_BLOCK)[None, :, None]
    t = jnp.arange(Q_BLOCK)[None, None, :]
    bias = tbl[rel_bucket(d * Q_BLOCK + t - s)] * LOG2E
    return jnp.moveaxis(bias, -1, 2).reshape(BIAS_TILES, Q_BLOCK, H * Q_BLOCK)


INT_MIN = -2 ** 31
NEG_INF_KEY = -1900671691


def _sortable_key(x):
    bits = pltpu.bitcast(x, jnp.int32)
    return jnp.where(bits < 0, bits ^ jnp.int32(0x7FFFFFFF), bits)


def _head_rms(x, g, scale):
    return x * lax.rsqrt(jnp.mean(x * x, axis=-1, keepdims=True) + EPS) * g * scale


def _dsa_body(kv_ref, kiw_ref, q_ref, qi_ref, wq_ref, qg_ref, kg_ref, bias_ref, o_ref,
              kn_scr, ki_scr, vt_scr, qall_scr, qiall_scr, sc_scr, th_scr, cnt_scr,
              m_scr, l_scr, acc_scr, *, top_k):
    i = pl.program_id(1)
    dh = HEAD_DIM
    QB = Q_BLOCK

    @pl.when(i == 0)
    def _prep():
        kv = kv_ref[...]
        kn_scr[...] = _head_rms(kv[:, :dh], kg_ref[...], 1.0).astype(BF16)
        vt_scr[...] = kv[:, dh:].T.astype(BF16)
        ki_scr[...] = kiw_ref[:, :dh].astype(BF16)

    q = q_ref[...]
    qi = qi_ref[...]
    for h in range(DSA_HEADS):
        qh = _head_rms(q[:, h * dh:(h + 1) * dh], qg_ref[...], dh ** -0.5 * LOG2E)
        qall_scr[h * QB:(h + 1) * QB, :] = qh.astype(BF16)
        qiall_scr[h * QB:(h + 1) * QB, :] = qi[:, h * dh:(h + 1) * dh].astype(BF16)
    w_t = (wq_ref[:, dh:dh + IDX_HEADS] * (IDX_HEADS ** -0.5)).T

    s_loc = lax.broadcasted_iota(jnp.int32, (QB, QB), 0)
    t_loc = lax.broadcasted_iota(jnp.int32, (QB, QB), 1)
    n_tiles = i + 1
    n_blk = (i + KEY_TILES) // KEY_TILES
    nt_dims = (((1,), (1,)), ((), ()))

    def score_block(kb, carry):
        rows = pl.ds(pl.multiple_of(kb * KEY_BLOCK, KEY_BLOCK), KEY_BLOCK)
        kib = ki_scr[rows, :]
        sc = jnp.zeros((KEY_BLOCK, QB), F32)
        for h in range(IDX_HEADS):
            r = lax.dot_general(kib, qiall_scr[h * QB:(h + 1) * QB, :], nt_dims,
                                preferred_element_type=F32)
            sc = sc + w_t[h:h + 1, :] * jnp.maximum(r, 0.0)
        sc = jnp.where(sc == 0.0, 0.0, sc)
        s_pos = kb * KEY_BLOCK + lax.broadcasted_iota(jnp.int32, (KEY_BLOCK, QB), 0)
        t_pos = i * QB + lax.broadcasted_iota(jnp.int32, (KEY_BLOCK, QB), 1)
        sc_scr[rows, :] = _sortable_key(jnp.where(s_pos <= t_pos, sc, NEG_INF))
        return carry

    lax.fori_loop(0, n_blk, score_block, 0)

    def count_ge(cand):
        def body(kb, c):
            blk = sc_scr[pl.ds(pl.multiple_of(kb * KEY_BLOCK, KEY_BLOCK), KEY_BLOCK), :]
            return c + jnp.sum((blk >= cand).astype(jnp.int32), axis=0, keepdims=True)
        return lax.fori_loop(0, n_blk, body, jnp.zeros((1, QB), jnp.int32))

    th_scr[...] = jnp.full((1, QB), NEG_INF_KEY + 1, jnp.int32)

    @pl.when(n_tiles * QB > top_k)
    def _select():
        def bit_body(it, carry):
            ans_u, cnt_ans = carry
            cand_u = ans_u | (jnp.int32(1) << (31 - it))
            c = count_ge(cand_u ^ jnp.int32(INT_MIN))
            ok = c >= top_k
            return jnp.where(ok, cand_u, ans_u), jnp.where(ok, c, cnt_ans)

        ans_u, cnt = lax.fori_loop(
            0, 32, bit_body,
            (jnp.zeros((1, QB), jnp.int32), jnp.full((1, QB), 1, jnp.int32) * (n_blk * KEY_BLOCK)))
        th_scr[...] = ans_u ^ jnp.int32(INT_MIN)
        cnt_scr[...] = cnt

        @pl.when(jnp.max(cnt) > top_k)
        def _ties():
            theta = th_scr[...]
            need = (top_k - count_ge(theta + 1)).astype(F32)
            tri = (s_loc >= t_loc).astype(BF16)

            def body(kt, seen):
                rows = pl.ds(pl.multiple_of(kt * QB, QB), QB)
                tile = sc_scr[rows, :]
                eq = tile == theta
                rank = seen + jnp.dot(tri, eq.astype(BF16), preferred_element_type=F32)
                sc_scr[rows, :] = jnp.where(eq & (rank > need), theta - 1, tile)
                return seen + jnp.sum(eq.astype(F32), axis=0, keepdims=True)

            lax.fori_loop(0, n_tiles, body, jnp.zeros((1, QB), F32))

    theta = th_scr[...]
    m_scr[...] = jnp.full(m_scr.shape, NEG_INF, F32)
    l_scr[...] = jnp.zeros(l_scr.shape, F32)
    acc_scr[...] = jnp.zeros(acc_scr.shape, F32)

    def att_block(kb, carry):
        row0 = pl.multiple_of(kb * KEY_BLOCK, KEY_BLOCK)
        rows = pl.ds(row0, KEY_BLOCK)
        masks = [sc_scr[pl.ds(row0 + j * QB, QB), :] >= theta for j in range(KEY_TILES)]
        _attend_block(kn_scr[rows, :], vt_scr[:, rows], qall_scr, bias_ref, i - kb * KEY_TILES,
                      masks, m_scr, l_scr, acc_scr)
        return carry

    lax.fori_loop(0, n_blk, att_block, 0)

    o_t = acc_scr[...] / jnp.maximum(l_scr[...], 1e-30)
    o_ref[...] = jnp.concatenate([o_t[:, h * QB:(h + 1) * QB].T for h in range(DSA_HEADS)], axis=1)


def dsa_attention(proj, col, q_gain, k_gain, bias, B, S):
    nq = S // Q_BLOCK
    top_k = min(DSA_TOPK_MAX, S // 4)
    HQ = DSA_HEADS * Q_BLOCK
    const = lambda *shape: pl.BlockSpec(shape, lambda b, i: (0,) * len(shape), pipeline_mode=pl.Buffered(1))
    return pl.pallas_call(
        functools.partial(_dsa_body, top_k=top_k),
        grid=(B, nq),
        in_specs=[pl.BlockSpec((S, 128), lambda b, i: (b, col["d_kv"] // 128)),
                  pl.BlockSpec((S, 128), lambda b, i: (b, col["d_kiw"] // 128)),
                  pl.BlockSpec((Q_BLOCK, MIX_W), lambda b, i: (b * nq + i, col["d_q"] // MIX_W)),
                  pl.BlockSpec((Q_BLOCK, MIX_W), lambda b, i: (b * nq + i, col["d_qi"] // MIX_W)),
                  pl.BlockSpec((Q_BLOCK, 128), lambda b, i: (b * nq + i, col["d_kiw"] // 128)),
                  const(1, HEAD_DIM), const(1, HEAD_DIM), const(BIAS_TILES, Q_BLOCK, HQ)],
        out_specs=pl.BlockSpec((Q_BLOCK, MIX_W), lambda b, i: (b * nq + i, 0)),
        out_shape=jax.ShapeDtypeStruct((B * S, MIX_W), F32),
        scratch_shapes=[pltpu.VMEM((S, HEAD_DIM), BF16), pltpu.VMEM((S, HEAD_DIM), BF16),
                        pltpu.VMEM((HEAD_DIM, S), BF16),
                        pltpu.VMEM((HQ, HEAD_DIM), BF16), pltpu.VMEM((HQ, HEAD_DIM), BF16),
                        pltpu.VMEM((S, Q_BLOCK), jnp.int32),
                        pltpu.VMEM((1, Q_BLOCK), jnp.int32), pltpu.VMEM((1, Q_BLOCK), jnp.int32),
                        pltpu.VMEM((1, HQ), F32), pltpu.VMEM((1, HQ), F32), pltpu.VMEM((HEAD_DIM, HQ), F32)],
        compiler_params=_cparams(("parallel", "arbitrary")),
        name="dsa_attention",
    )(proj, proj, proj, proj, proj, q_gain, k_gain, bias)


NSA_G = NSA_HEADS // NSA_KV_HEADS
CMP_PAD = 56
CMP_NEAR = 64
SEL_TAKEN = -3e38


def _cmp_rows(S):
    return -(-(CMP_PAD + S // CMP_STRIDE) // 64) * 64


def _nsa_prep_body(ks_ref, vs_ref, kw_ref, vw_ref, kg_ref, kso_ref, vso_ref, kwo_ref, vwo_ref):
    dh = HEAD_DIM
    for g in range(NSA_KV_HEADS):
        cs = slice(g * dh, (g + 1) * dh)
        kso_ref[0, g] = _head_rms(ks_ref[:, cs], kg_ref[1:2, :], 1.0).astype(BF16)
        kwo_ref[0, g] = _head_rms(kw_ref[:, cs], kg_ref[2:3, :], 1.0).astype(BF16)
        vso_ref[0, g] = vs_ref[:, cs].T.astype(BF16)
        vwo_ref[0, g] = vw_ref[:, cs].T.astype(BF16)


def nsa_prep(proj, k_gain, B, S, ts=512):
    nt = S // ts
    c0 = COL["n_kv"] // 128
    col = lambda j: pl.BlockSpec((ts, 128), lambda b, t: (b * nt + t, c0 + j))
    k_out = pl.BlockSpec((1, NSA_KV_HEADS, ts, HEAD_DIM), lambda b, t: (b, 0, t, 0))
    v_out = pl.BlockSpec((1, NSA_KV_HEADS, HEAD_DIM, ts), lambda b, t: (b, 0, 0, t))
    k_sds = jax.ShapeDtypeStruct((B, NSA_KV_HEADS, S, HEAD_DIM), BF16)
    v_sds = jax.ShapeDtypeStruct((B, NSA_KV_HEADS, HEAD_DIM, S), BF16)
    return pl.pallas_call(
        _nsa_prep_body,
        grid=(B, nt),
        in_specs=[col(2), col(3), col(4), col(5), pl.BlockSpec((3, HEAD_DIM), lambda b, t: (0, 0))],
        out_specs=[k_out, v_out, k_out, v_out],
        out_shape=[k_sds, v_sds, k_sds, v_sds],
        compiler_params=_cparams(("parallel", "parallel")),
        name="nsa_prep",
    )(proj, proj, proj, proj, k_gain)


def _nsa_compress_body(xk_ref, xv_ref, pe_ref, w1_ref, w2_ref, kg_ref, kc_ref, vct_ref):
    R = xk_ref.shape[3]
    ncp = kc_ref.shape[2]
    half = CMP_STRIDE * HEAD_DIM
    row = lax.broadcasted_iota(jnp.int32, (R, HEAD_DIM), 0)
    for j, x_ref in enumerate((xk_ref, xv_ref)):
        x = x_ref[0, 0, 0]
        pe = pe_ref[j]
        lo = jnp.dot((x + pe[:, :half]).astype(BF16), w1_ref[j, :half, :], preferred_element_type=F32)
        hi = jnp.dot((x + pe[:, half:]).astype(BF16), w1_ref[j, half:, :], preferred_element_type=F32)
        hid = jax.nn.gelu(lo + pltpu.roll(hi, R - 1, 0))
        c = jnp.dot(hid.astype(BF16), w2_ref[j], preferred_element_type=F32)
        if j == 0:
            c = _head_rms(c, kg_ref[0:1, :], 1.0)
        c = jnp.where(row < R - 1, c, 0.0)
        c = jnp.concatenate([jnp.zeros((CMP_PAD, HEAD_DIM), F32), c,
                             jnp.zeros((ncp - CMP_PAD - R, HEAD_DIM), F32)], axis=0)
        if j == 0:
            kc_ref[0, 0] = c.astype(BF16)
        else:
            vct_ref[0, 0] = c.T.astype(BF16)


def nsa_compress(xc, pe, w1, w2, k_gain, S):
    B = xc.shape[0]
    R = S // CMP_STRIDE
    ncp = _cmp_rows(S)
    const = lambda *shape: pl.BlockSpec(shape, lambda b, g: (0,) * len(shape), pipeline_mode=pl.Buffered(1))
    return pl.pallas_call(
        _nsa_compress_body,
        grid=(B, NSA_KV_HEADS),
        in_specs=[pl.BlockSpec((1, 1, 1, R, CMP_STRIDE * HEAD_DIM), lambda b, g: (b, 0, g, 0, 0)),
                  pl.BlockSpec((1, 1, 1, R, CMP_STRIDE * HEAD_DIM), lambda b, g: (b, 1, g, 0, 0)),
                  const(2, 1, CMP_BLOCK * HEAD_DIM), const(2, CMP_BLOCK * HEAD_DIM, CMP_HIDDEN),
                  const(2, CMP_HIDDEN, HEAD_DIM), const(3, HEAD_DIM)],
        out_specs=[pl.BlockSpec((1, 1, ncp, HEAD_DIM), lambda b, g: (b, g, 0, 0)),
                   pl.BlockSpec((1, 1, HEAD_DIM, ncp), lambda b, g: (b, g, 0, 0))],
        out_shape=[jax.ShapeDtypeStruct((B, NSA_KV_HEADS, ncp, HEAD_DIM), BF16),
                   jax.ShapeDtypeStruct((B, NSA_KV_HEADS, HEAD_DIM, ncp), BF16)],
        compiler_params=_cparams(("parallel", "parallel")),
        name="nsa_compress",
    )(xc, xc, pe, w1, w2, k_gain)


def _attend_block(k_b, v_b, q_scr, bias_ref, d0, masks, m_scr, l_scr, acc_scr):
    QB = Q_BLOCK
    CH = 2 * QB
    n_sub = len(masks)
    masks2 = [jnp.concatenate([mk, mk], axis=1) for mk in masks]
    d = [jnp.clip(d0 - j, 0, BIAS_TILES - 1) for j in range(n_sub)]
    chunks = [slice(c * CH, (c + 1) * CH) for c in range(q_scr.shape[0] // CH)]
    logits = [lax.dot_general(k_b, q_scr[ls, :], (((1,), (1,)), ((), ())), preferred_element_type=F32)
              for ls in chunks]
    for ls, lg in zip(chunks, logits):
        parts = [jnp.where(masks2[j], lg[j * QB:(j + 1) * QB] + bias_ref[d[j], :, ls], NEG_INF)
                 for j in range(n_sub)]
        m_old = m_scr[:, ls]
        m_new = m_old
        for part in parts:
            m_new = jnp.maximum(m_new, jnp.max(part, axis=0, keepdims=True))
        probs = [jnp.exp2(part - m_new) for part in parts]
        alpha = jnp.exp2(m_old - m_new)
        l_new = alpha * l_scr[:, ls]
        for pr in probs:
            l_new = l_new + jnp.sum(pr, axis=0, keepdims=True)
        l_scr[:, ls] = l_new
        pb = jnp.concatenate([pr.astype(BF16) for pr in probs], axis=0)
        acc_scr[:, ls] = alpha * acc_scr[:, ls] + jnp.dot(v_b, pb, preferred_element_type=F32)
        m_scr[:, ls] = m_new


def _nsa_body(q_ref, gl_ref, kc_ref, vct_ref, ks_ref, vst_ref, kw_ref, vwt_ref, qg_ref, bias_ref,
              gcd_ref, c31_ref, ovl_ref, o_ref,
              qall_scr, lc_scr, sel_scr, gt_scr, ms_scr, ls_scr, as_scr, mw_scr, lw_scr, aw_scr,
              *, n_sel, top_n):
    g = pl.program_id(1)
    i = pl.program_id(2)
    dh = HEAD_DIM
    QB = Q_BLOCK
    GQ = NSA_G * QB
    ncp = kc_ref.shape[2]

    q = q_ref[...]
    for r in range(NSA_G):
        qall_scr[r * QB:(r + 1) * QB, :] = _head_rms(q[:, r * dh:(r + 1) * dh], qg_ref[...],
                                                     dh ** -0.5 * LOG2E).astype(BF16)
    gt_scr[...] = jax.nn.sigmoid(gl_ref[...]).T

    lc_scr[...] = lax.dot_general(kc_ref[0, 0], qall_scr[...], (((1,), (1,)), ((), ())),
                                  preferred_element_type=F32) + c31_ref[0]
    near = pl.ds(pl.multiple_of(i * (QB // CMP_STRIDE), 8), CMP_NEAR)
    lc_scr[near, :] = lc_scr[near, :] + gcd_ref[0]
    n_p = lax.broadcasted_iota(jnp.int32, (ncp, GQ), 0)
    t_c = i * QB + (lax.broadcasted_iota(jnp.int32, (ncp, GQ), 1) & (QB - 1))
    cmp_end = jnp.where(n_p >= CMP_PAD, (n_p - CMP_PAD) * CMP_STRIDE + (CMP_BLOCK - 1), 1 << 30)
    mask_c = cmp_end <= t_c
    lc = jnp.where(mask_c, lc_scr[...], NEG_INF)
    e = jnp.where(mask_c, jnp.exp2(lc - jnp.max(lc, axis=0, keepdims=True)), 0.0)
    pc = (e / jnp.maximum(jnp.sum(e, axis=0, keepdims=True), 1e-30)).astype(BF16)
    o_c = jnp.dot(vct_ref[0, 0], pc, preferred_element_type=F32)

    imp = jnp.zeros((n_sel, QB), F32)
    for r in range(NSA_G):
        imp = imp + jnp.dot(ovl_ref[...], pc[:, r * QB:(r + 1) * QB], preferred_element_type=F32)
    j_io = lax.broadcasted_iota(jnp.int32, (n_sel, QB), 0).astype(F32)
    cur = ((i * QB + lax.broadcasted_iota(jnp.int32, (n_sel, QB), 1)) >> 6).astype(F32)
    imp = jnp.where((j_io == 0) | (j_io == cur), FORCED_SCORE, imp)
    imp = jnp.where(j_io <= cur, imp, NEG_INF)
    sel = jnp.zeros((n_sel, QB), F32)
    for _ in range(top_n):
        mx = jnp.max(imp, axis=0, keepdims=True)
        first = jnp.min(jnp.where(imp == mx, j_io, float(n_sel)), axis=0, keepdims=True)
        hit = j_io == first
        sel = jnp.where(hit, 1.0, sel)
        imp = jnp.where(hit, SEL_TAKEN, imp)
    sel_scr[...] = sel

    for m_scr, l_scr, a_scr in ((ms_scr, ls_scr, as_scr), (mw_scr, lw_scr, aw_scr)):
        m_scr[...] = jnp.full(m_scr.shape, NEG_INF, F32)
        l_scr[...] = jnp.zeros(l_scr.shape, F32)
        a_scr[...] = jnp.zeros(a_scr.shape, F32)
    s_loc = lax.broadcasted_iota(jnp.int32, (QB, QB), 0)
    t_loc = lax.broadcasted_iota(jnp.int32, (QB, QB), 1)
    blocks_per_tile = QB // SEL_BLOCK

    def slc_block(kb, carry):
        rows = pl.ds(pl.multiple_of(kb * KEY_BLOCK, KEY_BLOCK), KEY_BLOCK)
        masks = []
        for j in range(KEY_TILES):
            kt = kb * KEY_TILES + j
            picked = jnp.where(s_loc < SEL_BLOCK, sel_scr[pl.ds(kt * blocks_per_tile, 1), :],
                               sel_scr[pl.ds(kt * blocks_per_tile + 1, 1), :])
            dist = (t_loc + i * QB) - (s_loc + kt * QB)
            masks.append(jnp.where(dist >= 0, picked, 0.0) > 0.5)
        _attend_block(ks_ref[0, 0, rows, :], vst_ref[0, 0, :, rows], qall_scr, bias_ref,
                      i - kb * KEY_TILES, masks, ms_scr, ls_scr, as_scr)
        return carry

    lax.fori_loop(0, (i + KEY_TILES) // KEY_TILES, slc_block, 0)

    kt0 = jnp.maximum(i - WINDOW // QB, 0)
    rows = pl.ds(pl.multiple_of(kt0 * QB, QB), WIN_TILES * QB)
    masks = []
    for j in range(WIN_TILES):
        dist = (t_loc + i * QB) - (s_loc + (kt0 + j) * QB)
        masks.append((dist >= 0) & (dist < WINDOW))
    _attend_block(kw_ref[0, 0, rows, :], vwt_ref[0, 0, :, rows], qall_scr, bias_ref, i - kt0, masks,
                  mw_scr, lw_scr, aw_scr)

    o_s = as_scr[...] / jnp.maximum(ls_scr[...], 1e-30)
    o_w = aw_scr[...] / jnp.maximum(lw_scr[...], 1e-30)
    outs = []
    for r in range(NSA_G):
        ls = slice(r * QB, (r + 1) * QB)
        gate = lambda j: gt_scr[pl.ds(j * NSA_HEADS + g * NSA_G + r, 1), :]
        o_r = gate(0) * o_c[:, ls] + gate(1) * o_s[:, ls] + gate(2) * o_w[:, ls]
        outs.append(o_r.T)
    o_ref[...] = jnp.concatenate(outs, axis=1)


def nsa_attention(proj, kc, vct, ks, vst, kw, vwt, q_gain, bias, gcd, c31, ovl, B, S):
    nq = S // Q_BLOCK
    GQ = NSA_G * Q_BLOCK
    GW = NSA_G * HEAD_DIM
    ncp = kc.shape[2]
    n_sel = S // SEL_BLOCK
    const = lambda *shape: pl.BlockSpec(shape, lambda b, g, i: (0,) * len(shape), pipeline_mode=pl.Buffered(1))
    per_group = lambda *shape: pl.BlockSpec((1, 1) + shape, lambda b, g, i: (b, g, 0, 0))
    return pl.pallas_call(
        functools.partial(_nsa_body, n_sel=n_sel, top_n=min(SEL_TOPN, n_sel)),
        grid=(B, NSA_KV_HEADS, nq),
        in_specs=[pl.BlockSpec((Q_BLOCK, GW), lambda b, g, i: (b * nq + i, COL["n_q"] // GW + g)),
                  pl.BlockSpec((Q_BLOCK, 128), lambda b, g, i: (b * nq + i, COL["n_g"] // 128)),
                  per_group(ncp, HEAD_DIM), per_group(HEAD_DIM, ncp),
                  per_group(S, HEAD_DIM), per_group(HEAD_DIM, S),
                  per_group(S, HEAD_DIM), per_group(HEAD_DIM, S),
                  const(1, HEAD_DIM),
                  pl.BlockSpec((BIAS_TILES, Q_BLOCK, GQ), lambda b, g, i: (0, 0, g)),
                  pl.BlockSpec((1, CMP_NEAR, GQ), lambda b, g, i: (g, 0, 0)),
                  pl.BlockSpec((1, 1, GQ), lambda b, g, i: (g, 0, 0)),
                  const(n_sel, ncp)],
        out_specs=pl.BlockSpec((Q_BLOCK, GW), lambda b, g, i: (b * nq + i, g)),
        out_shape=jax.ShapeDtypeStruct((B * S, MIX_W), F32),
        scratch_shapes=[pltpu.VMEM((GQ, HEAD_DIM), BF16), pltpu.VMEM((ncp, GQ), F32),
                        pltpu.VMEM((n_sel, Q_BLOCK), F32), pltpu.VMEM((128, Q_BLOCK), F32)]
                       + [pltpu.VMEM((1, GQ), F32), pltpu.VMEM((1, GQ), F32), pltpu.VMEM((HEAD_DIM, GQ), F32)] * 2,
        compiler_params=_cparams(("parallel", "parallel", "arbitrary")),
        name="nsa_attention",
    )(proj, proj, kc, vct, ks, vst, kw, vwt, q_gain, bias, gcd, c31, ovl)


def nsa_tables(tbl, S):
    ncp = _cmp_rows(S)
    n_rel = jnp.arange(CMP_NEAR)[:, None] - CMP_PAD
    t_loc = jnp.arange(Q_BLOCK)[None, :]
    near = tbl[rel_bucket(t_loc - CMP_STRIDE * n_rel - (CMP_BLOCK - 1))] * LOG2E
    far = tbl[N_BUCKETS - 1] * LOG2E
    gcd = jnp.moveaxis(near - far, -1, 0).reshape(NSA_KV_HEADS, NSA_G, CMP_NEAR, Q_BLOCK)
    gcd = jnp.moveaxis(gcd, 1, 2).reshape(NSA_KV_HEADS, CMP_NEAR, NSA_G * Q_BLOCK)
    c31 = jnp.repeat(far, Q_BLOCK).reshape(NSA_KV_HEADS, 1, NSA_G * Q_BLOCK)
    n = np.arange(ncp) - CMP_PAD
    sel_start = np.arange(S // SEL_BLOCK) * SEL_BLOCK
    start = n * CMP_STRIDE
    ovl = ((start[None, :] < sel_start[:, None] + SEL_BLOCK) & (start[None, :] + CMP_BLOCK > sel_start[:, None])
           & (n[None, :] >= 0) & (n[None, :] < (S - CMP_BLOCK) // CMP_STRIDE + 1))
    return gcd, c31, jnp.asarray(ovl, BF16)


def nsa_mixer_pallas(proj, cmp_pe, cmp_w1, cmp_w2, q_gain, k_gain, rel_tbl, bias, B, S):
    dh = HEAD_DIM
    c0 = COL["n_kv"]
    xc = proj[:, c0:c0 + 2 * NSA_KV_HEADS * dh].reshape(B, S // CMP_STRIDE, CMP_STRIDE, 2, NSA_KV_HEADS, dh)
    xc = xc.transpose(0, 3, 4, 1, 2, 5).reshape(B, 2, NSA_KV_HEADS, S // CMP_STRIDE, CMP_STRIDE * dh)
    kc, vct = nsa_compress(xc, cmp_pe.reshape(2, 1, CMP_BLOCK * dh), cmp_w1.astype(BF16),
                           cmp_w2.astype(BF16), k_gain, S)
    ks, vst, kw, vwt = nsa_prep(proj, k_gain, B, S)
    gcd, c31, ovl = nsa_tables(rel_tbl, S)
    return nsa_attention(proj, kc, vct, ks, vst, kw, vwt, q_gain[None], bias, gcd, c31, ovl, B, S)


def rel_bucket(dist):
    n = jnp.maximum(dist, 0)
    max_exact = N_BUCKETS // 2
    nf = jnp.maximum(n, 1).astype(jnp.float32)
    large = max_exact + (jnp.log(nf / max_exact) / math.log(MAX_DISTANCE / max_exact)
                         * (N_BUCKETS - max_exact)).astype(jnp.int32)
    return jnp.where(n < max_exact, n, jnp.minimum(large, N_BUCKETS - 1))


CONV_HALO = 8


def _conv_body(b_ref, c_ref, x_ref, w_ref, o_ref, prev_scr):
    @pl.when(pl.program_id(1) == 0)
    def _():
        prev_scr[...] = jnp.zeros(prev_scr.shape, F32)

    bx = c_ref[...] * x_ref[...]
    ts = bx.shape[0]
    row = lax.broadcasted_iota(jnp.int32, bx.shape, 0)
    prev = prev_scr[...]
    last1 = prev[CONV_HALO - 1:CONV_HALO]
    last2 = prev[CONV_HALO - 2:CONV_HALO - 1]
    back1 = jnp.where(row == 0, last1, pltpu.roll(bx, 1, 0))
    back2 = jnp.where(row == 0, last2, jnp.where(row == 1, last1, pltpu.roll(bx, 2, 0)))
    w = w_ref[...]
    o_ref[...] = b_ref[...] * (w[0:1] * back2 + w[1:2] * back1 + w[2:3] * bx)
    prev_scr[...] = bx[ts - CONV_HALO:ts]


def conv_mixer(proj, conv_w, B, S, ts=512):
    nt = S // ts
    col = lambda name: pl.BlockSpec((ts, MIX_W), lambda b, t: (b * nt + t, COL[name] // MIX_W))
    return pl.pallas_call(
        _conv_body,
        grid=(B, nt),
        in_specs=[col("a_b"), col("a_c"), col("a_x"), pl.BlockSpec((CONV_K, MIX_W), lambda b, t: (0, 0))],
        out_specs=pl.BlockSpec((ts, MIX_W), lambda b, t: (b * nt + t, 0)),
        out_shape=jax.ShapeDtypeStruct((B * S, MIX_W), F32),
        scratch_shapes=[pltpu.VMEM((CONV_HALO, MIX_W), F32)],
        compiler_params=_cparams(("parallel", "arbitrary")),
        name="conv_mixer",
    )(proj, proj, proj, conv_w)


def _sgu_body(uv_ref, g_ref, w_ref, b_ref, o_ref):
    uv = jax.nn.gelu(uv_ref[...])
    u = uv[:, :MIX_W]
    v = uv[:, MIX_W:]
    v = v - jnp.mean(v, axis=-1, keepdims=True)
    v = (v * lax.rsqrt(jnp.mean(v * v, axis=-1, keepdims=True) + EPS) * g_ref[...]).astype(BF16)
    gw = MIX_W // SGU_GROUPS
    group = lax.broadcasted_iota(jnp.int32, (SGU_CHUNK, MIX_W), 1) >> (gw.bit_length() - 1)
    for c in range(uv.shape[0] // SGU_CHUNK):
        rows = slice(c * SGU_CHUNK, (c + 1) * SGU_CHUNK)
        vc = v[rows]
        s = b_ref[...]
        for g in range(SGU_GROUPS):
            s = s + jnp.dot(w_ref[g], jnp.where(group == g, vc, jnp.zeros_like(vc)),
                            preferred_element_type=F32)
        o_ref[rows, :] = u[rows] * s


def sgu_mixer_pallas(proj, ln_gain, w_s, b_s, T, tm=512):
    tri = jnp.tril(jnp.ones((SGU_CHUNK, SGU_CHUNK), dtype=bool))
    w = jnp.where(tri[None], w_s, 0).astype(BF16)
    b = jnp.repeat(b_s.T, MIX_W // SGU_GROUPS, axis=1)
    const = lambda *shape: pl.BlockSpec(shape, lambda i: (0,) * len(shape), pipeline_mode=pl.Buffered(1))
    return pl.pallas_call(
        _sgu_body,
        grid=(T // tm,),
        in_specs=[pl.BlockSpec((tm, 2 * MIX_W), lambda i: (i, COL["c_uv"] // (2 * MIX_W))),
                  const(1, MIX_W), const(SGU_GROUPS, SGU_CHUNK, SGU_CHUNK), const(SGU_CHUNK, MIX_W)],
        out_specs=pl.BlockSpec((tm, MIX_W), lambda i: (i, 0)),
        out_shape=jax.ShapeDtypeStruct((T, MIX_W), F32),
        compiler_params=_cparams(("parallel",)),
        name="sgu_mixer",
    )(proj, ln_gain[None], w, b)


def _layer(x2, p2, B, S, rel_bias, g_mix, w_in, conv_w, nsa_cmp_pe, nsa_cmp_w1, nsa_cmp_w2,
           nsa_q_gain, nsa_k_gain, sgu_ln_gain, sgu_w, sgu_b, dsa_q_gain, dsa_k_gain,
           w_gate, w_branch, w_out, g_ffn, peer_wq, peer_subkeys, peer_u, peer_v,
           g_ple, w_ple_gate, w_ple_proj):
    T = B * S
    proj = rms_matmul(x2, g_mix[None], pack_w_in(w_in), tm=512, tn=768)

    y_a = conv_mixer(proj, conv_w, B, S)
    y_b = nsa_mixer_pallas(proj, nsa_cmp_pe, nsa_cmp_w1, nsa_cmp_w2, nsa_q_gain, nsa_k_gain,
                           rel_bias[:, :NSA_HEADS], rel_bias_tiles(rel_bias[:, :NSA_HEADS]), B, S)
    y_c = sgu_mixer_pallas(proj, sgu_ln_gain, sgu_w, sgu_b, T)
    y_d = dsa_attention(proj, COL, dsa_q_gain[None], dsa_k_gain[None],
                        rel_bias_tiles(rel_bias[:, NSA_HEADS:]), B, S)
    ys = [y_a, y_b, y_c, y_d]

    x2 = merge_mixers(x2, g_mix[None], ys, w_gate.astype(BF16),
                      w_branch.reshape(N_MIXERS, MIX_W, D_MODEL).astype(BF16),
                      w_out.astype(BF16), tm=256)

    sk = peer_subkeys.reshape(2 * PEER_HEADS, PEER_KEYS, PEER_QDIM // 2).astype(BF16)
    x2 = peer_ffn_update(x2, g_ffn[None], peer_wq.astype(BF16), sk,
                         peer_u.astype(BF16), peer_v.astype(BF16).T)

    x2 = ple_update(x2, g_ple[None], p2, w_ple_gate.astype(BF16), w_ple_proj.astype(BF16), tm=512)
    return x2


def kernel(x, p, rel_bias, g_mix, w_in, conv_w, nsa_cmp_pe, nsa_cmp_w1, nsa_cmp_w2, nsa_q_gain,
           nsa_k_gain, sgu_ln_gain, sgu_w, sgu_b, dsa_q_gain, dsa_k_gain, w_gate, w_branch, w_out,
           g_ffn, peer_wq, peer_subkeys, peer_u, peer_v, g_ple, w_ple_gate, w_ple_proj):
    B, S, D = x.shape
    depth = p.shape[0]
    x2 = x.reshape(B * S, D)
    for l in range(depth):
        x2 = _layer(x2, p[l].reshape(B * S, PLE_DIM), B, S, rel_bias, g_mix[l], w_in[l], conv_w[l],
                    nsa_cmp_pe[l], nsa_cmp_w1[l], nsa_cmp_w2[l], nsa_q_gain[l], nsa_k_gain[l],
                    sgu_ln_gain[l], sgu_w[l], sgu_b[l], dsa_q_gain[l], dsa_k_gain[l],
                    w_gate[l], w_branch[l], w_out[l], g_ffn[l], peer_wq[l], peer_subkeys[l],
                    peer_u[l], peer_v[l], g_ple[l], w_ple_gate[l], w_ple_proj[l])
    return x2.reshape(B, S, D)
```

```python
import functools
import math

import jax
import jax.numpy as jnp
import numpy as np
from jax import lax
from jax.experimental import pallas as pl
from jax.experimental.pallas import tpu as pltpu

F32 = jnp.float32
BF16 = jnp.bfloat16

D_MODEL = 1024
HEAD_DIM = 64
N_MIXERS = 4
MIX_W = D_MODEL // 2
Q_BLOCK = 128
EPS = 1e-6
NEG_INF = -1e30
CONV_K = 3
NSA_HEADS = MIX_W // HEAD_DIM
NSA_KV_HEADS = 2
CMP_BLOCK = 32
CMP_STRIDE = 16
CMP_HIDDEN = 256
SEL_BLOCK = 64
SEL_TOPN = 8
WINDOW = 512
FORCED_SCORE = 1e4
SGU_CHUNK = 128
SGU_GROUPS = 8
DSA_HEADS = MIX_W // HEAD_DIM
IDX_HEADS = 8
IDX_DIM = 64
DSA_TOPK_MAX = 256
N_BUCKETS = 32
MAX_DISTANCE = 1024
PEER_HEADS = 8
PEER_KEYS = 128
PEER_QDIM = 128
PEER_TOPK = 16
N_EXPERTS = PEER_KEYS * PEER_KEYS
PLE_DIM = 256

SPLIT_WIDTHS = (
    MIX_W, MIX_W, MIX_W,
    NSA_HEADS * HEAD_DIM,
    6 * NSA_KV_HEADS * HEAD_DIM,
    3 * NSA_HEADS,
    2 * MIX_W,
    DSA_HEADS * HEAD_DIM, HEAD_DIM, HEAD_DIM,
    IDX_HEADS * IDX_DIM, IDX_DIM, IDX_HEADS,
)
IN_WIDTH = sum(SPLIT_WIDTHS)

COL = dict(a_b=0, a_c=512, a_x=1024, n_q=1536, d_q=2048, d_qi=2560, c_uv=3072, n_kv=4096,
           n_g=4864, d_kv=4992, d_kiw=5120)
N_PACK = 5376


def pack_w_in(w):
    o = dict(zip(("a_b", "a_c", "a_x", "n_q", "n_kv", "n_g", "c_uv", "d_q", "d_k", "d_v", "d_qi", "d_ki",
                  "d_wi", "end"), [0] + [int(c) for c in np.cumsum(SPLIT_WIDTHS)]))
    z = lambda n: jnp.zeros((w.shape[0], n), w.dtype)
    cols = [w[:, o["a_b"]:o["n_kv"]], w[:, o["d_q"]:o["d_k"]], w[:, o["d_qi"]:o["d_ki"]],
            w[:, o["c_uv"]:o["d_q"]], w[:, o["n_kv"]:o["n_g"]],
            w[:, o["n_g"]:o["c_uv"]], z(128 - 3 * NSA_HEADS),
            w[:, o["d_k"]:o["d_qi"]],
            w[:, o["d_ki"]:o["end"]], z(128 - IDX_DIM - IDX_HEADS)]
    packed = jnp.concatenate(cols, axis=1)
    return jnp.pad(packed, ((0, 0), (0, N_PACK - packed.shape[1]))).astype(BF16)

VMEM_LIMIT_BYTES = 56 * 1024 * 1024


def _cparams(sem):
    return pltpu.CompilerParams(dimension_semantics=sem, vmem_limit_bytes=VMEM_LIMIT_BYTES)


def _rms(x, g):
    return x * lax.rsqrt(jnp.mean(x * x, axis=-1, keepdims=True) + EPS) * g


def _rms_matmul_body(x_ref, g_ref, w_ref, o_ref, h_ref):
    @pl.when(pl.program_id(1) == 0)
    def _():
        h_ref[...] = _rms(x_ref[...], g_ref[...]).astype(BF16)

    o_ref[...] = jnp.dot(h_ref[...], w_ref[...], preferred_element_type=F32)


def rms_matmul(x, g, w, tm, tn):
    T, D = x.shape
    N = w.shape[1]
    return pl.pallas_call(
        _rms_matmul_body,
        grid=(T // tm, N // tn),
        in_specs=[pl.BlockSpec((tm, D), lambda i, j: (i, 0)),
                  pl.BlockSpec((1, D), lambda i, j: (0, 0)),
                  pl.BlockSpec((D, tn), lambda i, j: (0, j))],
        out_specs=pl.BlockSpec((tm, tn), lambda i, j: (i, j)),
        out_shape=jax.ShapeDtypeStruct((T, N), F32),
        scratch_shapes=[pltpu.VMEM((tm, D), BF16)],
        compiler_params=_cparams(("parallel", "arbitrary")),
        name="rms_matmul",
    )(x, g, w)


def _merge_body(x_ref, g_ref, ya_ref, yb_ref, yc_ref, yd_ref, wg_ref, wb_ref, wo_ref, o_ref):
    x = x_ref[...]
    h = _rms(x, g_ref[...]).astype(BF16)
    merged = jnp.zeros(x.shape, F32)
    for m, y_ref in enumerate((ya_ref, yb_ref, yc_ref, yd_ref)):
        z = jnp.dot(y_ref[...].astype(BF16), wb_ref[m], preferred_element_type=F32)
        gate = jax.nn.sigmoid(jnp.dot(h, wg_ref[:, m * D_MODEL:(m + 1) * D_MODEL],
                                      preferred_element_type=F32))
        merged = merged + gate * z
    o_ref[...] = x + jnp.dot(merged.astype(BF16), wo_ref[...], preferred_element_type=F32)


def merge_mixers(x, g, ys, w_gate, w_branch, w_out, tm):
    T, D = x.shape
    const = lambda *shape: pl.BlockSpec(shape, lambda i: (0,) * len(shape), pipeline_mode=pl.Buffered(1))
    return pl.pallas_call(
        _merge_body,
        grid=(T // tm,),
        in_specs=[pl.BlockSpec((tm, D), lambda i: (i, 0)),
                  const(1, D)]
                 + [pl.BlockSpec((tm, MIX_W), lambda i: (i, 0)) for _ in range(N_MIXERS)]
                 + [const(D, N_MIXERS * D), const(N_MIXERS, MIX_W, D), const(D, D)],
        out_specs=pl.BlockSpec((tm, D), lambda i: (i, 0)),
        out_shape=jax.ShapeDtypeStruct((T, D), F32),
        compiler_params=_cparams(("parallel",)),
        name="merge_mixers",
    )(x, g, *ys, w_gate, w_branch, w_out)


def _ple_body(x_ref, g_ref, p_ref, wg_ref, wp_ref, o_ref):
    x = x_ref[...]
    h = _rms(x, g_ref[...]).astype(BF16)
    gate = jax.nn.sigmoid(jnp.dot(h, wg_ref[...], preferred_element_type=F32))
    proj = jnp.dot(p_ref[...].astype(BF16), wp_ref[...], preferred_element_type=F32)
    o_ref[...] = x + gate * proj


def ple_update(x, g, p, w_gate, w_proj, tm):
    T, D = x.shape
    const = lambda *shape: pl.BlockSpec(shape, lambda i: (0,) * len(shape), pipeline_mode=pl.Buffered(1))
    return pl.pallas_call(
        _ple_body,
        grid=(T // tm,),
        in_specs=[pl.BlockSpec((tm, D), lambda i: (i, 0)), const(1, D),
                  pl.BlockSpec((tm, PLE_DIM), lambda i: (i, 0)),
                  const(D, D), const(PLE_DIM, D)],
        out_specs=pl.BlockSpec((tm, D), lambda i: (i, 0)),
        out_shape=jax.ShapeDtypeStruct((T, D), F32),
        compiler_params=_cparams(("parallel",)),
        name="ple_update",
    )(x, g, p, w_gate, w_proj)


def _extract_top16(cur, tops_ref):
    m = None
    for r in range(PEER_TOPK):
        m = jnp.max(cur, axis=0, keepdims=True)
        if tops_ref is not None:
            tops_ref[r:r + 1, :] = m
        cur = jnp.where(cur == m, NEG_INF, cur)
    return m


_CAND_ROWS_J = (16, 8, 5, 4, 3, 2, 2, 2)


def _peer_topk_body(x_ref, g_ref, wq_ref, sk_ref, hn_ref, cut_ref, e1_ref, s2_ref, e2_ref,
                    t1_ref, t2_ref):
    hn = _rms(x_ref[...], g_ref[...]).astype(BF16)
    hn_ref[...] = hn
    q = jnp.dot(hn, wq_ref[...], preferred_element_type=F32).astype(BF16)
    half = PEER_QDIM // 2
    tn = x_ref.shape[0]
    j_iota = lax.broadcasted_iota(jnp.int32, (8, LANE_CHUNK), 0)
    for h in range(PEER_HEADS):
        scores = []
        for p in range(2):
            c0 = (2 * h + p) * half
            scores.append(lax.dot_general(sk_ref[2 * h + p], q[:, c0:c0 + half],
                                          (((1,), (1,)), ((), ())), preferred_element_type=F32))
        for c in range(tn // LANE_CHUNK):
            ls = slice(c * LANE_CHUNK, (c + 1) * LANE_CHUNK)
            s = [sc[:, ls] for sc in scores]
            th1 = _extract_top16(s[0], t1_ref)
            th2 = _extract_top16(s[1], t2_ref)
            a = t1_ref[...]
            b = t2_ref[...]
            b8 = b[0:8]
            pieces = [a[0:1] + b]
            for i in range(1, 8):
                cand_i = a[i:i + 1] + b8
                if _CAND_ROWS_J[i] < 8:
                    cand_i = jnp.where(j_iota < _CAND_ROWS_J[i], cand_i, NEG_INF)
                pieces.append(cand_i)
            pieces.append(a[8:16] + b[0:1])
            cand = jnp.concatenate(pieces, axis=0)
            top = a[0:1] + b[0:1]
            theta = _extract_top16(cand, None)
            z = jnp.sum(jnp.where(cand >= theta, jnp.exp(cand - top), 0.0), axis=0, keepdims=True)
            s1 = jnp.where(s[0] >= th1, s[0], NEG_INF)
            s2 = jnp.where(s[1] >= th2, s[1], NEG_INF)
            cut = jnp.full(s1.shape, -NEG_INF, F32)
            for j in range(PEER_TOPK):
                bj = b[j:j + 1]
                cut = jnp.minimum(cut, jnp.where(s1 + bj >= theta, bj, -NEG_INF))
            cut_ref[h, :, ls] = cut
            s2_ref[h, :, ls] = s2
            e1_ref[h, :, ls] = jnp.exp(s1 - (a[0:1] + jnp.log(z)))
            e2_ref[h, :, ls] = jnp.exp(s2 - b[0:1])


def peer_topk(x, g, wq, subkeys, tn):
    T, D = x.shape
    const = lambda *shape: pl.BlockSpec(shape, lambda i: (0,) * len(shape), pipeline_mode=pl.Buffered(1))
    return pl.pallas_call(
        _peer_topk_body,
        grid=(T // tn,),
        in_specs=[pl.BlockSpec((tn, D), lambda i: (i, 0)), const(1, D), const(D, D),
                  const(2 * PEER_HEADS, PEER_KEYS, PEER_QDIM // 2)],
        out_specs=[pl.BlockSpec((tn, D), lambda i: (i, 0))]
                  + [pl.BlockSpec((PEER_HEADS, PEER_KEYS, tn), lambda i: (0, 0, i))] * 4,
        out_shape=[jax.ShapeDtypeStruct((T, D), BF16)]
                  + [jax.ShapeDtypeStruct((PEER_HEADS, PEER_KEYS, T), F32)] * 4,
        scratch_shapes=[pltpu.VMEM((PEER_TOPK, LANE_CHUNK), F32), pltpu.VMEM((PEER_TOPK, LANE_CHUNK), F32)],
        compiler_params=_cparams(("parallel",)),
        name="peer_topk",
    )(x, g, wq, subkeys)


LANE_CHUNK = 128


PEER_I1_STEP = 8


def _peer_main_body(hn_ref, cut_ref, e1_ref, s2_ref, e2_ref, u_ref, vt_prev_ref, vt_ref, x_ref, o_ref,
                    acc_ref, p_ref):
    j = pl.program_id(1)
    tn = hn_ref.shape[0]
    slot = j % 2

    @pl.when(j == 0)
    def _():
        acc_ref[...] = jnp.zeros(acc_ref.shape, F32)
        p_ref[1] = jnp.zeros(p_ref.shape[1:], BF16)

    hn = hn_ref[...]
    for k in range(PEER_I1_STEP):
        rows = slice(k * PEER_KEYS, (k + 1) * PEER_KEYS)
        at = lax.dot_general(u_ref[rows, :], hn, (((1,), (1,)), ((), ())),
                             preferred_element_type=F32)
        for c in range(tn // LANE_CHUNK):
            ls = slice(c * LANE_CHUNK, (c + 1) * LANE_CHUNK)
            w = jnp.zeros((PEER_KEYS, LANE_CHUNK), F32)
            for h in range(PEER_HEADS):
                picked = jnp.where(s2_ref[h, :, ls] >= cut_ref[h, k:k + 1, ls], e2_ref[h, :, ls], 0.0)
                w = w + picked * e1_ref[h, k:k + 1, ls]
            p_ref[slot, rows, ls] = (w * jax.nn.gelu(at[:, ls])).astype(BF16)

    acc_ref[...] += jnp.dot(vt_prev_ref[...], p_ref[1 - slot], preferred_element_type=F32)

    @pl.when(j == pl.num_programs(1) - 1)
    def _():
        acc = acc_ref[...] + jnp.dot(vt_ref[...], p_ref[slot], preferred_element_type=F32)
        o_ref[...] = x_ref[...] + acc.T


def peer_main(hn, cut, e1, s2, e2, u, vt, x, tn):
    T, D = x.shape
    e_blk = PEER_I1_STEP * PEER_KEYS
    prev = lambda j: jnp.maximum(j - 1, 0)
    per_i1 = pl.BlockSpec((PEER_HEADS, PEER_I1_STEP, tn), lambda i, j: (0, j, i))
    once = pl.Buffered(1)
    per_i2 = pl.BlockSpec((PEER_HEADS, PEER_KEYS, tn), lambda i, j: (0, 0, i), pipeline_mode=once)
    return pl.pallas_call(
        _peer_main_body,
        grid=(T // tn, PEER_KEYS // PEER_I1_STEP),
        in_specs=[pl.BlockSpec((tn, D), lambda i, j: (i, 0), pipeline_mode=once),
                  per_i1, per_i1, per_i2, per_i2,
                  pl.BlockSpec((e_blk, D), lambda i, j: (j, 0)),
                  pl.BlockSpec((D, e_blk), lambda i, j: (0, prev(j))),
                  pl.BlockSpec((D, e_blk), lambda i, j: (0, j)),
                  pl.BlockSpec((tn, D), lambda i, j: (i, 0), pipeline_mode=once)],
        out_specs=pl.BlockSpec((tn, D), lambda i, j: (i, 0)),
        out_shape=jax.ShapeDtypeStruct((T, D), F32),
        scratch_shapes=[pltpu.VMEM((D, tn), F32), pltpu.VMEM((2, e_blk, tn), BF16)],
        compiler_params=_cparams(("parallel", "arbitrary")),
        name="peer_main",
    )(hn, cut, e1, s2, e2, u, vt, vt, x)


def peer_ffn_update(x, g, wq, subkeys, u, vt):
    hn, cut, e1, s2, e2 = peer_topk(x, g, wq, subkeys, tn=256)
    return peer_main(hn, cut, e1, s2, e2, u, vt, x, tn=1024)


BIAS_TILES = 9
LOG2E = math.log2(math.e)
KEY_TILES = 4
KEY_BLOCK = KEY_TILES * Q_BLOCK
WIN_TILES = WINDOW // Q_BLOCK + 1


def rel_bias_tiles(tbl):
    H = tbl.shape[1]
    d = jnp.arange(BIAS_TILES)[:, None, None]
    s = jnp.arange(Q_BLOCK)[None, :, None]
    t = jnp.arange(Q_BLOCK)[None, None, :]
    bias = tbl[rel_bucket(d * Q_BLOCK + t - s)] * LOG2E
    return jnp.moveaxis(bias, -1, 2).reshape(BIAS_TILES, Q_BLOCK, H * Q_BLOCK)


INT_MIN = -2 ** 31
NEG_INF_KEY = -1900671691


def _sortable_key(x):
    bits = pltpu.bitcast(x, jnp.int32)
    return jnp.where(bits < 0, bits ^ jnp.int32(0x7FFFFFFF), bits)


def _head_rms(x, g, scale):
    return x * lax.rsqrt(jnp.mean(x * x, axis=-1, keepdims=True) + EPS) * g * scale


def _dsa_body(kv_ref, kiw_ref, q_ref, qi_ref, wq_ref, qg_ref, kg_ref, bias_ref, o_ref,
              kn_scr, ki_scr, vt_scr, qall_scr, qiall_scr, sc_scr, th_scr, cnt_scr,
              m_scr, l_scr, acc_scr, *, top_k):
    i = pl.program_id(1)
    dh = HEAD_DIM
    QB = Q_BLOCK

    @pl.when(i == 0)
    def _prep():
        kv = kv_ref[...]
        kn_scr[...] = _head_rms(kv[:, :dh], kg_ref[...], 1.0).astype(BF16)
        vt_scr[...] = kv[:, dh:].T.astype(BF16)
        ki_scr[...] = kiw_ref[:, :dh].astype(BF16)

    q = q_ref[...]
    qi = qi_ref[...]
    for h in range(DSA_HEADS):
        qh = _head_rms(q[:, h * dh:(h + 1) * dh], qg_ref[...], dh ** -0.5 * LOG2E)
        qall_scr[h * QB:(h + 1) * QB, :] = qh.astype(BF16)
        qiall_scr[h * QB:(h + 1) * QB, :] = qi[:, h * dh:(h + 1) * dh].astype(BF16)
    w_t = (wq_ref[:, dh:dh + IDX_HEADS] * (IDX_HEADS ** -0.5)).T

    s_loc = lax.broadcasted_iota(jnp.int32, (QB, QB), 0)
    t_loc = lax.broadcasted_iota(jnp.int32, (QB, QB), 1)
    n_tiles = i + 1
    n_blk = (i + KEY_TILES) // KEY_TILES
    nt_dims = (((1,), (1,)), ((), ()))

    def score_block(kb, carry):
        rows = pl.ds(pl.multiple_of(kb * KEY_BLOCK, KEY_BLOCK), KEY_BLOCK)
        kib = ki_scr[rows, :]
        sc = jnp.zeros((KEY_BLOCK, QB), F32)
        for h in range(IDX_HEADS):
            r = lax.dot_general(kib, qiall_scr[h * QB:(h + 1) * QB, :], nt_dims,
                                preferred_element_type=F32)
            sc = sc + w_t[h:h + 1, :] * jnp.maximum(r, 0.0)
        sc = jnp.where(sc == 0.0, 0.0, sc)
        s_pos = kb * KEY_BLOCK + lax.broadcasted_iota(jnp.int32, (KEY_BLOCK, QB), 0)
        t_pos = i * QB + lax.broadcasted_iota(jnp.int32, (KEY_BLOCK, QB), 1)
        sc_scr[rows, :] = _sortable_key(jnp.where(s_pos <= t_pos, sc, NEG_INF))
        return carry

    lax.fori_loop(0, n_blk, score_block, 0)

    def count_ge(cand):
        def body(kb, c):
            blk = sc_scr[pl.ds(pl.multiple_of(kb * KEY_BLOCK, KEY_BLOCK), KEY_BLOCK), :]
            return c + jnp.sum((blk >= cand).astype(jnp.int32), axis=0, keepdims=True)
        return lax.fori_loop(0, n_blk, body, jnp.zeros((1, QB), jnp.int32))

    th_scr[...] = jnp.full((1, QB), NEG_INF_KEY + 1, jnp.int32)

    @pl.when(n_tiles * QB > top_k)
    def _select():
        def bit_body(it, carry):
            ans_u, cnt_ans = carry
            cand_u = ans_u | (jnp.int32(1) << (31 - it))
            c = count_ge(cand_u ^ jnp.int32(INT_MIN))
            ok = c >= top_k
            return jnp.where(ok, cand_u, ans_u), jnp.where(ok, c, cnt_ans)

        ans_u, cnt = lax.fori_loop(
            0, 32, bit_body,
            (jnp.zeros((1, QB), jnp.int32), jnp.full((1, QB), 1, jnp.int32) * (n_blk * KEY_BLOCK)))
        th_scr[...] = ans_u ^ jnp.int32(INT_MIN)
        cnt_scr[...] = cnt

        @pl.when(jnp.max(cnt) > top_k)
        def _ties():
            theta = th_scr[...]
            need = (top_k - count_ge(theta + 1)).astype(F32)
            tri = (s_loc >= t_loc).astype(BF16)

            def body(kt, seen):
                rows = pl.ds(pl.multiple_of(kt * QB, QB), QB)
                tile = sc_scr[rows, :]
                eq = tile == theta
                rank = seen + jnp.dot(tri, eq.astype(BF16), preferred_element_type=F32)
                sc_scr[rows, :] = jnp.where(eq & (rank > need), theta - 1, tile)
                return seen + jnp.sum(eq.astype(F32), axis=0, keepdims=True)

            lax.fori_loop(0, n_tiles, body, jnp.zeros((1, QB), F32))

    theta = th_scr[...]
    m_scr[...] = jnp.full(m_scr.shape, NEG_INF, F32)
    l_scr[...] = jnp.zeros(l_scr.shape, F32)
    acc_scr[...] = jnp.zeros(acc_scr.shape, F32)

    def att_block(kb, carry):
        row0 = pl.multiple_of(kb * KEY_BLOCK, KEY_BLOCK)
        rows = pl.ds(row0, KEY_BLOCK)
        masks = [sc_scr[pl.ds(row0 + j * QB, QB), :] >= theta for j in range(KEY_TILES)]
        _attend_block(kn_scr[rows, :], vt_scr[:, rows], qall_scr, bias_ref, i - kb * KEY_TILES,
                      masks, m_scr, l_scr, acc_scr)
        return carry

    lax.fori_loop(0, n_blk, att_block, 0)

    o_t = acc_scr[...] / jnp.maximum(l_scr[...], 1e-30)
    o_ref[...] = jnp.concatenate([o_t[:, h * QB:(h + 1) * QB].T for h in range(DSA_HEADS)], axis=1)


def dsa_attention(proj, col, q_gain, k_gain, bias, B, S):
    nq = S // Q_BLOCK
    top_k = min(DSA_TOPK_MAX, S // 4)
    HQ = DSA_HEADS * Q_BLOCK
    const = lambda *shape: pl.BlockSpec(shape, lambda b, i: (0,) * len(shape), pipeline_mode=pl.Buffered(1))
    return pl.pallas_call(
        functools.partial(_dsa_body, top_k=top_k),
        grid=(B, nq),
        in_specs=[pl.BlockSpec((S, 128), lambda b, i: (b, col["d_kv"] // 128)),
                  pl.BlockSpec((S, 128), lambda b, i: (b, col["d_kiw"] // 128)),
                  pl.BlockSpec((Q_BLOCK, MIX_W), lambda b, i: (b * nq + i, col["d_q"] // MIX_W)),
                  pl.BlockSpec((Q_BLOCK, MIX_W), lambda b, i: (b * nq + i, col["d_qi"] // MIX_W)),
                  pl.BlockSpec((Q_BLOCK, 128), lambda b, i: (b * nq + i, col["d_kiw"] // 128)),
                  const(1, HEAD_DIM), const(1, HEAD_DIM), const(BIAS_TILES, Q_BLOCK, HQ)],
        out_specs=pl.BlockSpec((Q_BLOCK, MIX_W), lambda b, i: (b * nq + i, 0)),
        out_shape=jax.ShapeDtypeStruct((B * S, MIX_W), F32),
        scratch_shapes=[pltpu.VMEM((S, HEAD_DIM), BF16), pltpu.VMEM((S, HEAD_DIM), BF16),
                        pltpu.VMEM((HEAD_DIM, S), BF16),
                        pltpu.VMEM((HQ, HEAD_DIM), BF16), pltpu.VMEM((HQ, HEAD_DIM), BF16),
                        pltpu.VMEM((S, Q_BLOCK), jnp.int32),
                        pltpu.VMEM((1, Q_BLOCK), jnp.int32), pltpu.VMEM((1, Q_BLOCK), jnp.int32),
                        pltpu.VMEM((1, HQ), F32), pltpu.VMEM((1, HQ), F32), pltpu.VMEM((HEAD_DIM, HQ), F32)],
        compiler_params=_cparams(("parallel", "arbitrary")),
        name="dsa_attention",
    )(proj, proj, proj, proj, proj, q_gain, k_gain, bias)


NSA_G = NSA_HEADS // NSA_KV_HEADS
CMP_PAD = 56
CMP_NEAR = 64
SEL_TAKEN = -3e38


def _cmp_rows(S):
    return -(-(CMP_PAD + S // CMP_STRIDE) // 64) * 64


def _nsa_prep_body(ks_ref, vs_ref, kw_ref, vw_ref, kg_ref, kso_ref, vso_ref, kwo_ref, vwo_ref):
    dh = HEAD_DIM
    for g in range(NSA_KV_HEADS):
        cs = slice(g * dh, (g + 1) * dh)
        kso_ref[0, g] = _head_rms(ks_ref[:, cs], kg_ref[1:2, :], 1.0).astype(BF16)
        kwo_ref[0, g] = _head_rms(kw_ref[:, cs], kg_ref[2:3, :], 1.0).astype(BF16)
        vso_ref[0, g] = vs_ref[:, cs].T.astype(BF16)
        vwo_ref[0, g] = vw_ref[:, cs].T.astype(BF16)


def nsa_prep(proj, k_gain, B, S, ts=512):
    nt = S // ts
    c0 = COL["n_kv"] // 128
    col = lambda j: pl.BlockSpec((ts, 128), lambda b, t: (b * nt + t, c0 + j))
    k_out = pl.BlockSpec((1, NSA_KV_HEADS, ts, HEAD_DIM), lambda b, t: (b, 0, t, 0))
    v_out = pl.BlockSpec((1, NSA_KV_HEADS, HEAD_DIM, ts), lambda b, t: (b, 0, 0, t))
    k_sds = jax.ShapeDtypeStruct((B, NSA_KV_HEADS, S, HEAD_DIM), BF16)
    v_sds = jax.ShapeDtypeStruct((B, NSA_KV_HEADS, HEAD_DIM, S), BF16)
    return pl.pallas_call(
        _nsa_prep_body,
        grid=(B, nt),
        in_specs=[col(2), col(3), col(4), col(5), pl.BlockSpec((3, HEAD_DIM), lambda b, t: (0, 0))],
        out_specs=[k_out, v_out, k_out, v_out],
        out_shape=[k_sds, v_sds, k_sds, v_sds],
        compiler_params=_cparams(("parallel", "parallel")),
        name="nsa_prep",
    )(proj, proj, proj, proj, k_gain)


def _nsa_compress_body(xk_ref, xv_ref, pe_ref, w1_ref, w2_ref, kg_ref, kc_ref, vct_ref):
    R = xk_ref.shape[3]
    ncp = kc_ref.shape[2]
    half = CMP_STRIDE * HEAD_DIM
    row = lax.broadcasted_iota(jnp.int32, (R, HEAD_DIM), 0)
    for j, x_ref in enumerate((xk_ref, xv_ref)):
        x = x_ref[0, 0, 0]
        pe = pe_ref[j]
        lo = jnp.dot((x + pe[:, :half]).astype(BF16), w1_ref[j, :half, :], preferred_element_type=F32)
        hi = jnp.dot((x + pe[:, half:]).astype(BF16), w1_ref[j, half:, :], preferred_element_type=F32)
        hid = jax.nn.gelu(lo + pltpu.roll(hi, R - 1, 0))
        c = jnp.dot(hid.astype(BF16), w2_ref[j], preferred_element_type=F32)
        if j == 0:
            c = _head_rms(c, kg_ref[0:1, :], 1.0)
        c = jnp.where(row < R - 1, c, 0.0)
        c = jnp.concatenate([jnp.zeros((CMP_PAD, HEAD_DIM), F32), c,
                             jnp.zeros((ncp - CMP_PAD - R, HEAD_DIM), F32)], axis=0)
        if j == 0:
            kc_ref[0, 0] = c.astype(BF16)
        else:
            vct_ref[0, 0] = c.T.astype(BF16)


def nsa_compress(xc, pe, w1, w2, k_gain, S):
    B = xc.shape[0]
    R = S // CMP_STRIDE
    ncp = _cmp_rows(S)
    const = lambda *shape: pl.BlockSpec(shape, lambda b, g: (0,) * len(shape), pipeline_mode=pl.Buffered(1))
    return pl.pallas_call(
        _nsa_compress_body,
        grid=(B, NSA_KV_HEADS),
        in_specs=[pl.BlockSpec((1, 1, 1, R, CMP_STRIDE * HEAD_DIM), lambda b, g: (b, 0, g, 0, 0)),
                  pl.BlockSpec((1, 1, 1, R, CMP_STRIDE * HEAD_DIM), lambda b, g: (b, 1, g, 0, 0)),
                  const(2, 1, CMP_BLOCK * HEAD_DIM), const(2, CMP_BLOCK * HEAD_DIM, CMP_HIDDEN),
                  const(2, CMP_HIDDEN, HEAD_DIM), const(3, HEAD_DIM)],
        out_specs=[pl.BlockSpec((1, 1, ncp, HEAD_DIM), lambda b, g: (b, g, 0, 0)),
                   pl.BlockSpec((1, 1, HEAD_DIM, ncp), lambda b, g: (b, g, 0, 0))],
        out_shape=[jax.ShapeDtypeStruct((B, NSA_KV_HEADS, ncp, HEAD_DIM), BF16),
                   jax.ShapeDtypeStruct((B, NSA_KV_HEADS, HEAD_DIM, ncp), BF16)],
        compiler_params=_cparams(("parallel", "parallel")),
        name="nsa_compress",
    )(xc, xc, pe, w1, w2, k_gain)


def _attend_block(k_b, v_b, q_scr, bias_ref, d0, masks, m_scr, l_scr, acc_scr):
    QB = Q_BLOCK
    CH = 2 * QB
    n_sub = len(masks)
    masks2 = [jnp.concatenate([mk, mk], axis=1) for mk in masks]
    d = [jnp.clip(d0 - j, 0, BIAS_TILES - 1) for j in range(n_sub)]
    chunks = [slice(c * CH, (c + 1) * CH) for c in range(q_scr.shape[0] // CH)]
    logits = [lax.dot_general(k_b, q_scr[ls, :], (((1,), (1,)), ((), ())), preferred_element_type=F32)
              for ls in chunks]
    for ls, lg in zip(chunks, logits):
        parts = [jnp.where(masks2[j], lg[j * QB:(j + 1) * QB] + bias_ref[d[j], :, ls], NEG_INF)
                 for j in range(n_sub)]
        m_old = m_scr[:, ls]
        m_new = m_old
        for part in parts:
            m_new = jnp.maximum(m_new, jnp.max(part, axis=0, keepdims=True))
        probs = [jnp.exp2(part - m_new) for part in parts]
        alpha = jnp.exp2(m_old - m_new)
        l_new = alpha * l_scr[:, ls]
        for pr in probs:
            l_new = l_new + jnp.sum(pr, axis=0, keepdims=True)
        l_scr[:, ls] = l_new
        pb = jnp.concatenate([pr.astype(BF16) for pr in probs], axis=0)
        acc_scr[:, ls] = alpha * acc_scr[:, ls] + jnp.dot(v_b, pb, preferred_element_type=F32)
        m_scr[:, ls] = m_new


def _nsa_body(q_ref, gl_ref, kc_ref, vct_ref, ks_ref, vst_ref, kw_ref, vwt_ref, qg_ref, bias_ref,
              gcd_ref, c31_ref, ovl_ref, o_ref,
              qall_scr, lc_scr, sel_scr, gt_scr, ms_scr, ls_scr, as_scr, mw_scr, lw_scr, aw_scr,
              *, n_sel, top_n):
    g = pl.program_id(1)
    i = pl.program_id(2)
    dh = HEAD_DIM
    QB = Q_BLOCK
    GQ = NSA_G * QB
    ncp = kc_ref.shape[2]

    q = q_ref[...]
    for r in range(NSA_G):
        qall_scr[r * QB:(r + 1) * QB, :] = _head_rms(q[:, r * dh:(r + 1) * dh], qg_ref[...],
                                                     dh ** -0.5 * LOG2E).astype(BF16)
    gt_scr[...] = jax.nn.sigmoid(gl_ref[...]).T

    lc_scr[...] = lax.dot_general(kc_ref[0, 0], qall_scr[...], (((1,), (1,)), ((), ())),
                                  preferred_element_type=F32) + c31_ref[0]
    near = pl.ds(pl.multiple_of(i * (QB // CMP_STRIDE), 8), CMP_NEAR)
    lc_scr[near, :] = lc_scr[near, :] + gcd_ref[0]
    n_p = lax.broadcasted_iota(jnp.int32, (ncp, GQ), 0)
    t_c = i * QB + (lax.broadcasted_iota(jnp.int32, (ncp, GQ), 1) & (QB - 1))
    cmp_end = jnp.where(n_p >= CMP_PAD, (n_p - CMP_PAD) * CMP_STRIDE + (CMP_BLOCK - 1), 1 << 30)
    mask_c = cmp_end <= t_c
    lc = jnp.where(mask_c, lc_scr[...], NEG_INF)
    e = jnp.where(mask_c, jnp.exp2(lc - jnp.max(lc, axis=0, keepdims=True)), 0.0)
    pc = (e / jnp.maximum(jnp.sum(e, axis=0, keepdims=True), 1e-30)).astype(BF16)
    o_c = jnp.dot(vct_ref[0, 0], pc, preferred_element_type=F32)

    imp = jnp.zeros((n_sel, QB), F32)
    for r in range(NSA_G):
        imp = imp + jnp.dot(ovl_ref[...], pc[:, r * QB:(r + 1) * QB], preferred_element_type=F32)
    j_io = lax.broadcasted_iota(jnp.int32, (n_sel, QB), 0).astype(F32)
    cur = ((i * QB + lax.broadcasted_iota(jnp.int32, (n_sel, QB), 1)) >> 6).astype(F32)
    imp = jnp.where((j_io == 0) | (j_io == cur), FORCED_SCORE, imp)
    imp = jnp.where(j_io <= cur, imp, NEG_INF)
    sel = jnp.zeros((n_sel, QB), F32)
    for _ in range(top_n):
        mx = jnp.max(imp, axis=0, keepdims=True)
        first = jnp.min(jnp.where(imp == mx, j_io, float(n_sel)), axis=0, keepdims=True)
        hit = j_io == first
        sel = jnp.where(hit, 1.0, sel)
        imp = jnp.where(hit, SEL_TAKEN, imp)
    sel_scr[...] = sel

    for m_scr, l_scr, a_scr in ((ms_scr, ls_scr, as_scr), (mw_scr, lw_scr, aw_scr)):
        m_scr[...] = jnp.full(m_scr.shape, NEG_INF, F32)
        l_scr[...] = jnp.zeros(l_scr.shape, F32)
        a_scr[...] = jnp.zeros(a_scr.shape, F32)
    s_loc = lax.broadcasted_iota(jnp.int32, (QB, QB), 0)
    t_loc = lax.broadcasted_iota(jnp.int32, (QB, QB), 1)
    blocks_per_tile = QB // SEL_BLOCK

    def slc_block(kb, carry):
        rows = pl.ds(pl.multiple_of(kb * KEY_BLOCK, KEY_BLOCK), KEY_BLOCK)
        masks = []
        for j in range(KEY_TILES):
            kt = kb * KEY_TILES + j
            picked = jnp.where(s_loc < SEL_BLOCK, sel_scr[pl.ds(kt * blocks_per_tile, 1), :],
                               sel_scr[pl.ds(kt * blocks_per_tile + 1, 1), :])
            dist = (t_loc + i * QB) - (s_loc + kt * QB)
            masks.append(jnp.where(dist >= 0, picked, 0.0) > 0.5)
        _attend_block(ks_ref[0, 0, rows, :], vst_ref[0, 0, :, rows], qall_scr, bias_ref,
                      i - kb * KEY_TILES, masks, ms_scr, ls_scr, as_scr)
        return carry

    lax.fori_loop(0, (i + KEY_TILES) // KEY_TILES, slc_block, 0)

    kt0 = jnp.maximum(i - WINDOW // QB, 0)
    rows = pl.ds(pl.multiple_of(kt0 * QB, QB), WIN_TILES * QB)
    masks = []
    for j in range(WIN_TILES):
        dist = (t_loc + i * QB) - (s_loc + (kt0 + j) * QB)
        masks.append((dist >= 0) & (dist < WINDOW))
    _attend_block(kw_ref[0, 0, rows, :], vwt_ref[0, 0, :, rows], qall_scr, bias_ref, i - kt0, masks,
                  mw_scr, lw_scr, aw_scr)

    o_s = as_scr[...] / jnp.maximum(ls_scr[...], 1e-30)
    o_w = aw_scr[...] / jnp.maximum(lw_scr[...], 1e-30)
    outs = []
    for r in range(NSA_G):
        ls = slice(r * QB, (r + 1) * QB)
        gate = lambda j: gt_scr[pl.ds(j * NSA_HEADS + g * NSA_G + r, 1), :]
        o_r = gate(0) * o_c[:, ls] + gate(1) * o_s[:, ls] + gate(2) * o_w[:, ls]
        outs.append(o_r.T)
    o_ref[...] = jnp.concatenate(outs, axis=1)


def nsa_attention(proj, kc, vct, ks, vst, kw, vwt, q_gain, bias, gcd, c31, ovl, B, S):
    nq = S // Q_BLOCK
    GQ = NSA_G * Q_BLOCK
    GW = NSA_G * HEAD_DIM
    ncp = kc.shape[2]
    n_sel = S // SEL_BLOCK
    const = lambda *shape: pl.BlockSpec(shape, lambda b, g, i: (0,) * len(shape), pipeline_mode=pl.Buffered(1))
    per_group = lambda *shape: pl.BlockSpec((1, 1) + shape, lambda b, g, i: (b, g, 0, 0))
    return pl.pallas_call(
        functools.partial(_nsa_body, n_sel=n_sel, top_n=min(SEL_TOPN, n_sel)),
        grid=(B, NSA_KV_HEADS, nq),
        in_specs=[pl.BlockSpec((Q_BLOCK, GW), lambda b, g, i: (b * nq + i, COL["n_q"] // GW + g)),
                  pl.BlockSpec((Q_BLOCK, 128), lambda b, g, i: (b * nq + i, COL["n_g"] // 128)),
                  per_group(ncp, HEAD_DIM), per_group(HEAD_DIM, ncp),
                  per_group(S, HEAD_DIM), per_group(HEAD_DIM, S),
                  per_group(S, HEAD_DIM), per_group(HEAD_DIM, S),
                  const(1, HEAD_DIM),
                  pl.BlockSpec((BIAS_TILES, Q_BLOCK, GQ), lambda b, g, i: (0, 0, g)),
                  pl.BlockSpec((1, CMP_NEAR, GQ), lambda b, g, i: (g, 0, 0)),
                  pl.BlockSpec((1, 1, GQ), lambda b, g, i: (g, 0, 0)),
                  const(n_sel, ncp)],
        out_specs=pl.BlockSpec((Q_BLOCK, GW), lambda b, g, i: (b * nq + i, g)),
        out_shape=jax.ShapeDtypeStruct((B * S, MIX_W), F32),
        scratch_shapes=[pltpu.VMEM((GQ, HEAD_DIM), BF16), pltpu.VMEM((ncp, GQ), F32),
                        pltpu.VMEM((n_sel, Q_BLOCK), F32), pltpu.VMEM((128, Q_BLOCK), F32)]
                       + [pltpu.VMEM((1, GQ), F32), pltpu.VMEM((1, GQ), F32), pltpu.VMEM((HEAD_DIM, GQ), F32)] * 2,
        compiler_params=_cparams(("parallel", "parallel", "arbitrary")),
        name="nsa_attention",
    )(proj, proj, kc, vct, ks, vst, kw, vwt, q_gain, bias, gcd, c31, ovl)


def nsa_tables(tbl, S):
    ncp = _cmp_rows(S)
    n_rel = jnp.arange(CMP_NEAR)[:, None] - CMP_PAD
    t_loc = jnp.arange(Q_BLOCK)[None, :]
    near = tbl[rel_bucket(t_loc - CMP_STRIDE * n_rel - (CMP_BLOCK - 1))] * LOG2E
    far = tbl[N_BUCKETS - 1] * LOG2E
    gcd = jnp.moveaxis(near - far, -1, 0).reshape(NSA_KV_HEADS, NSA_G, CMP_NEAR, Q_BLOCK)
    gcd = jnp.moveaxis(gcd, 1, 2).reshape(NSA_KV_HEADS, CMP_NEAR, NSA_G * Q_BLOCK)
    c31 = jnp.repeat(far, Q_BLOCK).reshape(NSA_KV_HEADS, 1, NSA_G * Q_BLOCK)
    n = np.arange(ncp) - CMP_PAD
    sel_start = np.arange(S // SEL_BLOCK) * SEL_BLOCK
    start = n * CMP_STRIDE
    ovl = ((start[None, :] < sel_start[:, None] + SEL_BLOCK) & (start[None, :] + CMP_BLOCK > sel_start[:, None])
           & (n[None, :] >= 0) & (n[None, :] < (S - CMP_BLOCK) // CMP_STRIDE + 1))
    return gcd, c31, jnp.asarray(ovl, BF16)


def nsa_mixer_pallas(proj, cmp_pe, cmp_w1, cmp_w2, q_gain, k_gain, rel_tbl, bias, B, S):
    dh = HEAD_DIM
    c0 = COL["n_kv"]
    xc = proj[:, c0:c0 + 2 * NSA_KV_HEADS * dh].reshape(B, S // CMP_STRIDE, CMP_STRIDE, 2, NSA_KV_HEADS, dh)
    xc = xc.transpose(0, 3, 4, 1, 2, 5).reshape(B, 2, NSA_KV_HEADS, S // CMP_STRIDE, CMP_STRIDE * dh)
    kc, vct = nsa_compress(xc, cmp_pe.reshape(2, 1, CMP_BLOCK * dh), cmp_w1.astype(BF16),
                           cmp_w2.astype(BF16), k_gain, S)
    ks, vst, kw, vwt = nsa_prep(proj, k_gain, B, S)
    gcd, c31, ovl = nsa_tables(rel_tbl, S)
    return nsa_attention(proj, kc, vct, ks, vst, kw, vwt, q_gain[None], bias, gcd, c31, ovl, B, S)


def rel_bucket(dist):
    n = jnp.maximum(dist, 0)
    max_exact = N_BUCKETS // 2
    nf = jnp.maximum(n, 1).astype(jnp.float32)
    large = max_exact + (jnp.log(nf / max_exact) / math.log(MAX_DISTANCE / max_exact)
                         * (N_BUCKETS - max_exact)).astype(jnp.int32)
    return jnp.where(n < max_exact, n, jnp.minimum(large, N_BUCKETS - 1))


CONV_HALO = 8


def _conv_body(b_ref, c_ref, x_ref, w_ref, o_ref, prev_scr):
    @pl.when(pl.program_id(1) == 0)
    def _():
        prev_scr[...] = jnp.zeros(prev_scr.shape, F32)

    bx = c_ref[...] * x_ref[...]
    ts = bx.shape[0]
    row = lax.broadcasted_iota(jnp.int32, bx.shape, 0)
    prev = prev_scr[...]
    last1 = prev[CONV_HALO - 1:CONV_HALO]
    last2 = prev[CONV_HALO - 2:CONV_HALO - 1]
    back1 = jnp.where(row == 0, last1, pltpu.roll(bx, 1, 0))
    back2 = jnp.where(row == 0, last2, jnp.where(row == 1, last1, pltpu.roll(bx, 2, 0)))
    w = w_ref[...]
    o_ref[...] = b_ref[...] * (w[0:1] * back2 + w[1:2] * back1 + w[2:3] * bx)
    prev_scr[...] = bx[ts - CONV_HALO:ts]


def conv_mixer(proj, conv_w, B, S, ts=512):
    nt = S // ts
    col = lambda name: pl.BlockSpec((ts, MIX_W), lambda b, t: (b * nt + t, COL[name] // MIX_W))
    return pl.pallas_call(
        _conv_body,
        grid=(B, nt),
        in_specs=[col("a_b"), col("a_c"), col("a_x"), pl.BlockSpec((CONV_K, MIX_W), lambda b, t: (0, 0))],
        out_specs=pl.BlockSpec((ts, MIX_W), lambda b, t: (b * nt + t, 0)),
        out_shape=jax.ShapeDtypeStruct((B * S, MIX_W), F32),
        scratch_shapes=[pltpu.VMEM((CONV_HALO, MIX_W), F32)],
        compiler_params=_cparams(("parallel", "arbitrary")),
        name="conv_mixer",
    )(proj, proj, proj, conv_w)


def _sgu_body(uv_ref, g_ref, w_ref, b_ref, o_ref):
    uv = jax.nn.gelu(uv_ref[...])
    u = uv[:, :MIX_W]
    v = uv[:, MIX_W:]
    v = v - jnp.mean(v, axis=-1, keepdims=True)
    v = (v * lax.rsqrt(jnp.mean(v * v, axis=-1, keepdims=True) + EPS) * g_ref[...]).astype(BF16)
    gw = MIX_W // SGU_GROUPS
    group = lax.broadcasted_iota(jnp.int32, (SGU_CHUNK, MIX_W), 1) >> (gw.bit_length() - 1)
    for c in range(uv.shape[0] // SGU_CHUNK):
        rows = slice(c * SGU_CHUNK, (c + 1) * SGU_CHUNK)
        vc = v[rows]
        s = b_ref[...]
        for g in range(SGU_GROUPS):
            s = s + jnp.dot(w_ref[g], jnp.where(group == g, vc, jnp.zeros_like(vc)),
                            preferred_element_type=F32)
        o_ref[rows, :] = u[rows] * s


def sgu_mixer_pallas(proj, ln_gain, w_s, b_s, T, tm=512):
    tri = jnp.tril(jnp.ones((SGU_CHUNK, SGU_CHUNK), dtype=bool))
    w = jnp.where(tri[None], w_s, 0).astype(BF16)
    b = jnp.repeat(b_s.T, MIX_W // SGU_GROUPS, axis=1)
    const = lambda *shape: pl.BlockSpec(shape, lambda i: (0,) * len(shape), pipeline_mode=pl.Buffered(1))
    return pl.pallas_call(
        _sgu_body,
        grid=(T // tm,),
        in_specs=[pl.BlockSpec((tm, 2 * MIX_W), lambda i: (i, COL["c_uv"] // (2 * MIX_W))),
                  const(1, MIX_W), const(SGU_GROUPS, SGU_CHUNK, SGU_CHUNK), const(SGU_CHUNK, MIX_W)],
        out_specs=pl.BlockSpec((tm, MIX_W), lambda i: (i, 0)),
        out_shape=jax.ShapeDtypeStruct((T, MIX_W), F32),
        compiler_params=_cparams(("parallel",)),
        name="sgu_mixer",
    )(proj, ln_gain[None], w, b)


def _layer(x2, p2, B, S, rel_bias, g_mix, w_in, conv_w, nsa_cmp_pe, nsa_cmp_w1, nsa_cmp_w2,
           nsa_q_gain, nsa_k_gain, sgu_ln_gain, sgu_w, sgu_b, dsa_q_gain, dsa_k_gain,
           w_gate, w_branch, w_out, g_ffn, peer_wq, peer_subkeys, peer_u, peer_v,
           g_ple, w_ple_gate, w_ple_proj):
    T = B * S
    proj = rms_matmul(x2, g_mix[None], pack_w_in(w_in), tm=1024, tn=768)

    y_a = conv_mixer(proj, conv_w, B, S)
    y_b = nsa_mixer_pallas(proj, nsa_cmp_pe, nsa_cmp_w1, nsa_cmp_w2, nsa_q_gain, nsa_k_gain,
                           rel_bias[:, :NSA_HEADS], rel_bias_tiles(rel_bias[:, :NSA_HEADS]), B, S)
    y_c = sgu_mixer_pallas(proj, sgu_ln_gain, sgu_w, sgu_b, T)
    y_d = dsa_attention(proj, COL, dsa_q_gain[None], dsa_k_gain[None],
                        rel_bias_tiles(rel_bias[:, NSA_HEADS:]), B, S)
    ys = [y_a, y_b, y_c, y_d]

    x2 = merge_mixers(x2, g_mix[None], ys, w_gate.astype(BF16),
                      w_branch.reshape(N_MIXERS, MIX_W, D_MODEL).astype(BF16),
                      w_out.astype(BF16), tm=256)

    sk = peer_subkeys.reshape(2 * PEER_HEADS, PEER_KEYS, PEER_QDIM // 2).astype(BF16)
    x2 = peer_ffn_update(x2, g_ffn[None], peer_wq.astype(BF16), sk,
                         peer_u.astype(BF16), peer_v.astype(BF16).T)

    x2 = ple_update(x2, g_ple[None], p2, w_ple_gate.astype(BF16), w_ple_proj.astype(BF16), tm=512)
    return x2


def kernel(x, p, rel_bias, g_mix, w_in, conv_w, nsa_cmp_pe, nsa_cmp_w1, nsa_cmp_w2, nsa_q_gain,
           nsa_k_gain, sgu_ln_gain, sgu_w, sgu_b, dsa_q_gain, dsa_k_gain, w_gate, w_branch, w_out,
           g_ffn, peer_wq, peer_subkeys, peer_u, peer_v, g_ple, w_ple_gate, w_ple_proj):
    B, S, D = x.shape
    depth = p.shape[0]
    x2 = x.reshape(B * S, D)
    for l in range(depth):
        x2 = _layer(x2, p[l].reshape(B * S, PLE_DIM), B, S, rel_bias, g_mix[l], w_in[l], conv_w[l],
                    nsa_cmp_pe[l], nsa_cmp_w1[l], nsa_cmp_w2[l], nsa_q_gain[l], nsa_k_gain[l],
                    sgu_ln_gain[l], sgu_w[l], sgu_b[l], dsa_q_gain[l], dsa_k_gain[l],
                    w_gate[l], w_branch[l], w_out[l], g_ffn[l], peer_wq[l], peer_subkeys[l],
                    peer_u[l], peer_v[l], g_ple[l], w_ple_gate[l], w_ple_proj[l])
    return x2.reshape(B, S, D)
```

```python
import functools
import math

import jax
import jax.numpy as jnp
import numpy as np
from jax import lax
from jax.experimental import pallas as pl
from jax.experimental.pallas import tpu as pltpu

F32 = jnp.float32
BF16 = jnp.bfloat16

D_MODEL = 1024
HEAD_DIM = 64
N_MIXERS = 4
MIX_W = D_MODEL // 2
Q_BLOCK = 128
EPS = 1e-6
NEG_INF = -1e30
CONV_K = 3
NSA_HEADS = MIX_W // HEAD_DIM
NSA_KV_HEADS = 2
CMP_BLOCK = 32
CMP_STRIDE = 16
CMP_HIDDEN = 256
SEL_BLOCK = 64
SEL_TOPN = 8
WINDOW = 512
FORCED_SCORE = 1e4
SGU_CHUNK = 128
SGU_GROUPS = 8
DSA_HEADS = MIX_W // HEAD_DIM
IDX_HEADS = 8
IDX_DIM = 64
DSA_TOPK_MAX = 256
N_BUCKETS = 32
MAX_DISTANCE = 1024
PEER_HEADS = 8
PEER_KEYS = 128
PEER_QDIM = 128
PEER_TOPK = 16
N_EXPERTS = PEER_KEYS * PEER_KEYS
PLE_DIM = 256

SPLIT_WIDTHS = (
    MIX_W, MIX_W, MIX_W,
    NSA_HEADS * HEAD_DIM,
    6 * NSA_KV_HEADS * HEAD_DIM,
    3 * NSA_HEADS,
    2 * MIX_W,
    DSA_HEADS * HEAD_DIM, HEAD_DIM, HEAD_DIM,
    IDX_HEADS * IDX_DIM, IDX_DIM, IDX_HEADS,
)
IN_WIDTH = sum(SPLIT_WIDTHS)

COL = dict(a_b=0, a_c=512, a_x=1024, n_q=1536, d_q=2048, d_qi=2560, c_uv=3072, n_kv=4096,
           n_g=4864, d_kv=4992, d_kiw=5120)
N_PACK = 5376


def pack_w_in(w):
    o = dict(zip(("a_b", "a_c", "a_x", "n_q", "n_kv", "n_g", "c_uv", "d_q", "d_k", "d_v", "d_qi", "d_ki",
                  "d_wi", "end"), [0] + [int(c) for c in np.cumsum(SPLIT_WIDTHS)]))
    z = lambda n: jnp.zeros((w.shape[0], n), w.dtype)
    cols = [w[:, o["a_b"]:o["n_kv"]], w[:, o["d_q"]:o["d_k"]], w[:, o["d_qi"]:o["d_ki"]],
            w[:, o["c_uv"]:o["d_q"]], w[:, o["n_kv"]:o["n_g"]],
            w[:, o["n_g"]:o["c_uv"]], z(128 - 3 * NSA_HEADS),
            w[:, o["d_k"]:o["d_qi"]],
            w[:, o["d_ki"]:o["end"]], z(128 - IDX_DIM - IDX_HEADS)]
    packed = jnp.concatenate(cols, axis=1)
    return jnp.pad(packed, ((0, 0), (0, N_PACK - packed.shape[1]))).astype(BF16)

VMEM_LIMIT_BYTES = 56 * 1024 * 1024


def _cparams(sem):
    return pltpu.CompilerParams(dimension_semantics=sem, vmem_limit_bytes=VMEM_LIMIT_BYTES)


def _rms(x, g):
    return x * lax.rsqrt(jnp.mean(x * x, axis=-1, keepdims=True) + EPS) * g


def _rms_matmul_body(x_ref, g_ref, w_ref, o_ref, h_ref):
    @pl.when(pl.program_id(1) == 0)
    def _():
        h_ref[...] = _rms(x_ref[...], g_ref[...]).astype(BF16)

    o_ref[...] = jnp.dot(h_ref[...], w_ref[...], preferred_element_type=F32)


def rms_matmul(x, g, w, tm, tn):
    T, D = x.shape
    N = w.shape[1]
    return pl.pallas_call(
        _rms_matmul_body,
        grid=(T // tm, N // tn),
        in_specs=[pl.BlockSpec((tm, D), lambda i, j: (i, 0)),
                  pl.BlockSpec((1, D), lambda i, j: (0, 0)),
                  pl.BlockSpec((D, tn), lambda i, j: (0, j))],
        out_specs=pl.BlockSpec((tm, tn), lambda i, j: (i, j)),
        out_shape=jax.ShapeDtypeStruct((T, N), F32),
        scratch_shapes=[pltpu.VMEM((tm, D), BF16)],
        compiler_params=_cparams(("parallel", "arbitrary")),
        name="rms_matmul",
    )(x, g, w)


def _merge_body(x_ref, g_ref, ya_ref, yb_ref, yc_ref, yd_ref, wg_ref, wb_ref, wo_ref, o_ref):
    x = x_ref[...]
    h = _rms(x, g_ref[...]).astype(BF16)
    merged = jnp.zeros(x.shape, F32)
    for m, y_ref in enumerate((ya_ref, yb_ref, yc_ref, yd_ref)):
        z = jnp.dot(y_ref[...].astype(BF16), wb_ref[m], preferred_element_type=F32)
        gate = jax.nn.sigmoid(jnp.dot(h, wg_ref[:, m * D_MODEL:(m + 1) * D_MODEL],
                                      preferred_element_type=F32))
        merged = merged + gate * z
    o_ref[...] = x + jnp.dot(merged.astype(BF16), wo_ref[...], preferred_element_type=F32)


def merge_mixers(x, g, ys, w_gate, w_branch, w_out, tm):
    T, D = x.shape
    const = lambda *shape: pl.BlockSpec(shape, lambda i: (0,) * len(shape), pipeline_mode=pl.Buffered(1))
    return pl.pallas_call(
        _merge_body,
        grid=(T // tm,),
        in_specs=[pl.BlockSpec((tm, D), lambda i: (i, 0)),
                  const(1, D)]
                 + [pl.BlockSpec((tm, MIX_W), lambda i: (i, 0)) for _ in range(N_MIXERS)]
                 + [const(D, N_MIXERS * D), const(N_MIXERS, MIX_W, D), const(D, D)],
        out_specs=pl.BlockSpec((tm, D), lambda i: (i, 0)),
        out_shape=jax.ShapeDtypeStruct((T, D), F32),
        compiler_params=_cparams(("parallel",)),
        name="merge_mixers",
    )(x, g, *ys, w_gate, w_branch, w_out)


def _ple_body(x_ref, g_ref, p_ref, wg_ref, wp_ref, o_ref):
    x = x_ref[...]
    h = _rms(x, g_ref[...]).astype(BF16)
    gate = jax.nn.sigmoid(jnp.dot(h, wg_ref[...], preferred_element_type=F32))
    proj = jnp.dot(p_ref[...].astype(BF16), wp_ref[...], preferred_element_type=F32)
    o_ref[...] = x + gate * proj


def ple_update(x, g, p, w_gate, w_proj, tm):
    T, D = x.shape
    const = lambda *shape: pl.BlockSpec(shape, lambda i: (0,) * len(shape), pipeline_mode=pl.Buffered(1))
    return pl.pallas_call(
        _ple_body,
        grid=(T // tm,),
        in_specs=[pl.BlockSpec((tm, D), lambda i: (i, 0)), const(1, D),
                  pl.BlockSpec((tm, PLE_DIM), lambda i: (i, 0)),
                  const(D, D), const(PLE_DIM, D)],
        out_specs=pl.BlockSpec((tm, D), lambda i: (i, 0)),
        out_shape=jax.ShapeDtypeStruct((T, D), F32),
        compiler_params=_cparams(("parallel",)),
        name="ple_update",
    )(x, g, p, w_gate, w_proj)


TAKEN = -3e38
NO_PRIORITY = 1e9


def _extract_top16(cur, prio, tops_ref):
    rank = jnp.full(cur.shape, float(PEER_TOPK), F32)
    for r in range(PEER_TOPK):
        m = jnp.max(cur, axis=0, keepdims=True)
        if tops_ref is not None:
            tops_ref[r:r + 1, :] = m
        first = jnp.min(jnp.where(cur == m, prio, NO_PRIORITY), axis=0, keepdims=True)
        hit = prio == first
        rank = jnp.where(hit, float(r), rank)
        cur = jnp.where(hit, TAKEN, cur)
    return rank


_CAND_ROWS_J = (16, 8, 5, 4, 3, 2, 2, 2)


def _peer_topk_body(x_ref, g_ref, wq_ref, sk_ref, hn_ref, cnt_ref, e1_ref, rank_ref, e2_ref,
                    t1_ref, t2_ref, j_ref):
    hn = _rms(x_ref[...], g_ref[...]).astype(BF16)
    hn_ref[...] = hn
    q = jnp.dot(hn, wq_ref[...], preferred_element_type=F32).astype(BF16)
    half = PEER_QDIM // 2
    tn = x_ref.shape[0]
    key_prio = lax.broadcasted_iota(jnp.int32, (PEER_KEYS, LANE_CHUNK), 0).astype(F32)
    j8 = lax.broadcasted_iota(jnp.int32, (8, LANE_CHUNK), 0)
    j16 = lax.broadcasted_iota(jnp.int32, (PEER_TOPK, LANE_CHUNK), 0)
    cand_prio = jnp.concatenate([j16] + [j8 + PEER_TOPK * i for i in range(1, 8)]
                                + [(j8 + 8) * PEER_TOPK], axis=0).astype(F32)
    for h in range(PEER_HEADS):
        scores = []
        for p in range(2):
            c0 = (2 * h + p) * half
            scores.append(lax.dot_general(sk_ref[2 * h + p], q[:, c0:c0 + half],
                                          (((1,), (1,)), ((), ())), preferred_element_type=F32))
        for c in range(tn // LANE_CHUNK):
            ls = slice(c * LANE_CHUNK, (c + 1) * LANE_CHUNK)
            s1 = scores[0][:, ls]
            s2 = scores[1][:, ls]
            rank1 = _extract_top16(s1, key_prio, t1_ref)
            rank2 = _extract_top16(s2, key_prio, t2_ref)
            a = t1_ref[...]
            b = t2_ref[...]
            b8 = b[0:8]
            pieces = [a[0:1] + b]
            for i in range(1, 8):
                cand_i = a[i:i + 1] + b8
                if _CAND_ROWS_J[i] < 8:
                    cand_i = jnp.where(j8 < _CAND_ROWS_J[i], cand_i, NEG_INF)
                pieces.append(cand_i)
            pieces.append(a[8:16] + b[0:1])
            cand = jnp.concatenate(pieces, axis=0)
            taken = _extract_top16(cand, cand_prio, None) < PEER_TOPK
            top = a[0:1] + b[0:1]
            z = jnp.sum(jnp.where(taken, jnp.exp(cand - top), 0.0), axis=0, keepdims=True)
            one = jnp.where(taken, 1.0, 0.0)
            j_ref[0:1, :] = jnp.sum(one[0:PEER_TOPK], axis=0, keepdims=True)
            for i in range(1, 8):
                j_ref[i:i + 1, :] = jnp.sum(one[8 + 8 * i:16 + 8 * i], axis=0, keepdims=True)
            j_ref[8:PEER_TOPK, :] = one[9 * 8:10 * 8]
            counts = j_ref[...]
            cnt = jnp.zeros(s1.shape, F32)
            for r in range(PEER_TOPK):
                cnt = jnp.where(rank1 == float(r), counts[r:r + 1], cnt)
            cnt_ref[h, :, ls] = cnt
            rank_ref[h, :, ls] = rank2
            e1_ref[h, :, ls] = jnp.where(rank1 < PEER_TOPK, jnp.exp(s1 - (a[0:1] + jnp.log(z))), 0.0)
            e2_ref[h, :, ls] = jnp.where(rank2 < PEER_TOPK, jnp.exp(s2 - b[0:1]), 0.0)


def peer_topk(x, g, wq, subkeys, tn):
    T, D = x.shape
    const = lambda *shape: pl.BlockSpec(shape, lambda i: (0,) * len(shape), pipeline_mode=pl.Buffered(1))
    return pl.pallas_call(
        _peer_topk_body,
        grid=(T // tn,),
        in_specs=[pl.BlockSpec((tn, D), lambda i: (i, 0)), const(1, D), const(D, D),
                  const(2 * PEER_HEADS, PEER_KEYS, PEER_QDIM // 2)],
        out_specs=[pl.BlockSpec((tn, D), lambda i: (i, 0))]
                  + [pl.BlockSpec((PEER_HEADS, PEER_KEYS, tn), lambda i: (0, 0, i))] * 4,
        out_shape=[jax.ShapeDtypeStruct((T, D), BF16)]
                  + [jax.ShapeDtypeStruct((PEER_HEADS, PEER_KEYS, T), F32)] * 4,
        scratch_shapes=[pltpu.VMEM((PEER_TOPK, LANE_CHUNK), F32)] * 3,
        compiler_params=_cparams(("parallel",)),
        name="peer_topk",
    )(x, g, wq, subkeys)


LANE_CHUNK = 128


PEER_I1_STEP = 8


def _peer_main_body(hn_ref, cnt_ref, e1_ref, rank_ref, e2_ref, u_ref, vt_prev_ref, vt_ref, x_ref, o_ref,
                    acc_ref, p_ref):
    j = pl.program_id(1)
    tn = hn_ref.shape[0]
    slot = j % 2

    @pl.when(j == 0)
    def _():
        acc_ref[...] = jnp.zeros(acc_ref.shape, F32)
        p_ref[1] = jnp.zeros(p_ref.shape[1:], BF16)

    hn = hn_ref[...]
    for k in range(PEER_I1_STEP):
        rows = slice(k * PEER_KEYS, (k + 1) * PEER_KEYS)
        at = lax.dot_general(u_ref[rows, :], hn, (((1,), (1,)), ((), ())),
                             preferred_element_type=F32)
        for c in range(tn // LANE_CHUNK):
            ls = slice(c * LANE_CHUNK, (c + 1) * LANE_CHUNK)
            w = jnp.zeros((PEER_KEYS, LANE_CHUNK), F32)
            for h in range(PEER_HEADS):
                picked = jnp.where(rank_ref[h, :, ls] < cnt_ref[h, k:k + 1, ls], e2_ref[h, :, ls], 0.0)
                w = w + picked * e1_ref[h, k:k + 1, ls]
            p_ref[slot, rows, ls] = (w * jax.nn.gelu(at[:, ls])).astype(BF16)

    acc_ref[...] += jnp.dot(vt_prev_ref[...], p_ref[1 - slot], preferred_element_type=F32)

    @pl.when(j == pl.num_programs(1) - 1)
    def _():
        acc = acc_ref[...] + jnp.dot(vt_ref[...], p_ref[slot], preferred_element_type=F32)
        o_ref[...] = x_ref[...] + acc.T


def peer_main(hn, cnt, e1, rank2, e2, u, vt, x, tn):
    T, D = x.shape
    e_blk = PEER_I1_STEP * PEER_KEYS
    prev = lambda j: jnp.maximum(j - 1, 0)
    per_i1 = pl.BlockSpec((PEER_HEADS, PEER_I1_STEP, tn), lambda i, j: (0, j, i))
    once = pl.Buffered(1)
    per_i2 = pl.BlockSpec((PEER_HEADS, PEER_KEYS, tn), lambda i, j: (0, 0, i), pipeline_mode=once)
    return pl.pallas_call(
        _peer_main_body,
        grid=(T // tn, PEER_KEYS // PEER_I1_STEP),
        in_specs=[pl.BlockSpec((tn, D), lambda i, j: (i, 0), pipeline_mode=once),
                  per_i1, per_i1, per_i2, per_i2,
                  pl.BlockSpec((e_blk, D), lambda i, j: (j, 0)),
                  pl.BlockSpec((D, e_blk), lambda i, j: (0, prev(j))),
                  pl.BlockSpec((D, e_blk), lambda i, j: (0, j)),
                  pl.BlockSpec((tn, D), lambda i, j: (i, 0), pipeline_mode=once)],
        out_specs=pl.BlockSpec((tn, D), lambda i, j: (i, 0)),
        out_shape=jax.ShapeDtypeStruct((T, D), F32),
        scratch_shapes=[pltpu.VMEM((D, tn), F32), pltpu.VMEM((2, e_blk, tn), BF16)],
        compiler_params=_cparams(("parallel", "arbitrary")),
        name="peer_main",
    )(hn, cnt, e1, rank2, e2, u, vt, vt, x)


def peer_ffn_update(x, g, wq, subkeys, u, vt):
    hn, cnt, e1, rank2, e2 = peer_topk(x, g, wq, subkeys, tn=256)
    return peer_main(hn, cnt, e1, rank2, e2, u, vt, x, tn=1024)


BIAS_TILES = 9
LOG2E = math.log2(math.e)
KEY_TILES = 4
KEY_BLOCK = KEY_TILES * Q_BLOCK
WIN_TILES = WINDOW // Q_BLOCK + 1


def rel_bias_tiles(tbl):
    H = tbl.shape[1]
    d = jnp.arange(BIAS_TILES)[:, None, None]
    s = jnp.arange(Q_BLOCK)[None, :, None]
    t = jnp.arange(Q_BLOCK)[None, None, :]
    bias = tbl[rel_bucket(d * Q_BLOCK + t - s)] * LOG2E
    return jnp.moveaxis(bias, -1, 2).reshape(BIAS_TILES, Q_BLOCK, H * Q_BLOCK)


INT_MIN = -2 ** 31
NEG_INF_KEY = -1900671691


def _sortable_key(x):
    bits = pltpu.bitcast(x, jnp.int32)
    return jnp.where(bits < 0, bits ^ jnp.int32(0x7FFFFFFF), bits)


def _head_rms(x, g, scale):
    return x * lax.rsqrt(jnp.mean(x * x, axis=-1, keepdims=True) + EPS) * g * scale


def _dsa_body(kv_ref, kiw_ref, q_ref, qi_ref, wq_ref, qg_ref, kg_ref, bias_ref, o_ref,
              kn_scr, ki_scr, vt_scr, qall_scr, qiall_scr, sc_scr, th_scr,
              m_scr, l_scr, acc_scr, *, top_k):
    i = pl.program_id(1)
    dh = HEAD_DIM
    QB = Q_BLOCK

    @pl.when(i == 0)
    def _prep():
        kv = kv_ref[...]
        kn_scr[...] = _head_rms(kv[:, :dh], kg_ref[...], 1.0).astype(BF16)
        vt_scr[...] = kv[:, dh:].T.astype(BF16)
        ki_scr[...] = kiw_ref[:, :dh].astype(BF16)

    q = q_ref[...]
    qi = qi_ref[...]
    for h in range(DSA_HEADS):
        qh = _head_rms(q[:, h * dh:(h + 1) * dh], qg_ref[...], dh ** -0.5 * LOG2E)
        qall_scr[h * QB:(h + 1) * QB, :] = qh.astype(BF16)
        qiall_scr[h * QB:(h + 1) * QB, :] = qi[:, h * dh:(h + 1) * dh].astype(BF16)
    w_t = (wq_ref[:, dh:dh + IDX_HEADS] * (IDX_HEADS ** -0.5)).T

    s_loc = lax.broadcasted_iota(jnp.int32, (QB, QB), 0)
    t_loc = lax.broadcasted_iota(jnp.int32, (QB, QB), 1)
    n_tiles = i + 1
    n_blk = (i + KEY_TILES) // KEY_TILES
    nt_dims = (((1,), (1,)), ((), ()))

    def score_block(kb, carry):
        rows = pl.ds(pl.multiple_of(kb * KEY_BLOCK, KEY_BLOCK), KEY_BLOCK)
        kib = ki_scr[rows, :]
        sc = jnp.zeros((KEY_BLOCK, QB), F32)
        for h in range(IDX_HEADS):
            r = lax.dot_general(kib, qiall_scr[h * QB:(h + 1) * QB, :], nt_dims,
                                preferred_element_type=F32)
            sc = sc + w_t[h:h + 1, :] * jnp.maximum(r, 0.0)
        sc = jnp.where(sc == 0.0, 0.0, sc)
        s_pos = kb * KEY_BLOCK + lax.broadcasted_iota(jnp.int32, (KEY_BLOCK, QB), 0)
        t_pos = i * QB + lax.broadcasted_iota(jnp.int32, (KEY_BLOCK, QB), 1)
        sc_scr[rows, :] = _sortable_key(jnp.where(s_pos <= t_pos, sc, NEG_INF))
        return carry

    lax.fori_loop(0, n_blk, score_block, 0)

    def count_ge(cand):
        def body(kb, c):
            blk = sc_scr[pl.ds(pl.multiple_of(kb * KEY_BLOCK, KEY_BLOCK), KEY_BLOCK), :]
            return c + jnp.sum((blk >= cand).astype(jnp.int32), axis=0, keepdims=True)
        return lax.fori_loop(0, n_blk, body, jnp.zeros((1, QB), jnp.int32))

    th_scr[...] = jnp.full((1, QB), NEG_INF_KEY + 1, jnp.int32)

    @pl.when(n_tiles * QB > top_k)
    def _select():
        def bit_body(it, carry):
            ans_u, cnt_ans = carry
            cand_u = ans_u | (jnp.int32(1) << (31 - it))
            c = count_ge(cand_u ^ jnp.int32(INT_MIN))
            ok = c >= top_k
            return jnp.where(ok, cand_u, ans_u), jnp.where(ok, c, cnt_ans)

        ans_u, cnt = lax.fori_loop(
            0, 32, bit_body,
            (jnp.zeros((1, QB), jnp.int32), jnp.full((1, QB), 1, jnp.int32) * (n_blk * KEY_BLOCK)))
        th_scr[...] = ans_u ^ jnp.int32(INT_MIN)

        @pl.when(jnp.max(cnt) > top_k)
        def _ties():
            theta = th_scr[...]
            need = (top_k - count_ge(theta + 1)).astype(F32)
            tri = (s_loc >= t_loc).astype(BF16)

            def body(kt, seen):
                rows = pl.ds(pl.multiple_of(kt * QB, QB), QB)
                tile = sc_scr[rows, :]
                eq = tile == theta
                rank = seen + jnp.dot(tri, eq.astype(BF16), preferred_element_type=F32)
                sc_scr[rows, :] = jnp.where(eq & (rank > need), theta - 1, tile)
                return seen + jnp.sum(eq.astype(F32), axis=0, keepdims=True)

            lax.fori_loop(0, n_tiles, body, jnp.zeros((1, QB), F32))

    theta = th_scr[...]
    m_scr[...] = jnp.full(m_scr.shape, NEG_INF, F32)
    l_scr[...] = jnp.zeros(l_scr.shape, F32)
    acc_scr[...] = jnp.zeros(acc_scr.shape, F32)

    def att_block(kb, carry):
        row0 = pl.multiple_of(kb * KEY_BLOCK, KEY_BLOCK)
        rows = pl.ds(row0, KEY_BLOCK)
        masks = [sc_scr[pl.ds(row0 + j * QB, QB), :] >= theta for j in range(KEY_TILES)]
        _attend_block(kn_scr[rows, :], vt_scr[:, rows], qall_scr, bias_ref, i - kb * KEY_TILES,
                      masks, m_scr, l_scr, acc_scr)
        return carry

    lax.fori_loop(0, n_blk, att_block, 0)

    o_t = acc_scr[...] / jnp.maximum(l_scr[...], 1e-30)
    o_ref[...] = jnp.concatenate([o_t[:, h * QB:(h + 1) * QB].T for h in range(DSA_HEADS)], axis=1)


def dsa_attention(proj, col, q_gain, k_gain, bias, B, S):
    nq = S // Q_BLOCK
    top_k = min(DSA_TOPK_MAX, S // 4)
    HQ = DSA_HEADS * Q_BLOCK
    const = lambda *shape: pl.BlockSpec(shape, lambda b, i: (0,) * len(shape), pipeline_mode=pl.Buffered(1))
    return pl.pallas_call(
        functools.partial(_dsa_body, top_k=top_k),
        grid=(B, nq),
        in_specs=[pl.BlockSpec((S, 128), lambda b, i: (b, col["d_kv"] // 128)),
                  pl.BlockSpec((S, 128), lambda b, i: (b, col["d_kiw"] // 128)),
                  pl.BlockSpec((Q_BLOCK, MIX_W), lambda b, i: (b * nq + i, col["d_q"] // MIX_W)),
                  pl.BlockSpec((Q_BLOCK, MIX_W), lambda b, i: (b * nq + i, col["d_qi"] // MIX_W)),
                  pl.BlockSpec((Q_BLOCK, 128), lambda b, i: (b * nq + i, col["d_kiw"] // 128)),
                  const(1, HEAD_DIM), const(1, HEAD_DIM), const(BIAS_TILES, Q_BLOCK, HQ)],
        out_specs=pl.BlockSpec((Q_BLOCK, MIX_W), lambda b, i: (b * nq + i, 0)),
        out_shape=jax.ShapeDtypeStruct((B * S, MIX_W), F32),
        scratch_shapes=[pltpu.VMEM((S, HEAD_DIM), BF16), pltpu.VMEM((S, HEAD_DIM), BF16),
                        pltpu.VMEM((HEAD_DIM, S), BF16),
                        pltpu.VMEM((HQ, HEAD_DIM), BF16), pltpu.VMEM((HQ, HEAD_DIM), BF16),
                        pltpu.VMEM((S, Q_BLOCK), jnp.int32),
                        pltpu.VMEM((1, Q_BLOCK), jnp.int32),
                        pltpu.VMEM((1, HQ), F32), pltpu.VMEM((1, HQ), F32), pltpu.VMEM((HEAD_DIM, HQ), F32)],
        compiler_params=_cparams(("parallel", "arbitrary")),
        name="dsa_attention",
    )(proj, proj, proj, proj, proj, q_gain, k_gain, bias)


NSA_G = NSA_HEADS // NSA_KV_HEADS
CMP_PAD = 56
CMP_NEAR = 64
SEL_TAKEN = -3e38


def _cmp_rows(S):
    return -(-(CMP_PAD + S // CMP_STRIDE) // 64) * 64


def _nsa_prep_body(ks_ref, vs_ref, kw_ref, vw_ref, kg_ref, kso_ref, vso_ref, kwo_ref, vwo_ref):
    dh = HEAD_DIM
    for g in range(NSA_KV_HEADS):
        cs = slice(g * dh, (g + 1) * dh)
        kso_ref[0, g] = _head_rms(ks_ref[:, cs], kg_ref[1:2, :], 1.0).astype(BF16)
        kwo_ref[0, g] = _head_rms(kw_ref[:, cs], kg_ref[2:3, :], 1.0).astype(BF16)
        vso_ref[0, g] = vs_ref[:, cs].T.astype(BF16)
        vwo_ref[0, g] = vw_ref[:, cs].T.astype(BF16)


def nsa_prep(proj, k_gain, B, S, ts=512):
    nt = S // ts
    c0 = COL["n_kv"] // 128
    col = lambda j: pl.BlockSpec((ts, 128), lambda b, t: (b * nt + t, c0 + j))
    k_out = pl.BlockSpec((1, NSA_KV_HEADS, ts, HEAD_DIM), lambda b, t: (b, 0, t, 0))
    v_out = pl.BlockSpec((1, NSA_KV_HEADS, HEAD_DIM, ts), lambda b, t: (b, 0, 0, t))
    k_sds = jax.ShapeDtypeStruct((B, NSA_KV_HEADS, S, HEAD_DIM), BF16)
    v_sds = jax.ShapeDtypeStruct((B, NSA_KV_HEADS, HEAD_DIM, S), BF16)
    return pl.pallas_call(
        _nsa_prep_body,
        grid=(B, nt),
        in_specs=[col(2), col(3), col(4), col(5), pl.BlockSpec((3, HEAD_DIM), lambda b, t: (0, 0))],
        out_specs=[k_out, v_out, k_out, v_out],
        out_shape=[k_sds, v_sds, k_sds, v_sds],
        compiler_params=_cparams(("parallel", "parallel")),
        name="nsa_prep",
    )(proj, proj, proj, proj, k_gain)


def _nsa_compress_body(xk_ref, xv_ref, pe_ref, w1_ref, w2_ref, kg_ref, kc_ref, vct_ref):
    R = xk_ref.shape[3]
    ncp = kc_ref.shape[2]
    half = CMP_STRIDE * HEAD_DIM
    row = lax.broadcasted_iota(jnp.int32, (R, HEAD_DIM), 0)
    for j, x_ref in enumerate((xk_ref, xv_ref)):
        x = x_ref[0, 0, 0]
        pe = pe_ref[j]
        lo = jnp.dot((x + pe[:, :half]).astype(BF16), w1_ref[j, :half, :], preferred_element_type=F32)
        hi = jnp.dot((x + pe[:, half:]).astype(BF16), w1_ref[j, half:, :], preferred_element_type=F32)
        hid = jax.nn.gelu(lo + pltpu.roll(hi, R - 1, 0))
        c = jnp.dot(hid.astype(BF16), w2_ref[j], preferred_element_type=F32)
        if j == 0:
            c = _head_rms(c, kg_ref[0:1, :], 1.0)
        c = jnp.where(row < R - 1, c, 0.0)
        c = jnp.concatenate([jnp.zeros((CMP_PAD, HEAD_DIM), F32), c,
                             jnp.zeros((ncp - CMP_PAD - R, HEAD_DIM), F32)], axis=0)
        if j == 0:
            kc_ref[0, 0] = c.astype(BF16)
        else:
            vct_ref[0, 0] = c.T.astype(BF16)


def nsa_compress(xc, pe, w1, w2, k_gain, S):
    B = xc.shape[0]
    R = S // CMP_STRIDE
    ncp = _cmp_rows(S)
    const = lambda *shape: pl.BlockSpec(shape, lambda b, g: (0,) * len(shape), pipeline_mode=pl.Buffered(1))
    return pl.pallas_call(
        _nsa_compress_body,
        grid=(B, NSA_KV_HEADS),
        in_specs=[pl.BlockSpec((1, 1, 1, R, CMP_STRIDE * HEAD_DIM), lambda b, g: (b, 0, g, 0, 0)),
                  pl.BlockSpec((1, 1, 1, R, CMP_STRIDE * HEAD_DIM), lambda b, g: (b, 1, g, 0, 0)),
                  const(2, 1, CMP_BLOCK * HEAD_DIM), const(2, CMP_BLOCK * HEAD_DIM, CMP_HIDDEN),
                  const(2, CMP_HIDDEN, HEAD_DIM), const(3, HEAD_DIM)],
        out_specs=[pl.BlockSpec((1, 1, ncp, HEAD_DIM), lambda b, g: (b, g, 0, 0)),
                   pl.BlockSpec((1, 1, HEAD_DIM, ncp), lambda b, g: (b, g, 0, 0))],
        out_shape=[jax.ShapeDtypeStruct((B, NSA_KV_HEADS, ncp, HEAD_DIM), BF16),
                   jax.ShapeDtypeStruct((B, NSA_KV_HEADS, HEAD_DIM, ncp), BF16)],
        compiler_params=_cparams(("parallel", "parallel")),
        name="nsa_compress",
    )(xc, xc, pe, w1, w2, k_gain)


def _attend_block(k_b, v_b, q_scr, bias_ref, d0, masks, m_scr, l_scr, acc_scr):
    QB = Q_BLOCK
    CH = 2 * QB
    n_sub = len(masks)
    masks2 = [jnp.concatenate([mk, mk], axis=1) for mk in masks]
    d = [jnp.clip(d0 - j, 0, BIAS_TILES - 1) for j in range(n_sub)]
    chunks = [slice(c * CH, (c + 1) * CH) for c in range(q_scr.shape[0] // CH)]
    logits = [lax.dot_general(k_b, q_scr[ls, :], (((1,), (1,)), ((), ())), preferred_element_type=F32)
              for ls in chunks]
    for ls, lg in zip(chunks, logits):
        parts = [jnp.where(masks2[j], lg[j * QB:(j + 1) * QB] + bias_ref[d[j], :, ls], NEG_INF)
                 for j in range(n_sub)]
        m_old = m_scr[:, ls]
        m_new = m_old
        for part in parts:
            m_new = jnp.maximum(m_new, jnp.max(part, axis=0, keepdims=True))
        probs = [jnp.exp2(part - m_new) for part in parts]
        alpha = jnp.exp2(m_old - m_new)
        l_new = alpha * l_scr[:, ls]
        for pr in probs:
            l_new = l_new + jnp.sum(pr, axis=0, keepdims=True)
        l_scr[:, ls] = l_new
        pb = jnp.concatenate([pr.astype(BF16) for pr in probs], axis=0)
        acc_scr[:, ls] = alpha * acc_scr[:, ls] + jnp.dot(v_b, pb, preferred_element_type=F32)
        m_scr[:, ls] = m_new


def _nsa_body(q_ref, gl_ref, kc_ref, vct_ref, ks_ref, vst_ref, kw_ref, vwt_ref, qg_ref, bias_ref,
              gcd_ref, c31_ref, ovl_ref, o_ref,
              qall_scr, lc_scr, sel_scr, gt_scr, ms_scr, ls_scr, as_scr, mw_scr, lw_scr, aw_scr,
              *, n_sel, top_n):
    g = pl.program_id(1)
    i = pl.program_id(2)
    dh = HEAD_DIM
    QB = Q_BLOCK
    GQ = NSA_G * QB
    ncp = kc_ref.shape[2]

    q = q_ref[...]
    for r in range(NSA_G):
        qall_scr[r * QB:(r + 1) * QB, :] = _head_rms(q[:, r * dh:(r + 1) * dh], qg_ref[...],
                                                     dh ** -0.5 * LOG2E).astype(BF16)
    gt_scr[...] = jax.nn.sigmoid(gl_ref[...]).T

    lc_scr[...] = lax.dot_general(kc_ref[0, 0], qall_scr[...], (((1,), (1,)), ((), ())),
                                  preferred_element_type=F32) + c31_ref[0]
    near = pl.ds(pl.multiple_of(i * (QB // CMP_STRIDE), 8), CMP_NEAR)
    lc_scr[near, :] = lc_scr[near, :] + gcd_ref[0]
    n_p = lax.broadcasted_iota(jnp.int32, (ncp, GQ), 0)
    t_c = i * QB + (lax.broadcasted_iota(jnp.int32, (ncp, GQ), 1) & (QB - 1))
    cmp_end = jnp.where(n_p >= CMP_PAD, (n_p - CMP_PAD) * CMP_STRIDE + (CMP_BLOCK - 1), 1 << 30)
    mask_c = cmp_end <= t_c
    lc = jnp.where(mask_c, lc_scr[...], NEG_INF)
    e = jnp.where(mask_c, jnp.exp2(lc - jnp.max(lc, axis=0, keepdims=True)), 0.0)
    pc = (e / jnp.maximum(jnp.sum(e, axis=0, keepdims=True), 1e-30)).astype(BF16)
    o_c = jnp.dot(vct_ref[0, 0], pc, preferred_element_type=F32)

    imp = jnp.zeros((n_sel, QB), F32)
    for r in range(NSA_G):
        imp = imp + jnp.dot(ovl_ref[...], pc[:, r * QB:(r + 1) * QB], preferred_element_type=F32)
    j_io = lax.broadcasted_iota(jnp.int32, (n_sel, QB), 0).astype(F32)
    cur = ((i * QB + lax.broadcasted_iota(jnp.int32, (n_sel, QB), 1)) >> 6).astype(F32)
    imp = jnp.where((j_io == 0) | (j_io == cur), FORCED_SCORE, imp)
    imp = jnp.where(j_io <= cur, imp, NEG_INF)
    sel = jnp.zeros((n_sel, QB), F32)
    for _ in range(top_n):
        mx = jnp.max(imp, axis=0, keepdims=True)
        first = jnp.min(jnp.where(imp == mx, j_io, float(n_sel)), axis=0, keepdims=True)
        hit = j_io == first
        sel = jnp.where(hit, 1.0, sel)
        imp = jnp.where(hit, SEL_TAKEN, imp)
    sel_scr[...] = sel

    for m_scr, l_scr, a_scr in ((ms_scr, ls_scr, as_scr), (mw_scr, lw_scr, aw_scr)):
        m_scr[...] = jnp.full(m_scr.shape, NEG_INF, F32)
        l_scr[...] = jnp.zeros(l_scr.shape, F32)
        a_scr[...] = jnp.zeros(a_scr.shape, F32)
    s_loc = lax.broadcasted_iota(jnp.int32, (QB, QB), 0)
    t_loc = lax.broadcasted_iota(jnp.int32, (QB, QB), 1)
    blocks_per_tile = QB // SEL_BLOCK

    def slc_block(kb, carry):
        rows = pl.ds(pl.multiple_of(kb * KEY_BLOCK, KEY_BLOCK), KEY_BLOCK)
        masks = []
        for j in range(KEY_TILES):
            kt = kb * KEY_TILES + j
            picked = jnp.where(s_loc < SEL_BLOCK, sel_scr[pl.ds(kt * blocks_per_tile, 1), :],
                               sel_scr[pl.ds(kt * blocks_per_tile + 1, 1), :])
            dist = (t_loc + i * QB) - (s_loc + kt * QB)
            masks.append(jnp.where(dist >= 0, picked, 0.0) > 0.5)
        _attend_block(ks_ref[0, 0, rows, :], vst_ref[0, 0, :, rows], qall_scr, bias_ref,
                      i - kb * KEY_TILES, masks, ms_scr, ls_scr, as_scr)
        return carry

    lax.fori_loop(0, (i + KEY_TILES) // KEY_TILES, slc_block, 0)

    kt0 = jnp.maximum(i - WINDOW // QB, 0)
    rows = pl.ds(pl.multiple_of(kt0 * QB, QB), WIN_TILES * QB)
    masks = []
    for j in range(WIN_TILES):
        dist = (t_loc + i * QB) - (s_loc + (kt0 + j) * QB)
        masks.append((dist >= 0) & (dist < WINDOW))
    _attend_block(kw_ref[0, 0, rows, :], vwt_ref[0, 0, :, rows], qall_scr, bias_ref, i - kt0, masks,
                  mw_scr, lw_scr, aw_scr)

    o_s = as_scr[...] / jnp.maximum(ls_scr[...], 1e-30)
    o_w = aw_scr[...] / jnp.maximum(lw_scr[...], 1e-30)
    outs = []
    for r in range(NSA_G):
        ls = slice(r * QB, (r + 1) * QB)
        gate = lambda j: gt_scr[pl.ds(j * NSA_HEADS + g * NSA_G + r, 1), :]
        o_r = gate(0) * o_c[:, ls] + gate(1) * o_s[:, ls] + gate(2) * o_w[:, ls]
        outs.append(o_r.T)
    o_ref[...] = jnp.concatenate(outs, axis=1)


def nsa_attention(proj, kc, vct, ks, vst, kw, vwt, q_gain, bias, gcd, c31, ovl, B, S):
    nq = S // Q_BLOCK
    GQ = NSA_G * Q_BLOCK
    GW = NSA_G * HEAD_DIM
    ncp = kc.shape[2]
    n_sel = S // SEL_BLOCK
    const = lambda *shape: pl.BlockSpec(shape, lambda b, g, i: (0,) * len(shape), pipeline_mode=pl.Buffered(1))
    per_group = lambda *shape: pl.BlockSpec((1, 1) + shape, lambda b, g, i: (b, g, 0, 0))
    return pl.pallas_call(
        functools.partial(_nsa_body, n_sel=n_sel, top_n=min(SEL_TOPN, n_sel)),
        grid=(B, NSA_KV_HEADS, nq),
        in_specs=[pl.BlockSpec((Q_BLOCK, GW), lambda b, g, i: (b * nq + i, COL["n_q"] // GW + g)),
                  pl.BlockSpec((Q_BLOCK, 128), lambda b, g, i: (b * nq + i, COL["n_g"] // 128)),
                  per_group(ncp, HEAD_DIM), per_group(HEAD_DIM, ncp),
                  per_group(S, HEAD_DIM), per_group(HEAD_DIM, S),
                  per_group(S, HEAD_DIM), per_group(HEAD_DIM, S),
                  const(1, HEAD_DIM),
                  pl.BlockSpec((BIAS_TILES, Q_BLOCK, GQ), lambda b, g, i: (0, 0, g)),
                  pl.BlockSpec((1, CMP_NEAR, GQ), lambda b, g, i: (g, 0, 0)),
                  pl.BlockSpec((1, 1, GQ), lambda b, g, i: (g, 0, 0)),
                  const(n_sel, ncp)],
        out_specs=pl.BlockSpec((Q_BLOCK, GW), lambda b, g, i: (b * nq + i, g)),
        out_shape=jax.ShapeDtypeStruct((B * S, MIX_W), F32),
        scratch_shapes=[pltpu.VMEM((GQ, HEAD_DIM), BF16), pltpu.VMEM((ncp, GQ), F32),
                        pltpu.VMEM((n_sel, Q_BLOCK), F32), pltpu.VMEM((128, Q_BLOCK), F32)]
                       + [pltpu.VMEM((1, GQ), F32), pltpu.VMEM((1, GQ), F32), pltpu.VMEM((HEAD_DIM, GQ), F32)] * 2,
        compiler_params=_cparams(("parallel", "parallel", "arbitrary")),
        name="nsa_attention",
    )(proj, proj, kc, vct, ks, vst, kw, vwt, q_gain, bias, gcd, c31, ovl)


def nsa_tables(tbl, S):
    ncp = _cmp_rows(S)
    n_rel = jnp.arange(CMP_NEAR)[:, None] - CMP_PAD
    t_loc = jnp.arange(Q_BLOCK)[None, :]
    near = tbl[rel_bucket(t_loc - CMP_STRIDE * n_rel - (CMP_BLOCK - 1))] * LOG2E
    far = tbl[N_BUCKETS - 1] * LOG2E
    gcd = jnp.moveaxis(near - far, -1, 0).reshape(NSA_KV_HEADS, NSA_G, CMP_NEAR, Q_BLOCK)
    gcd = jnp.moveaxis(gcd, 1, 2).reshape(NSA_KV_HEADS, CMP_NEAR, NSA_G * Q_BLOCK)
    c31 = jnp.repeat(far, Q_BLOCK).reshape(NSA_KV_HEADS, 1, NSA_G * Q_BLOCK)
    n = np.arange(ncp) - CMP_PAD
    sel_start = np.arange(S // SEL_BLOCK) * SEL_BLOCK
    start = n * CMP_STRIDE
    ovl = ((start[None, :] < sel_start[:, None] + SEL_BLOCK) & (start[None, :] + CMP_BLOCK > sel_start[:, None])
           & (n[None, :] >= 0) & (n[None, :] < (S - CMP_BLOCK) // CMP_STRIDE + 1))
    return gcd, c31, jnp.asarray(ovl, BF16)


def nsa_mixer_pallas(proj, cmp_pe, cmp_w1, cmp_w2, q_gain, k_gain, rel_tbl, bias, B, S):
    dh = HEAD_DIM
    c0 = COL["n_kv"]
    xc = proj[:, c0:c0 + 2 * NSA_KV_HEADS * dh].reshape(B, S // CMP_STRIDE, CMP_STRIDE, 2, NSA_KV_HEADS, dh)
    xc = xc.transpose(0, 3, 4, 1, 2, 5).reshape(B, 2, NSA_KV_HEADS, S // CMP_STRIDE, CMP_STRIDE * dh)
    kc, vct = nsa_compress(xc, cmp_pe.reshape(2, 1, CMP_BLOCK * dh), cmp_w1.astype(BF16),
                           cmp_w2.astype(BF16), k_gain, S)
    ks, vst, kw, vwt = nsa_prep(proj, k_gain, B, S)
    gcd, c31, ovl = nsa_tables(rel_tbl, S)
    return nsa_attention(proj, kc, vct, ks, vst, kw, vwt, q_gain[None], bias, gcd, c31, ovl, B, S)


def rel_bucket(dist):
    n = jnp.maximum(dist, 0)
    max_exact = N_BUCKETS // 2
    nf = jnp.maximum(n, 1).astype(jnp.float32)
    large = max_exact + (jnp.log(nf / max_exact) / math.log(MAX_DISTANCE / max_exact)
                         * (N_BUCKETS - max_exact)).astype(jnp.int32)
    return jnp.where(n < max_exact, n, jnp.minimum(large, N_BUCKETS - 1))


CONV_HALO = 8


def _conv_body(b_ref, c_ref, x_ref, w_ref, o_ref, prev_scr):
    @pl.when(pl.program_id(1) == 0)
    def _():
        prev_scr[...] = jnp.zeros(prev_scr.shape, F32)

    bx = c_ref[...] * x_ref[...]
    ts = bx.shape[0]
    row = lax.broadcasted_iota(jnp.int32, bx.shape, 0)
    prev = prev_scr[...]
    last1 = prev[CONV_HALO - 1:CONV_HALO]
    last2 = prev[CONV_HALO - 2:CONV_HALO - 1]
    back1 = jnp.where(row == 0, last1, pltpu.roll(bx, 1, 0))
    back2 = jnp.where(row == 0, last2, jnp.where(row == 1, last1, pltpu.roll(bx, 2, 0)))
    w = w_ref[...]
    o_ref[...] = b_ref[...] * (w[0:1] * back2 + w[1:2] * back1 + w[2:3] * bx)
    prev_scr[...] = bx[ts - CONV_HALO:ts]


def conv_mixer(proj, conv_w, B, S, ts=512):
    nt = S // ts
    col = lambda name: pl.BlockSpec((ts, MIX_W), lambda b, t: (b * nt + t, COL[name] // MIX_W))
    return pl.pallas_call(
        _conv_body,
        grid=(B, nt),
        in_specs=[col("a_b"), col("a_c"), col("a_x"), pl.BlockSpec((CONV_K, MIX_W), lambda b, t: (0, 0))],
        out_specs=pl.BlockSpec((ts, MIX_W), lambda b, t: (b * nt + t, 0)),
        out_shape=jax.ShapeDtypeStruct((B * S, MIX_W), F32),
        scratch_shapes=[pltpu.VMEM((CONV_HALO, MIX_W), F32)],
        compiler_params=_cparams(("parallel", "arbitrary")),
        name="conv_mixer",
    )(proj, proj, proj, conv_w)


def _sgu_body(uv_ref, g_ref, w_ref, b_ref, o_ref):
    uv = jax.nn.gelu(uv_ref[...])
    u = uv[:, :MIX_W]
    v = uv[:, MIX_W:]
    v = v - jnp.mean(v, axis=-1, keepdims=True)
    v = (v * lax.rsqrt(jnp.mean(v * v, axis=-1, keepdims=True) + EPS) * g_ref[...]).astype(BF16)
    gw = MIX_W // SGU_GROUPS
    group = lax.broadcasted_iota(jnp.int32, (SGU_CHUNK, MIX_W), 1) >> (gw.bit_length() - 1)
    for c in range(uv.shape[0] // SGU_CHUNK):
        rows = slice(c * SGU_CHUNK, (c + 1) * SGU_CHUNK)
        vc = v[rows]
        s = b_ref[...]
        for g in range(SGU_GROUPS):
            s = s + jnp.dot(w_ref[g], jnp.where(group == g, vc, jnp.zeros_like(vc)),
                            preferred_element_type=F32)
        o_ref[rows, :] = u[rows] * s


def sgu_mixer_pallas(proj, ln_gain, w_s, b_s, T, tm=512):
    tri = jnp.tril(jnp.ones((SGU_CHUNK, SGU_CHUNK), dtype=bool))
    w = jnp.where(tri[None], w_s, 0).astype(BF16)
    b = jnp.repeat(b_s.T, MIX_W // SGU_GROUPS, axis=1)
    const = lambda *shape: pl.BlockSpec(shape, lambda i: (0,) * len(shape), pipeline_mode=pl.Buffered(1))
    return pl.pallas_call(
        _sgu_body,
        grid=(T // tm,),
        in_specs=[pl.BlockSpec((tm, 2 * MIX_W), lambda i: (i, COL["c_uv"] // (2 * MIX_W))),
                  const(1, MIX_W), const(SGU_GROUPS, SGU_CHUNK, SGU_CHUNK), const(SGU_CHUNK, MIX_W)],
        out_specs=pl.BlockSpec((tm, MIX_W), lambda i: (i, 0)),
        out_shape=jax.ShapeDtypeStruct((T, MIX_W), F32),
        compiler_params=_cparams(("parallel",)),
        name="sgu_mixer",
    )(proj, ln_gain[None], w, b)


def _layer(x2, p2, B, S, rel_bias, g_mix, w_in, conv_w, nsa_cmp_pe, nsa_cmp_w1, nsa_cmp_w2,
           nsa_q_gain, nsa_k_gain, sgu_ln_gain, sgu_w, sgu_b, dsa_q_gain, dsa_k_gain,
           w_gate, w_branch, w_out, g_ffn, peer_wq, peer_subkeys, peer_u, peer_v,
           g_ple, w_ple_gate, w_ple_proj):
    T = B * S
    proj = rms_matmul(x2, g_mix[None], pack_w_in(w_in), tm=1024, tn=768)

    y_a = conv_mixer(proj, conv_w, B, S)
    y_b = nsa_mixer_pallas(proj, nsa_cmp_pe, nsa_cmp_w1, nsa_cmp_w2, nsa_q_gain, nsa_k_gain,
                           rel_bias[:, :NSA_HEADS], rel_bias_tiles(rel_bias[:, :NSA_HEADS]), B, S)
    y_c = sgu_mixer_pallas(proj, sgu_ln_gain, sgu_w, sgu_b, T)
    y_d = dsa_attention(proj, COL, dsa_q_gain[None], dsa_k_gain[None],
                        rel_bias_tiles(rel_bias[:, NSA_HEADS:]), B, S)
    ys = [y_a, y_b, y_c, y_d]

    x2 = merge_mixers(x2, g_mix[None], ys, w_gate.astype(BF16),
                      w_branch.reshape(N_MIXERS, MIX_W, D_MODEL).astype(BF16),
                      w_out.astype(BF16), tm=256)

    sk = peer_subkeys.reshape(2 * PEER_HEADS, PEER_KEYS, PEER_QDIM // 2).astype(BF16)
    x2 = peer_ffn_update(x2, g_ffn[None], peer_wq.astype(BF16), sk,
                         peer_u.astype(BF16), peer_v.astype(BF16).T)

    x2 = ple_update(x2, g_ple[None], p2, w_ple_gate.astype(BF16), w_ple_proj.astype(BF16), tm=512)
    return x2


def kernel(x, p, rel_bias, g_mix, w_in, conv_w, nsa_cmp_pe, nsa_cmp_w1, nsa_cmp_w2, nsa_q_gain,
           nsa_k_gain, sgu_ln_gain, sgu_w, sgu_b, dsa_q_gain, dsa_k_gain, w_gate, w_branch, w_out,
           g_ffn, peer_wq, peer_subkeys, peer_u, peer_v, g_ple, w_ple_gate, w_ple_proj):
    B, S, D = x.shape
    depth = p.shape[0]
    x2 = x.reshape(B * S, D)
    for l in range(depth):
        x2 = _layer(x2, p[l].reshape(B * S, PLE_DIM), B, S, rel_bias, g_mix[l], w_in[l], conv_w[l],
                    nsa_cmp_pe[l], nsa_cmp_w1[l], nsa_cmp_w2[l], nsa_q_gain[l], nsa_k_gain[l],
                    sgu_ln_gain[l], sgu_w[l], sgu_b[l], dsa_q_gain[l], dsa_k_gain[l],
                    w_gate[l], w_branch[l], w_out[l], g_ffn[l], peer_wq[l], peer_subkeys[l],
                    peer_u[l], peer_v[l], g_ple[l], w_ple_gate[l], w_ple_proj[l])
    return x2.reshape(B, S, D)
```

```python
import functools
import math

import jax
import jax.numpy as jnp
import numpy as np
from jax import lax
from jax.experimental import pallas as pl
from jax.experimental.pallas import tpu as pltpu

F32 = jnp.float32
BF16 = jnp.bfloat16

D_MODEL = 1024
HEAD_DIM = 64
N_MIXERS = 4
MIX_W = D_MODEL // 2
Q_BLOCK = 128
EPS = 1e-6
NEG_INF = -1e30
CONV_K = 3
NSA_HEADS = MIX_W // HEAD_DIM
NSA_KV_HEADS = 2
CMP_BLOCK = 32
CMP_STRIDE = 16
CMP_HIDDEN = 256
SEL_BLOCK = 64
SEL_TOPN = 8
WINDOW = 512
FORCED_SCORE = 1e4
SGU_CHUNK = 128
SGU_GROUPS = 8
DSA_HEADS = MIX_W // HEAD_DIM
IDX_HEADS = 8
IDX_DIM = 64
DSA_TOPK_MAX = 256
N_BUCKETS = 32
MAX_DISTANCE = 1024
PEER_HEADS = 8
PEER_KEYS = 128
PEER_QDIM = 128
PEER_TOPK = 16
N_EXPERTS = PEER_KEYS * PEER_KEYS
PLE_DIM = 256

SPLIT_WIDTHS = (
    MIX_W, MIX_W, MIX_W,
    NSA_HEADS * HEAD_DIM,
    6 * NSA_KV_HEADS * HEAD_DIM,
    3 * NSA_HEADS,
    2 * MIX_W,
    DSA_HEADS * HEAD_DIM, HEAD_DIM, HEAD_DIM,
    IDX_HEADS * IDX_DIM, IDX_DIM, IDX_HEADS,
)
IN_WIDTH = sum(SPLIT_WIDTHS)

COL = dict(a_b=0, a_c=512, a_x=1024, n_q=1536, d_q=2048, d_qi=2560, c_uv=3072, n_kv=4096,
           n_g=4864, d_kv=4992, d_kiw=5120)
N_PACK = 5376


def pack_w_in(w):
    o = dict(zip(("a_b", "a_c", "a_x", "n_q", "n_kv", "n_g", "c_uv", "d_q", "d_k", "d_v", "d_qi", "d_ki",
                  "d_wi", "end"), [0] + [int(c) for c in np.cumsum(SPLIT_WIDTHS)]))
    z = lambda n: jnp.zeros((w.shape[0], n), w.dtype)
    cols = [w[:, o["a_b"]:o["n_kv"]], w[:, o["d_q"]:o["d_k"]], w[:, o["d_qi"]:o["d_ki"]],
            w[:, o["c_uv"]:o["d_q"]], w[:, o["n_kv"]:o["n_g"]],
            w[:, o["n_g"]:o["c_uv"]], z(128 - 3 * NSA_HEADS),
            w[:, o["d_k"]:o["d_qi"]],
            w[:, o["d_ki"]:o["end"]], z(128 - IDX_DIM - IDX_HEADS)]
    packed = jnp.concatenate(cols, axis=1)
    return jnp.pad(packed, ((0, 0), (0, N_PACK - packed.shape[1]))).astype(BF16)

VMEM_LIMIT_BYTES = 56 * 1024 * 1024


def _cparams(sem):
    return pltpu.CompilerParams(dimension_semantics=sem, vmem_limit_bytes=VMEM_LIMIT_BYTES)


def _rms(x, g):
    return x * lax.rsqrt(jnp.mean(x * x, axis=-1, keepdims=True) + EPS) * g


def _rms_matmul_body(x_ref, g_ref, w_ref, o_ref, h_ref):
    @pl.when(pl.program_id(1) == 0)
    def _():
        h_ref[...] = _rms(x_ref[...], g_ref[...]).astype(BF16)

    o_ref[...] = jnp.dot(h_ref[...], w_ref[...], preferred_element_type=F32)


def rms_matmul(x, g, w, tm, tn):
    T, D = x.shape
    N = w.shape[1]
    return pl.pallas_call(
        _rms_matmul_body,
        grid=(T // tm, N // tn),
        in_specs=[pl.BlockSpec((tm, D), lambda i, j: (i, 0)),
                  pl.BlockSpec((1, D), lambda i, j: (0, 0)),
                  pl.BlockSpec((D, tn), lambda i, j: (0, j))],
        out_specs=pl.BlockSpec((tm, tn), lambda i, j: (i, j)),
        out_shape=jax.ShapeDtypeStruct((T, N), F32),
        scratch_shapes=[pltpu.VMEM((tm, D), BF16)],
        compiler_params=_cparams(("parallel", "arbitrary")),
        name="rms_matmul",
    )(x, g, w)


def _merge_body(x_ref, g_ref, ya_ref, yb_ref, yc_ref, yd_ref, wg_ref, wb_ref, wo_ref, o_ref):
    x = x_ref[...]
    h = _rms(x, g_ref[...]).astype(BF16)
    merged = jnp.zeros(x.shape, F32)
    for m, y_ref in enumerate((ya_ref, yb_ref, yc_ref, yd_ref)):
        z = jnp.dot(y_ref[...].astype(BF16), wb_ref[m], preferred_element_type=F32)
        gate = jax.nn.sigmoid(jnp.dot(h, wg_ref[:, m * D_MODEL:(m + 1) * D_MODEL],
                                      preferred_element_type=F32))
        merged = merged + gate * z
    o_ref[...] = x + jnp.dot(merged.astype(BF16), wo_ref[...], preferred_element_type=F32)


def merge_mixers(x, g, ys, w_gate, w_branch, w_out, tm):
    T, D = x.shape
    const = lambda *shape: pl.BlockSpec(shape, lambda i: (0,) * len(shape), pipeline_mode=pl.Buffered(1))
    return pl.pallas_call(
        _merge_body,
        grid=(T // tm,),
        in_specs=[pl.BlockSpec((tm, D), lambda i: (i, 0)),
                  const(1, D)]
                 + [pl.BlockSpec((tm, MIX_W), lambda i: (i, 0)) for _ in range(N_MIXERS)]
                 + [const(D, N_MIXERS * D), const(N_MIXERS, MIX_W, D), const(D, D)],
        out_specs=pl.BlockSpec((tm, D), lambda i: (i, 0)),
        out_shape=jax.ShapeDtypeStruct((T, D), F32),
        compiler_params=_cparams(("parallel",)),
        name="merge_mixers",
    )(x, g, *ys, w_gate, w_branch, w_out)


def _ple_body(x_ref, g_ref, p_ref, wg_ref, wp_ref, o_ref):
    x = x_ref[...]
    h = _rms(x, g_ref[...]).astype(BF16)
    gate = jax.nn.sigmoid(jnp.dot(h, wg_ref[...], preferred_element_type=F32))
    proj = jnp.dot(p_ref[...].astype(BF16), wp_ref[...], preferred_element_type=F32)
    o_ref[...] = x + gate * proj


def ple_update(x, g, p, w_gate, w_proj, tm):
    T, D = x.shape
    const = lambda *shape: pl.BlockSpec(shape, lambda i: (0,) * len(shape), pipeline_mode=pl.Buffered(1))
    return pl.pallas_call(
        _ple_body,
        grid=(T // tm,),
        in_specs=[pl.BlockSpec((tm, D), lambda i: (i, 0)), const(1, D),
                  pl.BlockSpec((tm, PLE_DIM), lambda i: (i, 0)),
                  const(D, D), const(PLE_DIM, D)],
        out_specs=pl.BlockSpec((tm, D), lambda i: (i, 0)),
        out_shape=jax.ShapeDtypeStruct((T, D), F32),
        compiler_params=_cparams(("parallel",)),
        name="ple_update",
    )(x, g, p, w_gate, w_proj)


TAKEN = -3e38
NO_PRIORITY = 1e9


def _extract_top16(cur, prio, tops_ref, one_per_round):
    rank = jnp.full(cur.shape, float(PEER_TOPK), F32)
    for r in range(PEER_TOPK):
        m = jnp.max(cur, axis=0, keepdims=True)
        if tops_ref is not None:
            tops_ref[r:r + 1, :] = m
        hit = cur == m
        if one_per_round:
            hit = prio == jnp.min(jnp.where(hit, prio, NO_PRIORITY), axis=0, keepdims=True)
        rank = jnp.where(hit, float(r), rank)
        cur = jnp.where(hit, TAKEN, cur)
    return rank


_CAND_ROWS_J = (16, 8, 5, 4, 3, 2, 2, 2)


def _peer_topk_body(x_ref, g_ref, wq_ref, sk_ref, hn_ref, cnt_ref, e1_ref, rank_ref, e2_ref,
                    t1_ref, t2_ref, j_ref):
    hn = _rms(x_ref[...], g_ref[...]).astype(BF16)
    hn_ref[...] = hn
    q = jnp.dot(hn, wq_ref[...], preferred_element_type=F32).astype(BF16)
    half = PEER_QDIM // 2
    tn = x_ref.shape[0]
    key_prio = lax.broadcasted_iota(jnp.int32, (PEER_KEYS, LANE_CHUNK), 0).astype(F32)
    j8 = lax.broadcasted_iota(jnp.int32, (8, LANE_CHUNK), 0)
    j16 = lax.broadcasted_iota(jnp.int32, (PEER_TOPK, LANE_CHUNK), 0)
    cand_prio = jnp.concatenate([j16] + [j8 + PEER_TOPK * i for i in range(1, 8)]
                                + [(j8 + 8) * PEER_TOPK], axis=0).astype(F32)
    def select_all(one_per_round):
        n_max = jnp.zeros((1, LANE_CHUNK), F32)
        for h in range(PEER_HEADS):
            scores = []
            for p in range(2):
                c0 = (2 * h + p) * half
                scores.append(lax.dot_general(sk_ref[2 * h + p], q[:, c0:c0 + half],
                                              (((1,), (1,)), ((), ())), preferred_element_type=F32))
            for c in range(tn // LANE_CHUNK):
                ls = slice(c * LANE_CHUNK, (c + 1) * LANE_CHUNK)
                s1 = scores[0][:, ls]
                s2 = scores[1][:, ls]
                rank1 = _extract_top16(s1, key_prio, t1_ref, one_per_round)
                rank2 = _extract_top16(s2, key_prio, t2_ref, one_per_round)
                a = t1_ref[...]
                b = t2_ref[...]
                b8 = b[0:8]
                pieces = [a[0:1] + b]
                for i in range(1, 8):
                    cand_i = a[i:i + 1] + b8
                    if _CAND_ROWS_J[i] < 8:
                        cand_i = jnp.where(j8 < _CAND_ROWS_J[i], cand_i, NEG_INF)
                    pieces.append(cand_i)
                pieces.append(a[8:16] + b[0:1])
                cand = jnp.concatenate(pieces, axis=0)
                taken = _extract_top16(cand, cand_prio, None, one_per_round) < PEER_TOPK
                top = a[0:1] + b[0:1]
                z = jnp.sum(jnp.where(taken, jnp.exp(cand - top), 0.0), axis=0, keepdims=True)
                one = jnp.where(taken, 1.0, 0.0)
                j_ref[0:1, :] = jnp.sum(one[0:PEER_TOPK], axis=0, keepdims=True)
                for i in range(1, 8):
                    j_ref[i:i + 1, :] = jnp.sum(one[8 + 8 * i:16 + 8 * i], axis=0, keepdims=True)
                j_ref[8:PEER_TOPK, :] = one[9 * 8:10 * 8]
                counts = j_ref[...]
                cnt = jnp.zeros(s1.shape, F32)
                for r in range(PEER_TOPK):
                    cnt = jnp.where(rank1 == float(r), counts[r:r + 1], cnt)
                in1 = rank1 < PEER_TOPK
                in2 = rank2 < PEER_TOPK
                cnt_ref[h, :, ls] = cnt
                rank_ref[h, :, ls] = rank2
                e1_ref[h, :, ls] = jnp.where(in1, jnp.exp(s1 - (a[0:1] + jnp.log(z))), 0.0)
                e2_ref[h, :, ls] = jnp.where(in2, jnp.exp(s2 - b[0:1]), 0.0)
                n_max = jnp.maximum(n_max, jnp.sum(jnp.where(in1, 1.0, 0.0), axis=0, keepdims=True))
                n_max = jnp.maximum(n_max, jnp.sum(jnp.where(in2, 1.0, 0.0), axis=0, keepdims=True))
                n_max = jnp.maximum(n_max, jnp.sum(counts, axis=0, keepdims=True))
        return n_max

    @pl.when(jnp.max(select_all(False)) > PEER_TOPK)
    def _():
        select_all(True)


def peer_topk(x, g, wq, subkeys, tn):
    T, D = x.shape
    const = lambda *shape: pl.BlockSpec(shape, lambda i: (0,) * len(shape), pipeline_mode=pl.Buffered(1))
    return pl.pallas_call(
        _peer_topk_body,
        grid=(T // tn,),
        in_specs=[pl.BlockSpec((tn, D), lambda i: (i, 0)), const(1, D), const(D, D),
                  const(2 * PEER_HEADS, PEER_KEYS, PEER_QDIM // 2)],
        out_specs=[pl.BlockSpec((tn, D), lambda i: (i, 0))]
                  + [pl.BlockSpec((PEER_HEADS, PEER_KEYS, tn), lambda i: (0, 0, i))] * 4,
        out_shape=[jax.ShapeDtypeStruct((T, D), BF16)]
                  + [jax.ShapeDtypeStruct((PEER_HEADS, PEER_KEYS, T), F32)] * 4,
        scratch_shapes=[pltpu.VMEM((PEER_TOPK, LANE_CHUNK), F32)] * 3,
        compiler_params=_cparams(("parallel",)),
        name="peer_topk",
    )(x, g, wq, subkeys)


LANE_CHUNK = 128


PEER_I1_STEP = 8


def _peer_main_body(hn_ref, cnt_ref, e1_ref, rank_ref, e2_ref, u_ref, vt_prev_ref, vt_ref, x_ref, o_ref,
                    acc_ref, p_ref):
    j = pl.program_id(1)
    tn = hn_ref.shape[0]
    slot = j % 2

    @pl.when(j == 0)
    def _():
        acc_ref[...] = jnp.zeros(acc_ref.shape, F32)
        p_ref[1] = jnp.zeros(p_ref.shape[1:], BF16)

    hn = hn_ref[...]
    for k in range(PEER_I1_STEP):
        rows = slice(k * PEER_KEYS, (k + 1) * PEER_KEYS)
        at = lax.dot_general(u_ref[rows, :], hn, (((1,), (1,)), ((), ())),
                             preferred_element_type=F32)
        for c in range(tn // LANE_CHUNK):
            ls = slice(c * LANE_CHUNK, (c + 1) * LANE_CHUNK)
            w = jnp.zeros((PEER_KEYS, LANE_CHUNK), F32)
            for h in range(PEER_HEADS):
                picked = jnp.where(rank_ref[h, :, ls] < cnt_ref[h, k:k + 1, ls], e2_ref[h, :, ls], 0.0)
                w = w + picked * e1_ref[h, k:k + 1, ls]
            p_ref[slot, rows, ls] = (w * jax.nn.gelu(at[:, ls])).astype(BF16)

    acc_ref[...] += jnp.dot(vt_prev_ref[...], p_ref[1 - slot], preferred_element_type=F32)

    @pl.when(j == pl.num_programs(1) - 1)
    def _():
        acc = acc_ref[...] + jnp.dot(vt_ref[...], p_ref[slot], preferred_element_type=F32)
        o_ref[...] = x_ref[...] + acc.T


def peer_main(hn, cnt, e1, rank2, e2, u, vt, x, tn):
    T, D = x.shape
    e_blk = PEER_I1_STEP * PEER_KEYS
    prev = lambda j: jnp.maximum(j - 1, 0)
    per_i1 = pl.BlockSpec((PEER_HEADS, PEER_I1_STEP, tn), lambda i, j: (0, j, i))
    once = pl.Buffered(1)
    per_i2 = pl.BlockSpec((PEER_HEADS, PEER_KEYS, tn), lambda i, j: (0, 0, i), pipeline_mode=once)
    return pl.pallas_call(
        _peer_main_body,
        grid=(T // tn, PEER_KEYS // PEER_I1_STEP),
        in_specs=[pl.BlockSpec((tn, D), lambda i, j: (i, 0), pipeline_mode=once),
                  per_i1, per_i1, per_i2, per_i2,
                  pl.BlockSpec((e_blk, D), lambda i, j: (j, 0)),
                  pl.BlockSpec((D, e_blk), lambda i, j: (0, prev(j))),
                  pl.BlockSpec((D, e_blk), lambda i, j: (0, j)),
                  pl.BlockSpec((tn, D), lambda i, j: (i, 0), pipeline_mode=once)],
        out_specs=pl.BlockSpec((tn, D), lambda i, j: (i, 0)),
        out_shape=jax.ShapeDtypeStruct((T, D), F32),
        scratch_shapes=[pltpu.VMEM((D, tn), F32), pltpu.VMEM((2, e_blk, tn), BF16)],
        compiler_params=_cparams(("parallel", "arbitrary")),
        name="peer_main",
    )(hn, cnt, e1, rank2, e2, u, vt, vt, x)


def peer_ffn_update(x, g, wq, subkeys, u, vt):
    hn, cnt, e1, rank2, e2 = peer_topk(x, g, wq, subkeys, tn=256)
    return peer_main(hn, cnt, e1, rank2, e2, u, vt, x, tn=1024)


BIAS_TILES = 9
LOG2E = math.log2(math.e)
KEY_TILES = 4
KEY_BLOCK = KEY_TILES * Q_BLOCK
WIN_TILES = WINDOW // Q_BLOCK + 1


def rel_bias_tiles(tbl):
    H = tbl.shape[1]
    d = jnp.arange(BIAS_TILES)[:, None, None]
    s = jnp.arange(Q_BLOCK)[None, :, None]
    t = jnp.arange(Q_BLOCK)[None, None, :]
    bias = tbl[rel_bucket(d * Q_BLOCK + t - s)] * LOG2E
    return jnp.moveaxis(bias, -1, 2).reshape(BIAS_TILES, Q_BLOCK, H * Q_BLOCK)


INT_MIN = -2 ** 31
NEG_INF_KEY = -1900671691


def _sortable_key(x):
    bits = pltpu.bitcast(x, jnp.int32)
    return jnp.where(bits < 0, bits ^ jnp.int32(0x7FFFFFFF), bits)


def _head_rms(x, g, scale):
    return x * lax.rsqrt(jnp.mean(x * x, axis=-1, keepdims=True) + EPS) * g * scale


def _dsa_body(kv_ref, kiw_ref, q_ref, qi_ref, wq_ref, qg_ref, kg_ref, bias_ref, o_ref,
              kn_scr, ki_scr, vt_scr, qall_scr, qiall_scr, sc_scr, th_scr,
              m_scr, l_scr, acc_scr, *, top_k):
    i = pl.program_id(1)
    dh = HEAD_DIM
    QB = Q_BLOCK

    @pl.when(i == 0)
    def _prep():
        kv = kv_ref[...]
        kn_scr[...] = _head_rms(kv[:, :dh], kg_ref[...], 1.0).astype(BF16)
        vt_scr[...] = kv[:, dh:].T.astype(BF16)
        ki_scr[...] = kiw_ref[:, :dh].astype(BF16)

    q = q_ref[...]
    qi = qi_ref[...]
    for h in range(DSA_HEADS):
        qh = _head_rms(q[:, h * dh:(h + 1) * dh], qg_ref[...], dh ** -0.5 * LOG2E)
        qall_scr[h * QB:(h + 1) * QB, :] = qh.astype(BF16)
        qiall_scr[h * QB:(h + 1) * QB, :] = qi[:, h * dh:(h + 1) * dh].astype(BF16)
    w_t = (wq_ref[:, dh:dh + IDX_HEADS] * (IDX_HEADS ** -0.5)).T

    s_loc = lax.broadcasted_iota(jnp.int32, (QB, QB), 0)
    t_loc = lax.broadcasted_iota(jnp.int32, (QB, QB), 1)
    n_tiles = i + 1
    n_blk = (i + KEY_TILES) // KEY_TILES
    nt_dims = (((1,), (1,)), ((), ()))

    def score_block(kb, carry):
        rows = pl.ds(pl.multiple_of(kb * KEY_BLOCK, KEY_BLOCK), KEY_BLOCK)
        kib = ki_scr[rows, :]
        sc = jnp.zeros((KEY_BLOCK, QB), F32)
        for h in range(IDX_HEADS):
            r = lax.dot_general(kib, qiall_scr[h * QB:(h + 1) * QB, :], nt_dims,
                                preferred_element_type=F32)
            sc = sc + w_t[h:h + 1, :] * jnp.maximum(r, 0.0)
        sc = jnp.where(sc == 0.0, 0.0, sc)
        s_pos = kb * KEY_BLOCK + lax.broadcasted_iota(jnp.int32, (KEY_BLOCK, QB), 0)
        t_pos = i * QB + lax.broadcasted_iota(jnp.int32, (KEY_BLOCK, QB), 1)
        sc_scr[rows, :] = _sortable_key(jnp.where(s_pos <= t_pos, sc, NEG_INF))
        return carry

    lax.fori_loop(0, n_blk, score_block, 0)

    def count_ge(cand):
        def body(kb, c):
            blk = sc_scr[pl.ds(pl.multiple_of(kb * KEY_BLOCK, KEY_BLOCK), KEY_BLOCK), :]
            return c + jnp.sum((blk >= cand).astype(jnp.int32), axis=0, keepdims=True)
        return lax.fori_loop(0, n_blk, body, jnp.zeros((1, QB), jnp.int32))

    th_scr[...] = jnp.full((1, QB), NEG_INF_KEY + 1, jnp.int32)

    @pl.when(n_tiles * QB > top_k)
    def _select():
        def bit_body(it, carry):
            ans_u, cnt_ans = carry
            cand_u = ans_u | (jnp.int32(1) << (31 - it))
            c = count_ge(cand_u ^ jnp.int32(INT_MIN))
            ok = c >= top_k
            return jnp.where(ok, cand_u, ans_u), jnp.where(ok, c, cnt_ans)

        ans_u, cnt = lax.fori_loop(
            0, 32, bit_body,
            (jnp.zeros((1, QB), jnp.int32), jnp.full((1, QB), 1, jnp.int32) * (n_blk * KEY_BLOCK)))
        th_scr[...] = ans_u ^ jnp.int32(INT_MIN)

        @pl.when(jnp.max(cnt) > top_k)
        def _ties():
            theta = th_scr[...]
            need = (top_k - count_ge(theta + 1)).astype(F32)
            tri = (s_loc >= t_loc).astype(BF16)

            def body(kt, seen):
                rows = pl.ds(pl.multiple_of(kt * QB, QB), QB)
                tile = sc_scr[rows, :]
                eq = tile == theta
                rank = seen + jnp.dot(tri, eq.astype(BF16), preferred_element_type=F32)
                sc_scr[rows, :] = jnp.where(eq & (rank > need), theta - 1, tile)
                return seen + jnp.sum(eq.astype(F32), axis=0, keepdims=True)

            lax.fori_loop(0, n_tiles, body, jnp.zeros((1, QB), F32))

    theta = th_scr[...]
    m_scr[...] = jnp.full(m_scr.shape, NEG_INF, F32)
    l_scr[...] = jnp.zeros(l_scr.shape, F32)
    acc_scr[...] = jnp.zeros(acc_scr.shape, F32)

    def att_block(kb, carry):
        row0 = pl.multiple_of(kb * KEY_BLOCK, KEY_BLOCK)
        rows = pl.ds(row0, KEY_BLOCK)
        masks = [sc_scr[pl.ds(row0 + j * QB, QB), :] >= theta for j in range(KEY_TILES)]
        _attend_block(kn_scr[rows, :], vt_scr[:, rows], qall_scr, bias_ref, i - kb * KEY_TILES,
                      masks, m_scr, l_scr, acc_scr)
        return carry

    lax.fori_loop(0, n_blk, att_block, 0)

    o_t = acc_scr[...] / jnp.maximum(l_scr[...], 1e-30)
    o_ref[...] = jnp.concatenate([o_t[:, h * QB:(h + 1) * QB].T for h in range(DSA_HEADS)], axis=1)


def dsa_attention(proj, col, q_gain, k_gain, bias, B, S):
    nq = S // Q_BLOCK
    top_k = min(DSA_TOPK_MAX, S // 4)
    HQ = DSA_HEADS * Q_BLOCK
    const = lambda *shape: pl.BlockSpec(shape, lambda b, i: (0,) * len(shape), pipeline_mode=pl.Buffered(1))
    return pl.pallas_call(
        functools.partial(_dsa_body, top_k=top_k),
        grid=(B, nq),
        in_specs=[pl.BlockSpec((S, 128), lambda b, i: (b, col["d_kv"] // 128)),
                  pl.BlockSpec((S, 128), lambda b, i: (b, col["d_kiw"] // 128)),
                  pl.BlockSpec((Q_BLOCK, MIX_W), lambda b, i: (b * nq + i, col["d_q"] // MIX_W)),
                  pl.BlockSpec((Q_BLOCK, MIX_W), lambda b, i: (b * nq + i, col["d_qi"] // MIX_W)),
                  pl.BlockSpec((Q_BLOCK, 128), lambda b, i: (b * nq + i, col["d_kiw"] // 128)),
                  const(1, HEAD_DIM), const(1, HEAD_DIM), const(BIAS_TILES, Q_BLOCK, HQ)],
        out_specs=pl.BlockSpec((Q_BLOCK, MIX_W), lambda b, i: (b * nq + i, 0)),
        out_shape=jax.ShapeDtypeStruct((B * S, MIX_W), F32),
        scratch_shapes=[pltpu.VMEM((S, HEAD_DIM), BF16), pltpu.VMEM((S, HEAD_DIM), BF16),
                        pltpu.VMEM((HEAD_DIM, S), BF16),
                        pltpu.VMEM((HQ, HEAD_DIM), BF16), pltpu.VMEM((HQ, HEAD_DIM), BF16),
                        pltpu.VMEM((S, Q_BLOCK), jnp.int32),
                        pltpu.VMEM((1, Q_BLOCK), jnp.int32),
                        pltpu.VMEM((1, HQ), F32), pltpu.VMEM((1, HQ), F32), pltpu.VMEM((HEAD_DIM, HQ), F32)],
        compiler_params=_cparams(("parallel", "arbitrary")),
        name="dsa_attention",
    )(proj, proj, proj, proj, proj, q_gain, k_gain, bias)


NSA_G = NSA_HEADS // NSA_KV_HEADS
CMP_PAD = 56
CMP_NEAR = 64
SEL_TAKEN = -3e38


def _cmp_rows(S):
    return -(-(CMP_PAD + S // CMP_STRIDE) // 64) * 64


def _nsa_prep_body(ks_ref, vs_ref, kw_ref, vw_ref, kg_ref, kso_ref, vso_ref, kwo_ref, vwo_ref):
    dh = HEAD_DIM
    for g in range(NSA_KV_HEADS):
        cs = slice(g * dh, (g + 1) * dh)
        kso_ref[0, g] = _head_rms(ks_ref[:, cs], kg_ref[1:2, :], 1.0).astype(BF16)
        kwo_ref[0, g] = _head_rms(kw_ref[:, cs], kg_ref[2:3, :], 1.0).astype(BF16)
        vso_ref[0, g] = vs_ref[:, cs].T.astype(BF16)
        vwo_ref[0, g] = vw_ref[:, cs].T.astype(BF16)


def nsa_prep(proj, k_gain, B, S, ts=512):
    nt = S // ts
    c0 = COL["n_kv"] // 128
    col = lambda j: pl.BlockSpec((ts, 128), lambda b, t: (b * nt + t, c0 + j))
    k_out = pl.BlockSpec((1, NSA_KV_HEADS, ts, HEAD_DIM), lambda b, t: (b, 0, t, 0))
    v_out = pl.BlockSpec((1, NSA_KV_HEADS, HEAD_DIM, ts), lambda b, t: (b, 0, 0, t))
    k_sds = jax.ShapeDtypeStruct((B, NSA_KV_HEADS, S, HEAD_DIM), BF16)
    v_sds = jax.ShapeDtypeStruct((B, NSA_KV_HEADS, HEAD_DIM, S), BF16)
    return pl.pallas_call(
        _nsa_prep_body,
        grid=(B, nt),
        in_specs=[col(2), col(3), col(4), col(5), pl.BlockSpec((3, HEAD_DIM), lambda b, t: (0, 0))],
        out_specs=[k_out, v_out, k_out, v_out],
        out_shape=[k_sds, v_sds, k_sds, v_sds],
        compiler_params=_cparams(("parallel", "parallel")),
        name="nsa_prep",
    )(proj, proj, proj, proj, k_gain)


def _nsa_compress_body(xk_ref, xv_ref, pe_ref, w1_ref, w2_ref, kg_ref, kc_ref, vct_ref):
    R = xk_ref.shape[3]
    ncp = kc_ref.shape[2]
    half = CMP_STRIDE * HEAD_DIM
    row = lax.broadcasted_iota(jnp.int32, (R, HEAD_DIM), 0)
    for j, x_ref in enumerate((xk_ref, xv_ref)):
        x = x_ref[0, 0, 0]
        pe = pe_ref[j]
        lo = jnp.dot((x + pe[:, :half]).astype(BF16), w1_ref[j, :half, :], preferred_element_type=F32)
        hi = jnp.dot((x + pe[:, half:]).astype(BF16), w1_ref[j, half:, :], preferred_element_type=F32)
        hid = jax.nn.gelu(lo + pltpu.roll(hi, R - 1, 0))
        c = jnp.dot(hid.astype(BF16), w2_ref[j], preferred_element_type=F32)
        if j == 0:
            c = _head_rms(c, kg_ref[0:1, :], 1.0)
        c = jnp.where(row < R - 1, c, 0.0)
        c = jnp.concatenate([jnp.zeros((CMP_PAD, HEAD_DIM), F32), c,
                             jnp.zeros((ncp - CMP_PAD - R, HEAD_DIM), F32)], axis=0)
        if j == 0:
            kc_ref[0, 0] = c.astype(BF16)
        else:
            vct_ref[0, 0] = c.T.astype(BF16)


def nsa_compress(xc, pe, w1, w2, k_gain, S):
    B = xc.shape[0]
    R = S // CMP_STRIDE
    ncp = _cmp_rows(S)
    const = lambda *shape: pl.BlockSpec(shape, lambda b, g: (0,) * len(shape), pipeline_mode=pl.Buffered(1))
    return pl.pallas_call(
        _nsa_compress_body,
        grid=(B, NSA_KV_HEADS),
        in_specs=[pl.BlockSpec((1, 1, 1, R, CMP_STRIDE * HEAD_DIM), lambda b, g: (b, 0, g, 0, 0)),
                  pl.BlockSpec((1, 1, 1, R, CMP_STRIDE * HEAD_DIM), lambda b, g: (b, 1, g, 0, 0)),
                  const(2, 1, CMP_BLOCK * HEAD_DIM), const(2, CMP_BLOCK * HEAD_DIM, CMP_HIDDEN),
                  const(2, CMP_HIDDEN, HEAD_DIM), const(3, HEAD_DIM)],
        out_specs=[pl.BlockSpec((1, 1, ncp, HEAD_DIM), lambda b, g: (b, g, 0, 0)),
                   pl.BlockSpec((1, 1, HEAD_DIM, ncp), lambda b, g: (b, g, 0, 0))],
        out_shape=[jax.ShapeDtypeStruct((B, NSA_KV_HEADS, ncp, HEAD_DIM), BF16),
                   jax.ShapeDtypeStruct((B, NSA_KV_HEADS, HEAD_DIM, ncp), BF16)],
        compiler_params=_cparams(("parallel", "parallel")),
        name="nsa_compress",
    )(xc, xc, pe, w1, w2, k_gain)


def _attend_block(k_b, v_b, q_scr, bias_ref, d0, masks, m_scr, l_scr, acc_scr):
    QB = Q_BLOCK
    CH = 2 * QB
    n_sub = len(masks)
    masks2 = [jnp.concatenate([mk, mk], axis=1) for mk in masks]
    d = [jnp.clip(d0 - j, 0, BIAS_TILES - 1) for j in range(n_sub)]
    chunks = [slice(c * CH, (c + 1) * CH) for c in range(q_scr.shape[0] // CH)]
    logits = [lax.dot_general(k_b, q_scr[ls, :], (((1,), (1,)), ((), ())), preferred_element_type=F32)
              for ls in chunks]
    for ls, lg in zip(chunks, logits):
        parts = [jnp.where(masks2[j], lg[j * QB:(j + 1) * QB] + bias_ref[d[j], :, ls], NEG_INF)
                 for j in range(n_sub)]
        m_old = m_scr[:, ls]
        m_new = m_old
        for part in parts:
            m_new = jnp.maximum(m_new, jnp.max(part, axis=0, keepdims=True))
        probs = [jnp.exp2(part - m_new) for part in parts]
        alpha = jnp.exp2(m_old - m_new)
        l_new = alpha * l_scr[:, ls]
        for pr in probs:
            l_new = l_new + jnp.sum(pr, axis=0, keepdims=True)
        l_scr[:, ls] = l_new
        pb = jnp.concatenate([pr.astype(BF16) for pr in probs], axis=0)
        acc_scr[:, ls] = alpha * acc_scr[:, ls] + jnp.dot(v_b, pb, preferred_element_type=F32)
        m_scr[:, ls] = m_new


def _nsa_body(q_ref, gl_ref, kc_ref, vct_ref, ks_ref, vst_ref, kw_ref, vwt_ref, qg_ref, bias_ref,
              gcd_ref, c31_ref, ovl_ref, o_ref,
              qall_scr, lc_scr, sel_scr, gt_scr, ms_scr, ls_scr, as_scr, mw_scr, lw_scr, aw_scr,
              *, n_sel, top_n):
    g = pl.program_id(1)
    i = pl.program_id(2)
    dh = HEAD_DIM
    QB = Q_BLOCK
    GQ = NSA_G * QB
    ncp = kc_ref.shape[2]

    q = q_ref[...]
    for r in range(NSA_G):
        qall_scr[r * QB:(r + 1) * QB, :] = _head_rms(q[:, r * dh:(r + 1) * dh], qg_ref[...],
                                                     dh ** -0.5 * LOG2E).astype(BF16)
    gt_scr[...] = jax.nn.sigmoid(gl_ref[...]).T

    lc_scr[...] = lax.dot_general(kc_ref[0, 0], qall_scr[...], (((1,), (1,)), ((), ())),
                                  preferred_element_type=F32) + c31_ref[0]
    near = pl.ds(pl.multiple_of(i * (QB // CMP_STRIDE), 8), CMP_NEAR)
    lc_scr[near, :] = lc_scr[near, :] + gcd_ref[0]
    n_p = lax.broadcasted_iota(jnp.int32, (ncp, GQ), 0)
    t_c = i * QB + (lax.broadcasted_iota(jnp.int32, (ncp, GQ), 1) & (QB - 1))
    cmp_end = jnp.where(n_p >= CMP_PAD, (n_p - CMP_PAD) * CMP_STRIDE + (CMP_BLOCK - 1), 1 << 30)
    mask_c = cmp_end <= t_c
    lc = jnp.where(mask_c, lc_scr[...], NEG_INF)
    e = jnp.where(mask_c, jnp.exp2(lc - jnp.max(lc, axis=0, keepdims=True)), 0.0)
    pc = (e / jnp.maximum(jnp.sum(e, axis=0, keepdims=True), 1e-30)).astype(BF16)
    o_c = jnp.dot(vct_ref[0, 0], pc, preferred_element_type=F32)

    imp = jnp.zeros((n_sel, QB), F32)
    for r in range(NSA_G):
        imp = imp + jnp.dot(ovl_ref[...], pc[:, r * QB:(r + 1) * QB], preferred_element_type=F32)
    j_io = lax.broadcasted_iota(jnp.int32, (n_sel, QB), 0).astype(F32)
    cur = ((i * QB + lax.broadcasted_iota(jnp.int32, (n_sel, QB), 1)) >> 6).astype(F32)
    imp = jnp.where((j_io == 0) | (j_io == cur), FORCED_SCORE, imp)
    imp = jnp.where(j_io <= cur, imp, NEG_INF)
    sel = jnp.zeros((n_sel, QB), F32)
    for _ in range(top_n):
        mx = jnp.max(imp, axis=0, keepdims=True)
        first = jnp.min(jnp.where(imp == mx, j_io, float(n_sel)), axis=0, keepdims=True)
        hit = j_io == first
        sel = jnp.where(hit, 1.0, sel)
        imp = jnp.where(hit, SEL_TAKEN, imp)
    sel_scr[...] = sel

    for m_scr, l_scr, a_scr in ((ms_scr, ls_scr, as_scr), (mw_scr, lw_scr, aw_scr)):
        m_scr[...] = jnp.full(m_scr.shape, NEG_INF, F32)
        l_scr[...] = jnp.zeros(l_scr.shape, F32)
        a_scr[...] = jnp.zeros(a_scr.shape, F32)
    s_loc = lax.broadcasted_iota(jnp.int32, (QB, QB), 0)
    t_loc = lax.broadcasted_iota(jnp.int32, (QB, QB), 1)
    blocks_per_tile = QB // SEL_BLOCK

    def slc_block(kb, carry):
        rows = pl.ds(pl.multiple_of(kb * KEY_BLOCK, KEY_BLOCK), KEY_BLOCK)
        masks = []
        for j in range(KEY_TILES):
            kt = kb * KEY_TILES + j
            picked = jnp.where(s_loc < SEL_BLOCK, sel_scr[pl.ds(kt * blocks_per_tile, 1), :],
                               sel_scr[pl.ds(kt * blocks_per_tile + 1, 1), :])
            dist = (t_loc + i * QB) - (s_loc + kt * QB)
            masks.append(jnp.where(dist >= 0, picked, 0.0) > 0.5)
        _attend_block(ks_ref[0, 0, rows, :], vst_ref[0, 0, :, rows], qall_scr, bias_ref,
                      i - kb * KEY_TILES, masks, ms_scr, ls_scr, as_scr)
        return carry

    lax.fori_loop(0, (i + KEY_TILES) // KEY_TILES, slc_block, 0)

    kt0 = jnp.maximum(i - WINDOW // QB, 0)
    rows = pl.ds(pl.multiple_of(kt0 * QB, QB), WIN_TILES * QB)
    masks = []
    for j in range(WIN_TILES):
        dist = (t_loc + i * QB) - (s_loc + (kt0 + j) * QB)
        masks.append((dist >= 0) & (dist < WINDOW))
    _attend_block(kw_ref[0, 0, rows, :], vwt_ref[0, 0, :, rows], qall_scr, bias_ref, i - kt0, masks,
                  mw_scr, lw_scr, aw_scr)

    o_s = as_scr[...] / jnp.maximum(ls_scr[...], 1e-30)
    o_w = aw_scr[...] / jnp.maximum(lw_scr[...], 1e-30)
    outs = []
    for r in range(NSA_G):
        ls = slice(r * QB, (r + 1) * QB)
        gate = lambda j: gt_scr[pl.ds(j * NSA_HEADS + g * NSA_G + r, 1), :]
        o_r = gate(0) * o_c[:, ls] + gate(1) * o_s[:, ls] + gate(2) * o_w[:, ls]
        outs.append(o_r.T)
    o_ref[...] = jnp.concatenate(outs, axis=1)


def nsa_attention(proj, kc, vct, ks, vst, kw, vwt, q_gain, bias, gcd, c31, ovl, B, S):
    nq = S // Q_BLOCK
    GQ = NSA_G * Q_BLOCK
    GW = NSA_G * HEAD_DIM
    ncp = kc.shape[2]
    n_sel = S // SEL_BLOCK
    const = lambda *shape: pl.BlockSpec(shape, lambda b, g, i: (0,) * len(shape), pipeline_mode=pl.Buffered(1))
    per_group = lambda *shape: pl.BlockSpec((1, 1) + shape, lambda b, g, i: (b, g, 0, 0))
    return pl.pallas_call(
        functools.partial(_nsa_body, n_sel=n_sel, top_n=min(SEL_TOPN, n_sel)),
        grid=(B, NSA_KV_HEADS, nq),
        in_specs=[pl.BlockSpec((Q_BLOCK, GW), lambda b, g, i: (b * nq + i, COL["n_q"] // GW + g)),
                  pl.BlockSpec((Q_BLOCK, 128), lambda b, g, i: (b * nq + i, COL["n_g"] // 128)),
                  per_group(ncp, HEAD_DIM), per_group(HEAD_DIM, ncp),
                  per_group(S, HEAD_DIM), per_group(HEAD_DIM, S),
                  per_group(S, HEAD_DIM), per_group(HEAD_DIM, S),
                  const(1, HEAD_DIM),
                  pl.BlockSpec((BIAS_TILES, Q_BLOCK, GQ), lambda b, g, i: (0, 0, g)),
                  pl.BlockSpec((1, CMP_NEAR, GQ), lambda b, g, i: (g, 0, 0)),
                  pl.BlockSpec((1, 1, GQ), lambda b, g, i: (g, 0, 0)),
                  const(n_sel, ncp)],
        out_specs=pl.BlockSpec((Q_BLOCK, GW), lambda b, g, i: (b * nq + i, g)),
        out_shape=jax.ShapeDtypeStruct((B * S, MIX_W), F32),
        scratch_shapes=[pltpu.VMEM((GQ, HEAD_DIM), BF16), pltpu.VMEM((ncp, GQ), F32),
                        pltpu.VMEM((n_sel, Q_BLOCK), F32), pltpu.VMEM((128, Q_BLOCK), F32)]
                       + [pltpu.VMEM((1, GQ), F32), pltpu.VMEM((1, GQ), F32), pltpu.VMEM((HEAD_DIM, GQ), F32)] * 2,
        compiler_params=_cparams(("parallel", "parallel", "arbitrary")),
        name="nsa_attention",
    )(proj, proj, kc, vct, ks, vst, kw, vwt, q_gain, bias, gcd, c31, ovl)


def nsa_tables(tbl, S):
    ncp = _cmp_rows(S)
    n_rel = jnp.arange(CMP_NEAR)[:, None] - CMP_PAD
    t_loc = jnp.arange(Q_BLOCK)[None, :]
    near = tbl[rel_bucket(t_loc - CMP_STRIDE * n_rel - (CMP_BLOCK - 1))] * LOG2E
    far = tbl[N_BUCKETS - 1] * LOG2E
    gcd = jnp.moveaxis(near - far, -1, 0).reshape(NSA_KV_HEADS, NSA_G, CMP_NEAR, Q_BLOCK)
    gcd = jnp.moveaxis(gcd, 1, 2).reshape(NSA_KV_HEADS, CMP_NEAR, NSA_G * Q_BLOCK)
    c31 = jnp.repeat(far, Q_BLOCK).reshape(NSA_KV_HEADS, 1, NSA_G * Q_BLOCK)
    n = np.arange(ncp) - CMP_PAD
    sel_start = np.arange(S // SEL_BLOCK) * SEL_BLOCK
    start = n * CMP_STRIDE
    ovl = ((start[None, :] < sel_start[:, None] + SEL_BLOCK) & (start[None, :] + CMP_BLOCK > sel_start[:, None])
           & (n[None, :] >= 0) & (n[None, :] < (S - CMP_BLOCK) // CMP_STRIDE + 1))
    return gcd, c31, jnp.asarray(ovl, BF16)


def nsa_mixer_pallas(proj, cmp_pe, cmp_w1, cmp_w2, q_gain, k_gain, rel_tbl, bias, B, S):
    dh = HEAD_DIM
    c0 = COL["n_kv"]
    xc = proj[:, c0:c0 + 2 * NSA_KV_HEADS * dh].reshape(B, S // CMP_STRIDE, CMP_STRIDE, 2, NSA_KV_HEADS, dh)
    xc = xc.transpose(0, 3, 4, 1, 2, 5).reshape(B, 2, NSA_KV_HEADS, S // CMP_STRIDE, CMP_STRIDE * dh)
    kc, vct = nsa_compress(xc, cmp_pe.reshape(2, 1, CMP_BLOCK * dh), cmp_w1.astype(BF16),
                           cmp_w2.astype(BF16), k_gain, S)
    ks, vst, kw, vwt = nsa_prep(proj, k_gain, B, S)
    gcd, c31, ovl = nsa_tables(rel_tbl, S)
    return nsa_attention(proj, kc, vct, ks, vst, kw, vwt, q_gain[None], bias, gcd, c31, ovl, B, S)


def rel_bucket(dist):
    n = jnp.maximum(dist, 0)
    max_exact = N_BUCKETS // 2
    nf = jnp.maximum(n, 1).astype(jnp.float32)
    large = max_exact + (jnp.log(nf / max_exact) / math.log(MAX_DISTANCE / max_exact)
                         * (N_BUCKETS - max_exact)).astype(jnp.int32)
    return jnp.where(n < max_exact, n, jnp.minimum(large, N_BUCKETS - 1))


CONV_HALO = 8


def _conv_body(b_ref, c_ref, x_ref, w_ref, o_ref, prev_scr):
    @pl.when(pl.program_id(1) == 0)
    def _():
        prev_scr[...] = jnp.zeros(prev_scr.shape, F32)

    bx = c_ref[...] * x_ref[...]
    ts = bx.shape[0]
    row = lax.broadcasted_iota(jnp.int32, bx.shape, 0)
    prev = prev_scr[...]
    last1 = prev[CONV_HALO - 1:CONV_HALO]
    last2 = prev[CONV_HALO - 2:CONV_HALO - 1]
    back1 = jnp.where(row == 0, last1, pltpu.roll(bx, 1, 0))
    back2 = jnp.where(row == 0, last2, jnp.where(row == 1, last1, pltpu.roll(bx, 2, 0)))
    w = w_ref[...]
    o_ref[...] = b_ref[...] * (w[0:1] * back2 + w[1:2] * back1 + w[2:3] * bx)
    prev_scr[...] = bx[ts - CONV_HALO:ts]


def conv_mixer(proj, conv_w, B, S, ts=512):
    nt = S // ts
    col = lambda name: pl.BlockSpec((ts, MIX_W), lambda b, t: (b * nt + t, COL[name] // MIX_W))
    return pl.pallas_call(
        _conv_body,
        grid=(B, nt),
        in_specs=[col("a_b"), col("a_c"), col("a_x"), pl.BlockSpec((CONV_K, MIX_W), lambda b, t: (0, 0))],
        out_specs=pl.BlockSpec((ts, MIX_W), lambda b, t: (b * nt + t, 0)),
        out_shape=jax.ShapeDtypeStruct((B * S, MIX_W), F32),
        scratch_shapes=[pltpu.VMEM((CONV_HALO, MIX_W), F32)],
        compiler_params=_cparams(("parallel", "arbitrary")),
        name="conv_mixer",
    )(proj, proj, proj, conv_w)


def _sgu_body(uv_ref, g_ref, w_ref, b_ref, o_ref):
    uv = jax.nn.gelu(uv_ref[...])
    u = uv[:, :MIX_W]
    v = uv[:, MIX_W:]
    v = v - jnp.mean(v, axis=-1, keepdims=True)
    v = (v * lax.rsqrt(jnp.mean(v * v, axis=-1, keepdims=True) + EPS) * g_ref[...]).astype(BF16)
    gw = MIX_W // SGU_GROUPS
    group = lax.broadcasted_iota(jnp.int32, (SGU_CHUNK, MIX_W), 1) >> (gw.bit_length() - 1)
    for c in range(uv.shape[0] // SGU_CHUNK):
        rows = slice(c * SGU_CHUNK, (c + 1) * SGU_CHUNK)
        vc = v[rows]
        s = b_ref[...]
        for g in range(SGU_GROUPS):
            s = s + jnp.dot(w_ref[g], jnp.where(group == g, vc, jnp.zeros_like(vc)),
                            preferred_element_type=F32)
        o_ref[rows, :] = u[rows] * s


def sgu_mixer_pallas(proj, ln_gain, w_s, b_s, T, tm=512):
    tri = jnp.tril(jnp.ones((SGU_CHUNK, SGU_CHUNK), dtype=bool))
    w = jnp.where(tri[None], w_s, 0).astype(BF16)
    b = jnp.repeat(b_s.T, MIX_W // SGU_GROUPS, axis=1)
    const = lambda *shape: pl.BlockSpec(shape, lambda i: (0,) * len(shape), pipeline_mode=pl.Buffered(1))
    return pl.pallas_call(
        _sgu_body,
        grid=(T // tm,),
        in_specs=[pl.BlockSpec((tm, 2 * MIX_W), lambda i: (i, COL["c_uv"] // (2 * MIX_W))),
                  const(1, MIX_W), const(SGU_GROUPS, SGU_CHUNK, SGU_CHUNK), const(SGU_CHUNK, MIX_W)],
        out_specs=pl.BlockSpec((tm, MIX_W), lambda i: (i, 0)),
        out_shape=jax.ShapeDtypeStruct((T, MIX_W), F32),
        compiler_params=_cparams(("parallel",)),
        name="sgu_mixer",
    )(proj, ln_gain[None], w, b)


def _layer(x2, p2, B, S, rel_bias, g_mix, w_in, conv_w, nsa_cmp_pe, nsa_cmp_w1, nsa_cmp_w2,
           nsa_q_gain, nsa_k_gain, sgu_ln_gain, sgu_w, sgu_b, dsa_q_gain, dsa_k_gain,
           w_gate, w_branch, w_out, g_ffn, peer_wq, peer_subkeys, peer_u, peer_v,
           g_ple, w_ple_gate, w_ple_proj):
    T = B * S
    proj = rms_matmul(x2, g_mix[None], pack_w_in(w_in), tm=1024, tn=768)

    y_a = conv_mixer(proj, conv_w, B, S)
    y_b = nsa_mixer_pallas(proj, nsa_cmp_pe, nsa_cmp_w1, nsa_cmp_w2, nsa_q_gain, nsa_k_gain,
                           rel_bias[:, :NSA_HEADS], rel_bias_tiles(rel_bias[:, :NSA_HEADS]), B, S)
    y_c = sgu_mixer_pallas(proj, sgu_ln_gain, sgu_w, sgu_b, T)
    y_d = dsa_attention(proj, COL, dsa_q_gain[None], dsa_k_gain[None],
                        rel_bias_tiles(rel_bias[:, NSA_HEADS:]), B, S)
    ys = [y_a, y_b, y_c, y_d]

    x2 = merge_mixers(x2, g_mix[None], ys, w_gate.astype(BF16),
                      w_branch.reshape(N_MIXERS, MIX_W, D_MODEL).astype(BF16),
                      w_out.astype(BF16), tm=256)

    sk = peer_subkeys.reshape(2 * PEER_HEADS, PEER_KEYS, PEER_QDIM // 2).astype(BF16)
    x2 = peer_ffn_update(x2, g_ffn[None], peer_wq.astype(BF16), sk,
                         peer_u.astype(BF16), peer_v.astype(BF16).T)

    x2 = ple_update(x2, g_ple[None], p2, w_ple_gate.astype(BF16), w_ple_proj.astype(BF16), tm=512)
    return x2


def kernel(x, p, rel_bias, g_mix, w_in, conv_w, nsa_cmp_pe, nsa_cmp_w1, nsa_cmp_w2, nsa_q_gain,
           nsa_k_gain, sgu_ln_gain, sgu_w, sgu_b, dsa_q_gain, dsa_k_gain, w_gate, w_branch, w_out,
           g_ffn, peer_wq, peer_subkeys, peer_u, peer_v, g_ple, w_ple_gate, w_ple_proj):
    B, S, D = x.shape
    depth = p.shape[0]
    x2 = x.reshape(B * S, D)
    for l in range(depth):
        x2 = _layer(x2, p[l].reshape(B * S, PLE_DIM), B, S, rel_bias, g_mix[l], w_in[l], conv_w[l],
                    nsa_cmp_pe[l], nsa_cmp_w1[l], nsa_cmp_w2[l], nsa_q_gain[l], nsa_k_gain[l],
                    sgu_ln_gain[l], sgu_w[l], sgu_b[l], dsa_q_gain[l], dsa_k_gain[l],
                    w_gate[l], w_branch[l], w_out[l], g_ffn[l], peer_wq[l], peer_subkeys[l],
                    peer_u[l], peer_v[l], g_ple[l], w_ple_gate[l], w_ple_proj[l])
    return x2.reshape(B, S, D)
```

```python
import functools
import math

import jax
import jax.numpy as jnp
import numpy as np
from jax import lax
from jax.experimental import pallas as pl
from jax.experimental.pallas import tpu as pltpu

F32 = jnp.float32
BF16 = jnp.bfloat16

D_MODEL = 1024
HEAD_DIM = 64
N_MIXERS = 4
MIX_W = D_MODEL // 2
Q_BLOCK = 128
EPS = 1e-6
NEG_INF = -1e30
CONV_K = 3
NSA_HEADS = MIX_W // HEAD_DIM
NSA_KV_HEADS = 2
CMP_BLOCK = 32
CMP_STRIDE = 16
CMP_HIDDEN = 256
SEL_BLOCK = 64
SEL_TOPN = 8
WINDOW = 512
FORCED_SCORE = 1e4
SGU_CHUNK = 128
SGU_GROUPS = 8
DSA_HEADS = MIX_W // HEAD_DIM
IDX_HEADS = 8
IDX_DIM = 64
DSA_TOPK_MAX = 256
N_BUCKETS = 32
MAX_DISTANCE = 1024
PEER_HEADS = 8
PEER_KEYS = 128
PEER_QDIM = 128
PEER_TOPK = 16
N_EXPERTS = PEER_KEYS * PEER_KEYS
PLE_DIM = 256

SPLIT_WIDTHS = (
    MIX_W, MIX_W, MIX_W,
    NSA_HEADS * HEAD_DIM,
    6 * NSA_KV_HEADS * HEAD_DIM,
    3 * NSA_HEADS,
    2 * MIX_W,
    DSA_HEADS * HEAD_DIM, HEAD_DIM, HEAD_DIM,
    IDX_HEADS * IDX_DIM, IDX_DIM, IDX_HEADS,
)
IN_WIDTH = sum(SPLIT_WIDTHS)

COL = dict(a_b=0, a_c=512, a_x=1024, n_q=1536, d_q=2048, d_qi=2560, c_uv=3072, n_kv=4096,
           n_g=4864, d_kv=4992, d_kiw=5120)
N_PACK = 5376


def pack_w_in(w):
    o = dict(zip(("a_b", "a_c", "a_x", "n_q", "n_kv", "n_g", "c_uv", "d_q", "d_k", "d_v", "d_qi", "d_ki",
                  "d_wi", "end"), [0] + [int(c) for c in np.cumsum(SPLIT_WIDTHS)]))
    z = lambda n: jnp.zeros((w.shape[0], n), w.dtype)
    cols = [w[:, o["a_b"]:o["n_kv"]], w[:, o["d_q"]:o["d_k"]], w[:, o["d_qi"]:o["d_ki"]],
            w[:, o["c_uv"]:o["d_q"]], w[:, o["n_kv"]:o["n_g"]],
            w[:, o["n_g"]:o["c_uv"]], z(128 - 3 * NSA_HEADS),
            w[:, o["d_k"]:o["d_qi"]],
            w[:, o["d_ki"]:o["end"]], z(128 - IDX_DIM - IDX_HEADS)]
    packed = jnp.concatenate(cols, axis=1)
    return jnp.pad(packed, ((0, 0), (0, N_PACK - packed.shape[1]))).astype(BF16)

VMEM_LIMIT_BYTES = 56 * 1024 * 1024


def _cparams(sem):
    return pltpu.CompilerParams(dimension_semantics=sem, vmem_limit_bytes=VMEM_LIMIT_BYTES)


def _rms(x, g):
    return x * lax.rsqrt(jnp.mean(x * x, axis=-1, keepdims=True) + EPS) * g


def _rms_matmul_body(x_ref, g_ref, w_ref, o_ref, h_ref):
    @pl.when(pl.program_id(1) == 0)
    def _():
        h_ref[...] = _rms(x_ref[...], g_ref[...]).astype(BF16)

    o_ref[...] = jnp.dot(h_ref[...], w_ref[...], preferred_element_type=F32)


def rms_matmul(x, g, w, tm, tn):
    T, D = x.shape
    N = w.shape[1]
    return pl.pallas_call(
        _rms_matmul_body,
        grid=(T // tm, N // tn),
        in_specs=[pl.BlockSpec((tm, D), lambda i, j: (i, 0)),
                  pl.BlockSpec((1, D), lambda i, j: (0, 0)),
                  pl.BlockSpec((D, tn), lambda i, j: (0, j))],
        out_specs=pl.BlockSpec((tm, tn), lambda i, j: (i, j)),
        out_shape=jax.ShapeDtypeStruct((T, N), F32),
        scratch_shapes=[pltpu.VMEM((tm, D), BF16)],
        compiler_params=_cparams(("parallel", "arbitrary")),
        name="rms_matmul",
    )(x, g, w)


def _merge_body(x_ref, g_ref, ya_ref, yb_ref, yc_ref, yd_ref, wg_ref, wb_ref, wo_ref, o_ref):
    x = x_ref[...]
    h = _rms(x, g_ref[...]).astype(BF16)
    merged = jnp.zeros(x.shape, F32)
    for m, y_ref in enumerate((ya_ref, yb_ref, yc_ref, yd_ref)):
        z = jnp.dot(y_ref[...].astype(BF16), wb_ref[m], preferred_element_type=F32)
        gate = jax.nn.sigmoid(jnp.dot(h, wg_ref[:, m * D_MODEL:(m + 1) * D_MODEL],
                                      preferred_element_type=F32))
        merged = merged + gate * z
    o_ref[...] = x + jnp.dot(merged.astype(BF16), wo_ref[...], preferred_element_type=F32)


def merge_mixers(x, g, ys, w_gate, w_branch, w_out, tm):
    T, D = x.shape
    const = lambda *shape: pl.BlockSpec(shape, lambda i: (0,) * len(shape), pipeline_mode=pl.Buffered(1))
    return pl.pallas_call(
        _merge_body,
        grid=(T // tm,),
        in_specs=[pl.BlockSpec((tm, D), lambda i: (i, 0)),
                  const(1, D)]
                 + [pl.BlockSpec((tm, MIX_W), lambda i: (i, 0)) for _ in range(N_MIXERS)]
                 + [const(D, N_MIXERS * D), const(N_MIXERS, MIX_W, D), const(D, D)],
        out_specs=pl.BlockSpec((tm, D), lambda i: (i, 0)),
        out_shape=jax.ShapeDtypeStruct((T, D), F32),
        compiler_params=_cparams(("parallel",)),
        name="merge_mixers",
    )(x, g, *ys, w_gate, w_branch, w_out)


def _ple_body(x_ref, g_ref, p_ref, wg_ref, wp_ref, o_ref):
    x = x_ref[...]
    h = _rms(x, g_ref[...]).astype(BF16)
    gate = jax.nn.sigmoid(jnp.dot(h, wg_ref[...], preferred_element_type=F32))
    proj = jnp.dot(p_ref[...].astype(BF16), wp_ref[...], preferred_element_type=F32)
    o_ref[...] = x + gate * proj


def ple_update(x, g, p, w_gate, w_proj, tm):
    T, D = x.shape
    const = lambda *shape: pl.BlockSpec(shape, lambda i: (0,) * len(shape), pipeline_mode=pl.Buffered(1))
    return pl.pallas_call(
        _ple_body,
        grid=(T // tm,),
        in_specs=[pl.BlockSpec((tm, D), lambda i: (i, 0)), const(1, D),
                  pl.BlockSpec((tm, PLE_DIM), lambda i: (i, 0)),
                  const(D, D), const(PLE_DIM, D)],
        out_specs=pl.BlockSpec((tm, D), lambda i: (i, 0)),
        out_shape=jax.ShapeDtypeStruct((T, D), F32),
        compiler_params=_cparams(("parallel",)),
        name="ple_update",
    )(x, g, p, w_gate, w_proj)


TAKEN = -3e38
NO_PRIORITY = 1e9


def _extract_top16(cur, prio, tops_ref, one_per_round):
    rank = jnp.full(cur.shape, float(PEER_TOPK), F32)
    for r in range(PEER_TOPK):
        m = jnp.max(cur, axis=0, keepdims=True)
        if tops_ref is not None:
            tops_ref[r:r + 1, :] = m
        hit = cur == m
        if one_per_round:
            hit = prio == jnp.min(jnp.where(hit, prio, NO_PRIORITY), axis=0, keepdims=True)
        rank = jnp.where(hit, float(r), rank)
        cur = jnp.where(hit, TAKEN, cur)
    return rank


_CAND_ROWS_J = (16, 8, 5, 4, 3, 2, 2, 2)


def _peer_topk_body(x_ref, g_ref, wq_ref, sk_ref, hn_ref, cnt_ref, e1_ref, rank_ref, e2_ref, taken_ref,
                    t1_ref, t2_ref, j_ref, *, one_per_round):
    hn = _rms(x_ref[...], g_ref[...]).astype(BF16)
    hn_ref[...] = hn
    q = jnp.dot(hn, wq_ref[...], preferred_element_type=F32).astype(BF16)
    half = PEER_QDIM // 2
    tn = x_ref.shape[0]
    key_prio = lax.broadcasted_iota(jnp.int32, (PEER_KEYS, LANE_CHUNK), 0).astype(F32)
    j8 = lax.broadcasted_iota(jnp.int32, (8, LANE_CHUNK), 0)
    j16 = lax.broadcasted_iota(jnp.int32, (PEER_TOPK, LANE_CHUNK), 0)
    cand_prio = jnp.concatenate([j16] + [j8 + PEER_TOPK * i for i in range(1, 8)]
                                + [(j8 + 8) * PEER_TOPK], axis=0).astype(F32)
    n_max = jnp.zeros((1, LANE_CHUNK), F32)
    for h in range(PEER_HEADS):
        scores = []
        for p in range(2):
            c0 = (2 * h + p) * half
            scores.append(lax.dot_general(sk_ref[2 * h + p], q[:, c0:c0 + half],
                                          (((1,), (1,)), ((), ())), preferred_element_type=F32))
        for c in range(tn // LANE_CHUNK):
            ls = slice(c * LANE_CHUNK, (c + 1) * LANE_CHUNK)
            s1 = scores[0][:, ls]
            s2 = scores[1][:, ls]
            rank1 = _extract_top16(s1, key_prio, t1_ref, one_per_round)
            rank2 = _extract_top16(s2, key_prio, t2_ref, one_per_round)
            a = t1_ref[...]
            b = t2_ref[...]
            b8 = b[0:8]
            pieces = [a[0:1] + b]
            for i in range(1, 8):
                cand_i = a[i:i + 1] + b8
                if _CAND_ROWS_J[i] < 8:
                    cand_i = jnp.where(j8 < _CAND_ROWS_J[i], cand_i, NEG_INF)
                pieces.append(cand_i)
            pieces.append(a[8:16] + b[0:1])
            cand = jnp.concatenate(pieces, axis=0)
            taken = _extract_top16(cand, cand_prio, None, one_per_round) < PEER_TOPK
            top = a[0:1] + b[0:1]
            z = jnp.sum(jnp.where(taken, jnp.exp(cand - top), 0.0), axis=0, keepdims=True)
            one = jnp.where(taken, 1.0, 0.0)
            j_ref[0:1, :] = jnp.sum(one[0:PEER_TOPK], axis=0, keepdims=True)
            for i in range(1, 8):
                j_ref[i:i + 1, :] = jnp.sum(one[8 + 8 * i:16 + 8 * i], axis=0, keepdims=True)
            j_ref[8:PEER_TOPK, :] = one[9 * 8:10 * 8]
            counts = j_ref[...]
            cnt = jnp.zeros(s1.shape, F32)
            for r in range(PEER_TOPK):
                cnt = jnp.where(rank1 == float(r), counts[r:r + 1], cnt)
            in1 = rank1 < PEER_TOPK
            in2 = rank2 < PEER_TOPK
            cnt_ref[h, :, ls] = cnt
            rank_ref[h, :, ls] = rank2
            e1_ref[h, :, ls] = jnp.where(in1, jnp.exp(s1 - (a[0:1] + jnp.log(z))), 0.0)
            e2_ref[h, :, ls] = jnp.where(in2, jnp.exp(s2 - b[0:1]), 0.0)
            n_max = jnp.maximum(n_max, jnp.sum(jnp.where(in1, 1.0, 0.0), axis=0, keepdims=True))
            n_max = jnp.maximum(n_max, jnp.sum(jnp.where(in2, 1.0, 0.0), axis=0, keepdims=True))
            n_max = jnp.maximum(n_max, jnp.sum(counts, axis=0, keepdims=True))
    taken_ref[0] = n_max


def peer_topk(x, g, wq, subkeys, tn, one_per_round):
    T, D = x.shape
    const = lambda *shape: pl.BlockSpec(shape, lambda i: (0,) * len(shape), pipeline_mode=pl.Buffered(1))
    return pl.pallas_call(
        functools.partial(_peer_topk_body, one_per_round=one_per_round),
        grid=(T // tn,),
        in_specs=[pl.BlockSpec((tn, D), lambda i: (i, 0)), const(1, D), const(D, D),
                  const(2 * PEER_HEADS, PEER_KEYS, PEER_QDIM // 2)],
        out_specs=[pl.BlockSpec((tn, D), lambda i: (i, 0))]
                  + [pl.BlockSpec((PEER_HEADS, PEER_KEYS, tn), lambda i: (0, 0, i))] * 4
                  + [pl.BlockSpec((1, 1, LANE_CHUNK), lambda i: (i, 0, 0))],
        out_shape=[jax.ShapeDtypeStruct((T, D), BF16)]
                  + [jax.ShapeDtypeStruct((PEER_HEADS, PEER_KEYS, T), F32)] * 4
                  + [jax.ShapeDtypeStruct((T // tn, 1, LANE_CHUNK), F32)],
        scratch_shapes=[pltpu.VMEM((PEER_TOPK, LANE_CHUNK), F32)] * 3,
        compiler_params=_cparams(("parallel",)),
        name="peer_topk_exact" if one_per_round else "peer_topk",
    )(x, g, wq, subkeys)


LANE_CHUNK = 128


PEER_I1_STEP = 8


def _peer_main_body(hn_ref, cnt_ref, e1_ref, rank_ref, e2_ref, u_ref, vt_prev_ref, vt_ref, x_ref, o_ref,
                    acc_ref, p_ref):
    j = pl.program_id(1)
    tn = hn_ref.shape[0]
    slot = j % 2

    @pl.when(j == 0)
    def _():
        acc_ref[...] = jnp.zeros(acc_ref.shape, F32)
        p_ref[1] = jnp.zeros(p_ref.shape[1:], BF16)

    hn = hn_ref[...]
    for k in range(PEER_I1_STEP):
        rows = slice(k * PEER_KEYS, (k + 1) * PEER_KEYS)
        at = lax.dot_general(u_ref[rows, :], hn, (((1,), (1,)), ((), ())),
                             preferred_element_type=F32)
        for c in range(tn // LANE_CHUNK):
            ls = slice(c * LANE_CHUNK, (c + 1) * LANE_CHUNK)
            w = jnp.zeros((PEER_KEYS, LANE_CHUNK), F32)
            for h in range(PEER_HEADS):
                picked = jnp.where(rank_ref[h, :, ls] < cnt_ref[h, k:k + 1, ls], e2_ref[h, :, ls], 0.0)
                w = w + picked * e1_ref[h, k:k + 1, ls]
            p_ref[slot, rows, ls] = (w * jax.nn.gelu(at[:, ls])).astype(BF16)

    acc_ref[...] += jnp.dot(vt_prev_ref[...], p_ref[1 - slot], preferred_element_type=F32)

    @pl.when(j == pl.num_programs(1) - 1)
    def _():
        acc = acc_ref[...] + jnp.dot(vt_ref[...], p_ref[slot], preferred_element_type=F32)
        o_ref[...] = x_ref[...] + acc.T


def peer_main(hn, cnt, e1, rank2, e2, u, vt, x, tn):
    T, D = x.shape
    e_blk = PEER_I1_STEP * PEER_KEYS
    prev = lambda j: jnp.maximum(j - 1, 0)
    per_i1 = pl.BlockSpec((PEER_HEADS, PEER_I1_STEP, tn), lambda i, j: (0, j, i))
    once = pl.Buffered(1)
    per_i2 = pl.BlockSpec((PEER_HEADS, PEER_KEYS, tn), lambda i, j: (0, 0, i), pipeline_mode=once)
    return pl.pallas_call(
        _peer_main_body,
        grid=(T // tn, PEER_KEYS // PEER_I1_STEP),
        in_specs=[pl.BlockSpec((tn, D), lambda i, j: (i, 0), pipeline_mode=once),
                  per_i1, per_i1, per_i2, per_i2,
                  pl.BlockSpec((e_blk, D), lambda i, j: (j, 0)),
                  pl.BlockSpec((D, e_blk), lambda i, j: (0, prev(j))),
                  pl.BlockSpec((D, e_blk), lambda i, j: (0, j)),
                  pl.BlockSpec((tn, D), lambda i, j: (i, 0), pipeline_mode=once)],
        out_specs=pl.BlockSpec((tn, D), lambda i, j: (i, 0)),
        out_shape=jax.ShapeDtypeStruct((T, D), F32),
        scratch_shapes=[pltpu.VMEM((D, tn), F32), pltpu.VMEM((2, e_blk, tn), BF16)],
        compiler_params=_cparams(("parallel", "arbitrary")),
        name="peer_main",
    )(hn, cnt, e1, rank2, e2, u, vt, vt, x)


def peer_ffn_update(x, g, wq, subkeys, u, vt):
    *fast, most_taken = peer_topk(x, g, wq, subkeys, tn=256, one_per_round=False)
    hn, cnt, e1, rank2, e2 = lax.cond(
        jnp.max(most_taken) > PEER_TOPK,
        lambda: tuple(peer_topk(x, g, wq, subkeys, tn=256, one_per_round=True)[:-1]),
        lambda: tuple(fast))
    return peer_main(hn, cnt, e1, rank2, e2, u, vt, x, tn=1024)


BIAS_TILES = 9
LOG2E = math.log2(math.e)
KEY_TILES = 4
KEY_BLOCK = KEY_TILES * Q_BLOCK
WIN_TILES = WINDOW // Q_BLOCK + 1


def rel_bias_tiles(tbl):
    H = tbl.shape[1]
    d = jnp.arange(BIAS_TILES)[:, None, None]
    s = jnp.arange(Q_BLOCK)[None, :, None]
    t = jnp.arange(Q_BLOCK)[None, None, :]
    bias = tbl[rel_bucket(d * Q_BLOCK + t - s)] * LOG2E
    return jnp.moveaxis(bias, -1, 2).reshape(BIAS_TILES, Q_BLOCK, H * Q_BLOCK)


INT_MIN = -2 ** 31
NEG_INF_KEY = -1900671691


def _sortable_key(x):
    bits = pltpu.bitcast(x, jnp.int32)
    return jnp.where(bits < 0, bits ^ jnp.int32(0x7FFFFFFF), bits)


def _head_rms(x, g, scale):
    return x * lax.rsqrt(jnp.mean(x * x, axis=-1, keepdims=True) + EPS) * g * scale


def _dsa_body(kv_ref, kiw_ref, q_ref, qi_ref, wq_ref, qg_ref, kg_ref, bias_ref, o_ref,
              kn_scr, ki_scr, vt_scr, qall_scr, qiall_scr, sc_scr, th_scr,
              m_scr, l_scr, acc_scr, *, top_k):
    i = pl.program_id(1)
    dh = HEAD_DIM
    QB = Q_BLOCK

    @pl.when(i == 0)
    def _prep():
        kv = kv_ref[...]
        kn_scr[...] = _head_rms(kv[:, :dh], kg_ref[...], 1.0).astype(BF16)
        vt_scr[...] = kv[:, dh:].T.astype(BF16)
        ki_scr[...] = kiw_ref[:, :dh].astype(BF16)

    q = q_ref[...]
    qi = qi_ref[...]
    for h in range(DSA_HEADS):
        qh = _head_rms(q[:, h * dh:(h + 1) * dh], qg_ref[...], dh ** -0.5 * LOG2E)
        qall_scr[h * QB:(h + 1) * QB, :] = qh.astype(BF16)
        qiall_scr[h * QB:(h + 1) * QB, :] = qi[:, h * dh:(h + 1) * dh].astype(BF16)
    w_t = (wq_ref[:, dh:dh + IDX_HEADS] * (IDX_HEADS ** -0.5)).T

    s_loc = lax.broadcasted_iota(jnp.int32, (QB, QB), 0)
    t_loc = lax.broadcasted_iota(jnp.int32, (QB, QB), 1)
    n_tiles = i + 1
    n_blk = (i + KEY_TILES) // KEY_TILES
    nt_dims = (((1,), (1,)), ((), ()))

    def score_block(kb, carry):
        rows = pl.ds(pl.multiple_of(kb * KEY_BLOCK, KEY_BLOCK), KEY_BLOCK)
        kib = ki_scr[rows, :]
        sc = jnp.zeros((KEY_BLOCK, QB), F32)
        for h in range(IDX_HEADS):
            r = lax.dot_general(kib, qiall_scr[h * QB:(h + 1) * QB, :], nt_dims,
                                preferred_element_type=F32)
            sc = sc + w_t[h:h + 1, :] * jnp.maximum(r, 0.0)
        sc = jnp.where(sc == 0.0, 0.0, sc)
        s_pos = kb * KEY_BLOCK + lax.broadcasted_iota(jnp.int32, (KEY_BLOCK, QB), 0)
        t_pos = i * QB + lax.broadcasted_iota(jnp.int32, (KEY_BLOCK, QB), 1)
        sc_scr[rows, :] = _sortable_key(jnp.where(s_pos <= t_pos, sc, NEG_INF))
        return carry

    lax.fori_loop(0, n_blk, score_block, 0)

    def count_ge(cand):
        def body(kb, c):
            blk = sc_scr[pl.ds(pl.multiple_of(kb * KEY_BLOCK, KEY_BLOCK), KEY_BLOCK), :]
            return c + jnp.sum((blk >= cand).astype(jnp.int32), axis=0, keepdims=True)
        return lax.fori_loop(0, n_blk, body, jnp.zeros((1, QB), jnp.int32))

    th_scr[...] = jnp.full((1, QB), NEG_INF_KEY + 1, jnp.int32)

    @pl.when(n_tiles * QB > top_k)
    def _select():
        def bit_body(it, carry):
            ans_u, cnt_ans = carry
            cand_u = ans_u | (jnp.int32(1) << (31 - it))
            c = count_ge(cand_u ^ jnp.int32(INT_MIN))
            ok = c >= top_k
            return jnp.where(ok, cand_u, ans_u), jnp.where(ok, c, cnt_ans)

        ans_u, cnt = lax.fori_loop(
            0, 32, bit_body,
            (jnp.zeros((1, QB), jnp.int32), jnp.full((1, QB), 1, jnp.int32) * (n_blk * KEY_BLOCK)))
        th_scr[...] = ans_u ^ jnp.int32(INT_MIN)

        @pl.when(jnp.max(cnt) > top_k)
        def _ties():
            theta = th_scr[...]
            need = (top_k - count_ge(theta + 1)).astype(F32)
            tri = (s_loc >= t_loc).astype(BF16)

            def body(kt, seen):
                rows = pl.ds(pl.multiple_of(kt * QB, QB), QB)
                tile = sc_scr[rows, :]
                eq = tile == theta
                rank = seen + jnp.dot(tri, eq.astype(BF16), preferred_element_type=F32)
                sc_scr[rows, :] = jnp.where(eq & (rank > need), theta - 1, tile)
                return seen + jnp.sum(eq.astype(F32), axis=0, keepdims=True)

            lax.fori_loop(0, n_tiles, body, jnp.zeros((1, QB), F32))

    theta = th_scr[...]
    m_scr[...] = jnp.full(m_scr.shape, NEG_INF, F32)
    l_scr[...] = jnp.zeros(l_scr.shape, F32)
    acc_scr[...] = jnp.zeros(acc_scr.shape, F32)

    def att_block(kb, carry):
        row0 = pl.multiple_of(kb * KEY_BLOCK, KEY_BLOCK)
        rows = pl.ds(row0, KEY_BLOCK)
        masks = [sc_scr[pl.ds(row0 + j * QB, QB), :] >= theta for j in range(KEY_TILES)]
        _attend_block(kn_scr[rows, :], vt_scr[:, rows], qall_scr, bias_ref, i - kb * KEY_TILES,
                      masks, m_scr, l_scr, acc_scr)
        return carry

    lax.fori_loop(0, n_blk, att_block, 0)

    o_t = acc_scr[...] / jnp.maximum(l_scr[...], 1e-30)
    o_ref[...] = jnp.concatenate([o_t[:, h * QB:(h + 1) * QB].T for h in range(DSA_HEADS)], axis=1)


def dsa_attention(proj, col, q_gain, k_gain, bias, B, S):
    nq = S // Q_BLOCK
    top_k = min(DSA_TOPK_MAX, S // 4)
    HQ = DSA_HEADS * Q_BLOCK
    const = lambda *shape: pl.BlockSpec(shape, lambda b, i: (0,) * len(shape), pipeline_mode=pl.Buffered(1))
    return pl.pallas_call(
        functools.partial(_dsa_body, top_k=top_k),
        grid=(B, nq),
        in_specs=[pl.BlockSpec((S, 128), lambda b, i: (b, col["d_kv"] // 128)),
                  pl.BlockSpec((S, 128), lambda b, i: (b, col["d_kiw"] // 128)),
                  pl.BlockSpec((Q_BLOCK, MIX_W), lambda b, i: (b * nq + i, col["d_q"] // MIX_W)),
                  pl.BlockSpec((Q_BLOCK, MIX_W), lambda b, i: (b * nq + i, col["d_qi"] // MIX_W)),
                  pl.BlockSpec((Q_BLOCK, 128), lambda b, i: (b * nq + i, col["d_kiw"] // 128)),
                  const(1, HEAD_DIM), const(1, HEAD_DIM), const(BIAS_TILES, Q_BLOCK, HQ)],
        out_specs=pl.BlockSpec((Q_BLOCK, MIX_W), lambda b, i: (b * nq + i, 0)),
        out_shape=jax.ShapeDtypeStruct((B * S, MIX_W), F32),
        scratch_shapes=[pltpu.VMEM((S, HEAD_DIM), BF16), pltpu.VMEM((S, HEAD_DIM), BF16),
                        pltpu.VMEM((HEAD_DIM, S), BF16),
                        pltpu.VMEM((HQ, HEAD_DIM), BF16), pltpu.VMEM((HQ, HEAD_DIM), BF16),
                        pltpu.VMEM((S, Q_BLOCK), jnp.int32),
                        pltpu.VMEM((1, Q_BLOCK), jnp.int32),
                        pltpu.VMEM((1, HQ), F32), pltpu.VMEM((1, HQ), F32), pltpu.VMEM((HEAD_DIM, HQ), F32)],
        compiler_params=_cparams(("parallel", "arbitrary")),
        name="dsa_attention",
    )(proj, proj, proj, proj, proj, q_gain, k_gain, bias)


NSA_G = NSA_HEADS // NSA_KV_HEADS
CMP_PAD = 56
CMP_NEAR = 64
SEL_TAKEN = -3e38


def _cmp_rows(S):
    return -(-(CMP_PAD + S // CMP_STRIDE) // 64) * 64


def _nsa_prep_body(ks_ref, vs_ref, kw_ref, vw_ref, kg_ref, kso_ref, vso_ref, kwo_ref, vwo_ref):
    dh = HEAD_DIM
    for g in range(NSA_KV_HEADS):
        cs = slice(g * dh, (g + 1) * dh)
        kso_ref[0, g] = _head_rms(ks_ref[:, cs], kg_ref[1:2, :], 1.0).astype(BF16)
        kwo_ref[0, g] = _head_rms(kw_ref[:, cs], kg_ref[2:3, :], 1.0).astype(BF16)
        vso_ref[0, g] = vs_ref[:, cs].T.astype(BF16)
        vwo_ref[0, g] = vw_ref[:, cs].T.astype(BF16)


def nsa_prep(proj, k_gain, B, S, ts=512):
    nt = S // ts
    c0 = COL["n_kv"] // 128
    col = lambda j: pl.BlockSpec((ts, 128), lambda b, t: (b * nt + t, c0 + j))
    k_out = pl.BlockSpec((1, NSA_KV_HEADS, ts, HEAD_DIM), lambda b, t: (b, 0, t, 0))
    v_out = pl.BlockSpec((1, NSA_KV_HEADS, HEAD_DIM, ts), lambda b, t: (b, 0, 0, t))
    k_sds = jax.ShapeDtypeStruct((B, NSA_KV_HEADS, S, HEAD_DIM), BF16)
    v_sds = jax.ShapeDtypeStruct((B, NSA_KV_HEADS, HEAD_DIM, S), BF16)
    return pl.pallas_call(
        _nsa_prep_body,
        grid=(B, nt),
        in_specs=[col(2), col(3), col(4), col(5), pl.BlockSpec((3, HEAD_DIM), lambda b, t: (0, 0))],
        out_specs=[k_out, v_out, k_out, v_out],
        out_shape=[k_sds, v_sds, k_sds, v_sds],
        compiler_params=_cparams(("parallel", "parallel")),
        name="nsa_prep",
    )(proj, proj, proj, proj, k_gain)


def _nsa_compress_body(xk_ref, xv_ref, pe_ref, w1_ref, w2_ref, kg_ref, kc_ref, vct_ref):
    R = xk_ref.shape[3]
    ncp = kc_ref.shape[2]
    half = CMP_STRIDE * HEAD_DIM
    row = lax.broadcasted_iota(jnp.int32, (R, HEAD_DIM), 0)
    for j, x_ref in enumerate((xk_ref, xv_ref)):
        x = x_ref[0, 0, 0]
        pe = pe_ref[j]
        lo = jnp.dot((x + pe[:, :half]).astype(BF16), w1_ref[j, :half, :], preferred_element_type=F32)
        hi = jnp.dot((x + pe[:, half:]).astype(BF16), w1_ref[j, half:, :], preferred_element_type=F32)
        hid = jax.nn.gelu(lo + pltpu.roll(hi, R - 1, 0))
        c = jnp.dot(hid.astype(BF16), w2_ref[j], preferred_element_type=F32)
        if j == 0:
            c = _head_rms(c, kg_ref[0:1, :], 1.0)
        c = jnp.where(row < R - 1, c, 0.0)
        c = jnp.concatenate([jnp.zeros((CMP_PAD, HEAD_DIM), F32), c,
                             jnp.zeros((ncp - CMP_PAD - R, HEAD_DIM), F32)], axis=0)
        if j == 0:
            kc_ref[0, 0] = c.astype(BF16)
        else:
            vct_ref[0, 0] = c.T.astype(BF16)


def nsa_compress(xc, pe, w1, w2, k_gain, S):
    B = xc.shape[0]
    R = S // CMP_STRIDE
    ncp = _cmp_rows(S)
    const = lambda *shape: pl.BlockSpec(shape, lambda b, g: (0,) * len(shape), pipeline_mode=pl.Buffered(1))
    return pl.pallas_call(
        _nsa_compress_body,
        grid=(B, NSA_KV_HEADS),
        in_specs=[pl.BlockSpec((1, 1, 1, R, CMP_STRIDE * HEAD_DIM), lambda b, g: (b, 0, g, 0, 0)),
                  pl.BlockSpec((1, 1, 1, R, CMP_STRIDE * HEAD_DIM), lambda b, g: (b, 1, g, 0, 0)),
                  const(2, 1, CMP_BLOCK * HEAD_DIM), const(2, CMP_BLOCK * HEAD_DIM, CMP_HIDDEN),
                  const(2, CMP_HIDDEN, HEAD_DIM), const(3, HEAD_DIM)],
        out_specs=[pl.BlockSpec((1, 1, ncp, HEAD_DIM), lambda b, g: (b, g, 0, 0)),
                   pl.BlockSpec((1, 1, HEAD_DIM, ncp), lambda b, g: (b, g, 0, 0))],
        out_shape=[jax.ShapeDtypeStruct((B, NSA_KV_HEADS, ncp, HEAD_DIM), BF16),
                   jax.ShapeDtypeStruct((B, NSA_KV_HEADS, HEAD_DIM, ncp), BF16)],
        compiler_params=_cparams(("parallel", "parallel")),
        name="nsa_compress",
    )(xc, xc, pe, w1, w2, k_gain)


def _attend_block(k_b, v_b, q_scr, bias_ref, d0, masks, m_scr, l_scr, acc_scr):
    QB = Q_BLOCK
    CH = 2 * QB
    n_sub = len(masks)
    masks2 = [jnp.concatenate([mk, mk], axis=1) for mk in masks]
    d = [jnp.clip(d0 - j, 0, BIAS_TILES - 1) for j in range(n_sub)]
    chunks = [slice(c * CH, (c + 1) * CH) for c in range(q_scr.shape[0] // CH)]
    logits = [lax.dot_general(k_b, q_scr[ls, :], (((1,), (1,)), ((), ())), preferred_element_type=F32)
              for ls in chunks]
    for ls, lg in zip(chunks, logits):
        parts = [jnp.where(masks2[j], lg[j * QB:(j + 1) * QB] + bias_ref[d[j], :, ls], NEG_INF)
                 for j in range(n_sub)]
        m_old = m_scr[:, ls]
        m_new = m_old
        for part in parts:
            m_new = jnp.maximum(m_new, jnp.max(part, axis=0, keepdims=True))
        probs = [jnp.exp2(part - m_new) for part in parts]
        alpha = jnp.exp2(m_old - m_new)
        l_new = alpha * l_scr[:, ls]
        for pr in probs:
            l_new = l_new + jnp.sum(pr, axis=0, keepdims=True)
        l_scr[:, ls] = l_new
        pb = jnp.concatenate([pr.astype(BF16) for pr in probs], axis=0)
        acc_scr[:, ls] = alpha * acc_scr[:, ls] + jnp.dot(v_b, pb, preferred_element_type=F32)
        m_scr[:, ls] = m_new


def _nsa_body(q_ref, gl_ref, kc_ref, vct_ref, ks_ref, vst_ref, kw_ref, vwt_ref, qg_ref, bias_ref,
              gcd_ref, c31_ref, ovl_ref, o_ref,
              qall_scr, lc_scr, sel_scr, gt_scr, ms_scr, ls_scr, as_scr, mw_scr, lw_scr, aw_scr,
              *, n_sel, top_n):
    g = pl.program_id(1)
    i = pl.program_id(2)
    dh = HEAD_DIM
    QB = Q_BLOCK
    GQ = NSA_G * QB
    ncp = kc_ref.shape[2]

    q = q_ref[...]
    for r in range(NSA_G):
        qall_scr[r * QB:(r + 1) * QB, :] = _head_rms(q[:, r * dh:(r + 1) * dh], qg_ref[...],
                                                     dh ** -0.5 * LOG2E).astype(BF16)
    gt_scr[...] = jax.nn.sigmoid(gl_ref[...]).T

    lc_scr[...] = lax.dot_general(kc_ref[0, 0], qall_scr[...], (((1,), (1,)), ((), ())),
                                  preferred_element_type=F32) + c31_ref[0]
    near = pl.ds(pl.multiple_of(i * (QB // CMP_STRIDE), 8), CMP_NEAR)
    lc_scr[near, :] = lc_scr[near, :] + gcd_ref[0]
    n_p = lax.broadcasted_iota(jnp.int32, (ncp, GQ), 0)
    t_c = i * QB + (lax.broadcasted_iota(jnp.int32, (ncp, GQ), 1) & (QB - 1))
    cmp_end = jnp.where(n_p >= CMP_PAD, (n_p - CMP_PAD) * CMP_STRIDE + (CMP_BLOCK - 1), 1 << 30)
    mask_c = cmp_end <= t_c
    lc = jnp.where(mask_c, lc_scr[...], NEG_INF)
    e = jnp.where(mask_c, jnp.exp2(lc - jnp.max(lc, axis=0, keepdims=True)), 0.0)
    pc = (e / jnp.maximum(jnp.sum(e, axis=0, keepdims=True), 1e-30)).astype(BF16)
    o_c = jnp.dot(vct_ref[0, 0], pc, preferred_element_type=F32)

    imp = jnp.zeros((n_sel, QB), F32)
    for r in range(NSA_G):
        imp = imp + jnp.dot(ovl_ref[...], pc[:, r * QB:(r + 1) * QB], preferred_element_type=F32)
    j_io = lax.broadcasted_iota(jnp.int32, (n_sel, QB), 0).astype(F32)
    cur = ((i * QB + lax.broadcasted_iota(jnp.int32, (n_sel, QB), 1))
           >> (SEL_BLOCK.bit_length() - 1)).astype(F32)
    imp = jnp.where((j_io == 0) | (j_io == cur), FORCED_SCORE, imp)
    imp = jnp.where(j_io <= cur, imp, NEG_INF)
    sel = jnp.zeros((n_sel, QB), F32)
    for _ in range(top_n):
        mx = jnp.max(imp, axis=0, keepdims=True)
        first = jnp.min(jnp.where(imp == mx, j_io, float(n_sel)), axis=0, keepdims=True)
        hit = j_io == first
        sel = jnp.where(hit, 1.0, sel)
        imp = jnp.where(hit, SEL_TAKEN, imp)
    sel_scr[...] = sel

    for m_scr, l_scr, a_scr in ((ms_scr, ls_scr, as_scr), (mw_scr, lw_scr, aw_scr)):
        m_scr[...] = jnp.full(m_scr.shape, NEG_INF, F32)
        l_scr[...] = jnp.zeros(l_scr.shape, F32)
        a_scr[...] = jnp.zeros(a_scr.shape, F32)
    s_loc = lax.broadcasted_iota(jnp.int32, (QB, QB), 0)
    t_loc = lax.broadcasted_iota(jnp.int32, (QB, QB), 1)
    blocks_per_tile = QB // SEL_BLOCK

    def slc_block(kb, carry):
        rows = pl.ds(pl.multiple_of(kb * KEY_BLOCK, KEY_BLOCK), KEY_BLOCK)
        masks = []
        for j in range(KEY_TILES):
            kt = kb * KEY_TILES + j
            picked = jnp.where(s_loc < SEL_BLOCK, sel_scr[pl.ds(kt * blocks_per_tile, 1), :],
                               sel_scr[pl.ds(kt * blocks_per_tile + 1, 1), :])
            dist = (t_loc + i * QB) - (s_loc + kt * QB)
            masks.append(jnp.where(dist >= 0, picked, 0.0) > 0.5)
        _attend_block(ks_ref[0, 0, rows, :], vst_ref[0, 0, :, rows], qall_scr, bias_ref,
                      i - kb * KEY_TILES, masks, ms_scr, ls_scr, as_scr)
        return carry

    lax.fori_loop(0, (i + KEY_TILES) // KEY_TILES, slc_block, 0)

    kt0 = jnp.maximum(i - WINDOW // QB, 0)
    rows = pl.ds(pl.multiple_of(kt0 * QB, QB), WIN_TILES * QB)
    masks = []
    for j in range(WIN_TILES):
        dist = (t_loc + i * QB) - (s_loc + (kt0 + j) * QB)
        masks.append((dist >= 0) & (dist < WINDOW))
    _attend_block(kw_ref[0, 0, rows, :], vwt_ref[0, 0, :, rows], qall_scr, bias_ref, i - kt0, masks,
                  mw_scr, lw_scr, aw_scr)

    o_s = as_scr[...] / jnp.maximum(ls_scr[...], 1e-30)
    o_w = aw_scr[...] / jnp.maximum(lw_scr[...], 1e-30)
    outs = []
    for r in range(NSA_G):
        ls = slice(r * QB, (r + 1) * QB)
        gate = lambda j: gt_scr[pl.ds(j * NSA_HEADS + g * NSA_G + r, 1), :]
        o_r = gate(0) * o_c[:, ls] + gate(1) * o_s[:, ls] + gate(2) * o_w[:, ls]
        outs.append(o_r.T)
    o_ref[...] = jnp.concatenate(outs, axis=1)


def nsa_attention(proj, kc, vct, ks, vst, kw, vwt, q_gain, bias, gcd, c31, ovl, B, S):
    nq = S // Q_BLOCK
    GQ = NSA_G * Q_BLOCK
    GW = NSA_G * HEAD_DIM
    ncp = kc.shape[2]
    n_sel = S // SEL_BLOCK
    const = lambda *shape: pl.BlockSpec(shape, lambda b, g, i: (0,) * len(shape), pipeline_mode=pl.Buffered(1))
    per_group = lambda *shape: pl.BlockSpec((1, 1) + shape, lambda b, g, i: (b, g, 0, 0))
    return pl.pallas_call(
        functools.partial(_nsa_body, n_sel=n_sel, top_n=min(SEL_TOPN, n_sel)),
        grid=(B, NSA_KV_HEADS, nq),
        in_specs=[pl.BlockSpec((Q_BLOCK, GW), lambda b, g, i: (b * nq + i, COL["n_q"] // GW + g)),
                  pl.BlockSpec((Q_BLOCK, 128), lambda b, g, i: (b * nq + i, COL["n_g"] // 128)),
                  per_group(ncp, HEAD_DIM), per_group(HEAD_DIM, ncp),
                  per_group(S, HEAD_DIM), per_group(HEAD_DIM, S),
                  per_group(S, HEAD_DIM), per_group(HEAD_DIM, S),
                  const(1, HEAD_DIM),
                  pl.BlockSpec((BIAS_TILES, Q_BLOCK, GQ), lambda b, g, i: (0, 0, g)),
                  pl.BlockSpec((1, CMP_NEAR, GQ), lambda b, g, i: (g, 0, 0)),
                  pl.BlockSpec((1, 1, GQ), lambda b, g, i: (g, 0, 0)),
                  const(n_sel, ncp)],
        out_specs=pl.BlockSpec((Q_BLOCK, GW), lambda b, g, i: (b * nq + i, g)),
        out_shape=jax.ShapeDtypeStruct((B * S, MIX_W), F32),
        scratch_shapes=[pltpu.VMEM((GQ, HEAD_DIM), BF16), pltpu.VMEM((ncp, GQ), F32),
                        pltpu.VMEM((n_sel, Q_BLOCK), F32), pltpu.VMEM((128, Q_BLOCK), F32)]
                       + [pltpu.VMEM((1, GQ), F32), pltpu.VMEM((1, GQ), F32), pltpu.VMEM((HEAD_DIM, GQ), F32)] * 2,
        compiler_params=_cparams(("parallel", "parallel", "arbitrary")),
        name="nsa_attention",
    )(proj, proj, kc, vct, ks, vst, kw, vwt, q_gain, bias, gcd, c31, ovl)


def nsa_tables(tbl, S):
    ncp = _cmp_rows(S)
    n_rel = jnp.arange(CMP_NEAR)[:, None] - CMP_PAD
    t_loc = jnp.arange(Q_BLOCK)[None, :]
    near = tbl[rel_bucket(t_loc - CMP_STRIDE * n_rel - (CMP_BLOCK - 1))] * LOG2E
    far = tbl[N_BUCKETS - 1] * LOG2E
    gcd = jnp.moveaxis(near - far, -1, 0).reshape(NSA_KV_HEADS, NSA_G, CMP_NEAR, Q_BLOCK)
    gcd = jnp.moveaxis(gcd, 1, 2).reshape(NSA_KV_HEADS, CMP_NEAR, NSA_G * Q_BLOCK)
    c31 = jnp.repeat(far, Q_BLOCK).reshape(NSA_KV_HEADS, 1, NSA_G * Q_BLOCK)
    n = np.arange(ncp) - CMP_PAD
    sel_start = np.arange(S // SEL_BLOCK) * SEL_BLOCK
    start = n * CMP_STRIDE
    ovl = ((start[None, :] < sel_start[:, None] + SEL_BLOCK) & (start[None, :] + CMP_BLOCK > sel_start[:, None])
           & (n[None, :] >= 0) & (n[None, :] < (S - CMP_BLOCK) // CMP_STRIDE + 1))
    return gcd, c31, jnp.asarray(ovl, BF16)


def nsa_mixer_pallas(proj, cmp_pe, cmp_w1, cmp_w2, q_gain, k_gain, rel_tbl, bias, B, S):
    dh = HEAD_DIM
    c0 = COL["n_kv"]
    xc = proj[:, c0:c0 + 2 * NSA_KV_HEADS * dh].reshape(B, S // CMP_STRIDE, CMP_STRIDE, 2, NSA_KV_HEADS, dh)
    xc = xc.transpose(0, 3, 4, 1, 2, 5).reshape(B, 2, NSA_KV_HEADS, S // CMP_STRIDE, CMP_STRIDE * dh)
    kc, vct = nsa_compress(xc, cmp_pe.reshape(2, 1, CMP_BLOCK * dh), cmp_w1.astype(BF16),
                           cmp_w2.astype(BF16), k_gain, S)
    ks, vst, kw, vwt = nsa_prep(proj, k_gain, B, S)
    gcd, c31, ovl = nsa_tables(rel_tbl, S)
    return nsa_attention(proj, kc, vct, ks, vst, kw, vwt, q_gain[None], bias, gcd, c31, ovl, B, S)


def rel_bucket(dist):
    n = jnp.maximum(dist, 0)
    max_exact = N_BUCKETS // 2
    nf = jnp.maximum(n, 1).astype(jnp.float32)
    large = max_exact + (jnp.log(nf / max_exact) / math.log(MAX_DISTANCE / max_exact)
                         * (N_BUCKETS - max_exact)).astype(jnp.int32)
    return jnp.where(n < max_exact, n, jnp.minimum(large, N_BUCKETS - 1))


CONV_HALO = 8


def _conv_body(b_ref, c_ref, x_ref, w_ref, o_ref, prev_scr):
    @pl.when(pl.program_id(1) == 0)
    def _():
        prev_scr[...] = jnp.zeros(prev_scr.shape, F32)

    bx = c_ref[...] * x_ref[...]
    ts = bx.shape[0]
    row = lax.broadcasted_iota(jnp.int32, bx.shape, 0)
    prev = prev_scr[...]
    last1 = prev[CONV_HALO - 1:CONV_HALO]
    last2 = prev[CONV_HALO - 2:CONV_HALO - 1]
    back1 = jnp.where(row == 0, last1, pltpu.roll(bx, 1, 0))
    back2 = jnp.where(row == 0, last2, jnp.where(row == 1, last1, pltpu.roll(bx, 2, 0)))
    w = w_ref[...]
    o_ref[...] = b_ref[...] * (w[0:1] * back2 + w[1:2] * back1 + w[2:3] * bx)
    prev_scr[...] = bx[ts - CONV_HALO:ts]


def conv_mixer(proj, conv_w, B, S, ts=512):
    nt = S // ts
    col = lambda name: pl.BlockSpec((ts, MIX_W), lambda b, t: (b * nt + t, COL[name] // MIX_W))
    return pl.pallas_call(
        _conv_body,
        grid=(B, nt),
        in_specs=[col("a_b"), col("a_c"), col("a_x"), pl.BlockSpec((CONV_K, MIX_W), lambda b, t: (0, 0))],
        out_specs=pl.BlockSpec((ts, MIX_W), lambda b, t: (b * nt + t, 0)),
        out_shape=jax.ShapeDtypeStruct((B * S, MIX_W), F32),
        scratch_shapes=[pltpu.VMEM((CONV_HALO, MIX_W), F32)],
        compiler_params=_cparams(("parallel", "arbitrary")),
        name="conv_mixer",
    )(proj, proj, proj, conv_w)


def _sgu_body(uv_ref, g_ref, w_ref, b_ref, o_ref):
    uv = jax.nn.gelu(uv_ref[...])
    u = uv[:, :MIX_W]
    v = uv[:, MIX_W:]
    v = v - jnp.mean(v, axis=-1, keepdims=True)
    v = (v * lax.rsqrt(jnp.mean(v * v, axis=-1, keepdims=True) + EPS) * g_ref[...]).astype(BF16)
    gw = MIX_W // SGU_GROUPS
    group = lax.broadcasted_iota(jnp.int32, (SGU_CHUNK, MIX_W), 1) >> (gw.bit_length() - 1)
    for c in range(uv.shape[0] // SGU_CHUNK):
        rows = slice(c * SGU_CHUNK, (c + 1) * SGU_CHUNK)
        vc = v[rows]
        s = b_ref[...]
        for g in range(SGU_GROUPS):
            s = s + jnp.dot(w_ref[g], jnp.where(group == g, vc, jnp.zeros_like(vc)),
                            preferred_element_type=F32)
        o_ref[rows, :] = u[rows] * s


def sgu_mixer_pallas(proj, ln_gain, w_s, b_s, T, tm=512):
    tri = jnp.tril(jnp.ones((SGU_CHUNK, SGU_CHUNK), dtype=bool))
    w = jnp.where(tri[None], w_s, 0).astype(BF16)
    b = jnp.repeat(b_s.T, MIX_W // SGU_GROUPS, axis=1)
    const = lambda *shape: pl.BlockSpec(shape, lambda i: (0,) * len(shape), pipeline_mode=pl.Buffered(1))
    return pl.pallas_call(
        _sgu_body,
        grid=(T // tm,),
        in_specs=[pl.BlockSpec((tm, 2 * MIX_W), lambda i: (i, COL["c_uv"] // (2 * MIX_W))),
                  const(1, MIX_W), const(SGU_GROUPS, SGU_CHUNK, SGU_CHUNK), const(SGU_CHUNK, MIX_W)],
        out_specs=pl.BlockSpec((tm, MIX_W), lambda i: (i, 0)),
        out_shape=jax.ShapeDtypeStruct((T, MIX_W), F32),
        compiler_params=_cparams(("parallel",)),
        name="sgu_mixer",
    )(proj, ln_gain[None], w, b)


def _layer(x2, p2, B, S, rel_bias, g_mix, w_in, conv_w, nsa_cmp_pe, nsa_cmp_w1, nsa_cmp_w2,
           nsa_q_gain, nsa_k_gain, sgu_ln_gain, sgu_w, sgu_b, dsa_q_gain, dsa_k_gain,
           w_gate, w_branch, w_out, g_ffn, peer_wq, peer_subkeys, peer_u, peer_v,
           g_ple, w_ple_gate, w_ple_proj):
    T = B * S
    proj = rms_matmul(x2, g_mix[None], pack_w_in(w_in), tm=1024, tn=768)

    y_a = conv_mixer(proj, conv_w, B, S)
    y_b = nsa_mixer_pallas(proj, nsa_cmp_pe, nsa_cmp_w1, nsa_cmp_w2, nsa_q_gain, nsa_k_gain,
                           rel_bias[:, :NSA_HEADS], rel_bias_tiles(rel_bias[:, :NSA_HEADS]), B, S)
    y_c = sgu_mixer_pallas(proj, sgu_ln_gain, sgu_w, sgu_b, T)
    y_d = dsa_attention(proj, COL, dsa_q_gain[None], dsa_k_gain[None],
                        rel_bias_tiles(rel_bias[:, NSA_HEADS:]), B, S)
    ys = [y_a, y_b, y_c, y_d]

    x2 = merge_mixers(x2, g_mix[None], ys, w_gate.astype(BF16),
                      w_branch.reshape(N_MIXERS, MIX_W, D_MODEL).astype(BF16),
                      w_out.astype(BF16), tm=256)

    sk = peer_subkeys.reshape(2 * PEER_HEADS, PEER_KEYS, PEER_QDIM // 2).astype(BF16)
    x2 = peer_ffn_update(x2, g_ffn[None], peer_wq.astype(BF16), sk,
                         peer_u.astype(BF16), peer_v.astype(BF16).T)

    x2 = ple_update(x2, g_ple[None], p2, w_ple_gate.astype(BF16), w_ple_proj.astype(BF16), tm=512)
    return x2


def kernel(x, p, rel_bias, g_mix, w_in, conv_w, nsa_cmp_pe, nsa_cmp_w1, nsa_cmp_w2, nsa_q_gain,
           nsa_k_gain, sgu_ln_gain, sgu_w, sgu_b, dsa_q_gain, dsa_k_gain, w_gate, w_branch, w_out,
           g_ffn, peer_wq, peer_subkeys, peer_u, peer_v, g_ple, w_ple_gate, w_ple_proj):
    B, S, D = x.shape
    depth = p.shape[0]
    x2 = x.reshape(B * S, D)
    for l in range(depth):
        x2 = _layer(x2, p[l].reshape(B * S, PLE_DIM), B, S, rel_bias, g_mix[l], w_in[l], conv_w[l],
                    nsa_cmp_pe[l], nsa_cmp_w1[l], nsa_cmp_w2[l], nsa_q_gain[l], nsa_k_gain[l],
                    sgu_ln_gain[l], sgu_w[l], sgu_b[l], dsa_q_gain[l], dsa_k_gain[l],
                    w_gate[l], w_branch[l], w_out[l], g_ffn[l], peer_wq[l], peer_subkeys[l],
                    peer_u[l], peer_v[l], g_ple[l], w_ple_gate[l], w_ple_proj[l])
    return x2.reshape(B, S, D)
```

```python
import functools
import math

import jax
import jax.numpy as jnp
import numpy as np
from jax import lax
from jax.experimental import pallas as pl
from jax.experimental.pallas import tpu as pltpu

F32 = jnp.float32
BF16 = jnp.bfloat16

D_MODEL = 1024
HEAD_DIM = 64
N_MIXERS = 4
MIX_W = D_MODEL // 2
Q_BLOCK = 128
EPS = 1e-6
NEG_INF = -1e30
CONV_K = 3
NSA_HEADS = MIX_W // HEAD_DIM
NSA_KV_HEADS = 2
CMP_BLOCK = 32
CMP_STRIDE = 16
CMP_HIDDEN = 256
SEL_BLOCK = 64
SEL_TOPN = 8
WINDOW = 512
FORCED_SCORE = 1e4
SGU_CHUNK = 128
SGU_GROUPS = 8
DSA_HEADS = MIX_W // HEAD_DIM
IDX_HEADS = 8
IDX_DIM = 64
DSA_TOPK_MAX = 256
N_BUCKETS = 32
MAX_DISTANCE = 1024
PEER_HEADS = 8
PEER_KEYS = 128
PEER_QDIM = 128
PEER_TOPK = 16
N_EXPERTS = PEER_KEYS * PEER_KEYS
PLE_DIM = 256

SPLIT_WIDTHS = (
    MIX_W, MIX_W, MIX_W,
    NSA_HEADS * HEAD_DIM,
    6 * NSA_KV_HEADS * HEAD_DIM,
    3 * NSA_HEADS,
    2 * MIX_W,
    DSA_HEADS * HEAD_DIM, HEAD_DIM, HEAD_DIM,
    IDX_HEADS * IDX_DIM, IDX_DIM, IDX_HEADS,
)
IN_WIDTH = sum(SPLIT_WIDTHS)

COL = dict(a_b=0, a_c=512, a_x=1024, n_q=1536, d_q=2048, d_qi=2560, c_uv=3072, n_kv=4096,
           n_g=4864, d_kv=4992, d_kiw=5120)
N_PACK = 5376


def pack_w_in(w):
    o = dict(zip(("a_b", "a_c", "a_x", "n_q", "n_kv", "n_g", "c_uv", "d_q", "d_k", "d_v", "d_qi", "d_ki",
                  "d_wi", "end"), [0] + [int(c) for c in np.cumsum(SPLIT_WIDTHS)]))
    z = lambda n: jnp.zeros((w.shape[0], n), w.dtype)
    cols = [w[:, o["a_b"]:o["n_kv"]], w[:, o["d_q"]:o["d_k"]], w[:, o["d_qi"]:o["d_ki"]],
            w[:, o["c_uv"]:o["d_q"]], w[:, o["n_kv"]:o["n_g"]],
            w[:, o["n_g"]:o["c_uv"]], z(128 - 3 * NSA_HEADS),
            w[:, o["d_k"]:o["d_qi"]],
            w[:, o["d_ki"]:o["end"]], z(128 - IDX_DIM - IDX_HEADS)]
    packed = jnp.concatenate(cols, axis=1)
    return jnp.pad(packed, ((0, 0), (0, N_PACK - packed.shape[1]))).astype(BF16)

VMEM_LIMIT_BYTES = 56 * 1024 * 1024


def _cparams(sem):
    return pltpu.CompilerParams(dimension_semantics=sem, vmem_limit_bytes=VMEM_LIMIT_BYTES)


def _rms(x, g):
    return x * lax.rsqrt(jnp.mean(x * x, axis=-1, keepdims=True) + EPS) * g


def _rms_matmul_body(x_ref, g_ref, w_ref, o_ref, h_ref):
    @pl.when(pl.program_id(1) == 0)
    def _():
        h_ref[...] = _rms(x_ref[...], g_ref[...]).astype(BF16)

    o_ref[...] = jnp.dot(h_ref[...], w_ref[...], preferred_element_type=F32)


def rms_matmul(x, g, w, tm, tn):
    T, D = x.shape
    N = w.shape[1]
    return pl.pallas_call(
        _rms_matmul_body,
        grid=(T // tm, N // tn),
        in_specs=[pl.BlockSpec((tm, D), lambda i, j: (i, 0)),
                  pl.BlockSpec((1, D), lambda i, j: (0, 0)),
                  pl.BlockSpec((D, tn), lambda i, j: (0, j))],
        out_specs=pl.BlockSpec((tm, tn), lambda i, j: (i, j)),
        out_shape=jax.ShapeDtypeStruct((T, N), F32),
        scratch_shapes=[pltpu.VMEM((tm, D), BF16)],
        compiler_params=_cparams(("parallel", "arbitrary")),
        name="rms_matmul",
    )(x, g, w)


def _merge_body(x_ref, g_ref, ya_ref, yb_ref, yc_ref, yd_ref, wg_ref, wb_ref, wo_ref, o_ref):
    x = x_ref[...]
    h = _rms(x, g_ref[...]).astype(BF16)
    merged = jnp.zeros(x.shape, F32)
    for m, y_ref in enumerate((ya_ref, yb_ref, yc_ref, yd_ref)):
        z = jnp.dot(y_ref[...].astype(BF16), wb_ref[m], preferred_element_type=F32)
        gate = jax.nn.sigmoid(jnp.dot(h, wg_ref[:, m * D_MODEL:(m + 1) * D_MODEL],
                                      preferred_element_type=F32))
        merged = merged + gate * z
    o_ref[...] = x + jnp.dot(merged.astype(BF16), wo_ref[...], preferred_element_type=F32)


def merge_mixers(x, g, ys, w_gate, w_branch, w_out, tm):
    T, D = x.shape
    const = lambda *shape: pl.BlockSpec(shape, lambda i: (0,) * len(shape), pipeline_mode=pl.Buffered(1))
    return pl.pallas_call(
        _merge_body,
        grid=(T // tm,),
        in_specs=[pl.BlockSpec((tm, D), lambda i: (i, 0)),
                  const(1, D)]
                 + [pl.BlockSpec((tm, MIX_W), lambda i: (i, 0)) for _ in range(N_MIXERS)]
                 + [const(D, N_MIXERS * D), const(N_MIXERS, MIX_W, D), const(D, D)],
        out_specs=pl.BlockSpec((tm, D), lambda i: (i, 0)),
        out_shape=jax.ShapeDtypeStruct((T, D), F32),
        compiler_params=_cparams(("parallel",)),
        name="merge_mixers",
    )(x, g, *ys, w_gate, w_branch, w_out)


def _ple_body(x_ref, g_ref, p_ref, wg_ref, wp_ref, o_ref):
    x = x_ref[...]
    h = _rms(x, g_ref[...]).astype(BF16)
    gate = jax.nn.sigmoid(jnp.dot(h, wg_ref[...], preferred_element_type=F32))
    proj = jnp.dot(p_ref[...].astype(BF16), wp_ref[...], preferred_element_type=F32)
    o_ref[...] = x + gate * proj


def ple_update(x, g, p, w_gate, w_proj, tm):
    T, D = x.shape
    const = lambda *shape: pl.BlockSpec(shape, lambda i: (0,) * len(shape), pipeline_mode=pl.Buffered(1))
    return pl.pallas_call(
        _ple_body,
        grid=(T // tm,),
        in_specs=[pl.BlockSpec((tm, D), lambda i: (i, 0)), const(1, D),
                  pl.BlockSpec((tm, PLE_DIM), lambda i: (i, 0)),
                  const(D, D), const(PLE_DIM, D)],
        out_specs=pl.BlockSpec((tm, D), lambda i: (i, 0)),
        out_shape=jax.ShapeDtypeStruct((T, D), F32),
        compiler_params=_cparams(("parallel",)),
        name="ple_update",
    )(x, g, p, w_gate, w_proj)


TAKEN = -3e38
NO_PRIORITY = 1e9


def _extract_top16(cur, prio, tops_ref, one_per_round):
    rank = jnp.full(cur.shape, float(PEER_TOPK), F32)
    for r in range(PEER_TOPK):
        m = jnp.max(cur, axis=0, keepdims=True)
        if tops_ref is not None:
            tops_ref[r:r + 1, :] = m
        hit = cur == m
        if one_per_round:
            hit = prio == jnp.min(jnp.where(hit, prio, NO_PRIORITY), axis=0, keepdims=True)
        rank = jnp.where(hit, float(r), rank)
        cur = jnp.where(hit, TAKEN, cur)
    return rank


_CAND_ROWS_J = (16, 8, 5, 4, 3, 2, 2, 2)


def _peer_topk_body(x_ref, g_ref, wq_ref, sk_ref, hn_ref, cnt_ref, e1_ref, rank_ref, e2_ref,
                    t1_ref, t2_ref, j_ref):
    hn = _rms(x_ref[...], g_ref[...]).astype(BF16)
    hn_ref[...] = hn
    q = jnp.dot(hn, wq_ref[...], preferred_element_type=F32).astype(BF16)
    half = PEER_QDIM // 2
    tn = x_ref.shape[0]
    key_prio = lax.broadcasted_iota(jnp.int32, (PEER_KEYS, LANE_CHUNK), 0).astype(F32)
    j8 = lax.broadcasted_iota(jnp.int32, (8, LANE_CHUNK), 0)
    j16 = lax.broadcasted_iota(jnp.int32, (PEER_TOPK, LANE_CHUNK), 0)
    cand_prio = jnp.concatenate([j16] + [j8 + PEER_TOPK * i for i in range(1, 8)]
                                + [(j8 + 8) * PEER_TOPK], axis=0).astype(F32)

    def head_scores(h):
        return [lax.dot_general(sk_ref[2 * h + p], q[:, (2 * h + p) * half:(2 * h + p + 1) * half],
                                (((1,), (1,)), ((), ())), preferred_element_type=F32) for p in range(2)]

    def select(h, ls, s1, s2, one_per_round):
        rank1 = _extract_top16(s1, key_prio, t1_ref, one_per_round)
        rank2 = _extract_top16(s2, key_prio, t2_ref, one_per_round)
        a = t1_ref[...]
        b = t2_ref[...]
        b8 = b[0:8]
        pieces = [a[0:1] + b]
        for i in range(1, 8):
            cand_i = a[i:i + 1] + b8
            if _CAND_ROWS_J[i] < 8:
                cand_i = jnp.where(j8 < _CAND_ROWS_J[i], cand_i, NEG_INF)
            pieces.append(cand_i)
        pieces.append(a[8:16] + b[0:1])
        cand = jnp.concatenate(pieces, axis=0)
        taken = _extract_top16(cand, cand_prio, None, one_per_round) < PEER_TOPK
        top = a[0:1] + b[0:1]
        z = jnp.sum(jnp.where(taken, jnp.exp(cand - top), 0.0), axis=0, keepdims=True)
        one = jnp.where(taken, 1.0, 0.0)
        j_ref[0:1, :] = jnp.sum(one[0:PEER_TOPK], axis=0, keepdims=True)
        for i in range(1, 8):
            j_ref[i:i + 1, :] = jnp.sum(one[8 + 8 * i:16 + 8 * i], axis=0, keepdims=True)
        j_ref[8:PEER_TOPK, :] = one[9 * 8:10 * 8]
        counts = j_ref[...]
        cnt = jnp.zeros(s1.shape, F32)
        for r in range(PEER_TOPK):
            cnt = jnp.where(rank1 == float(r), counts[r:r + 1], cnt)
        in1 = rank1 < PEER_TOPK
        in2 = rank2 < PEER_TOPK
        cnt_ref[h, :, ls] = cnt
        rank_ref[h, :, ls] = rank2
        e1_ref[h, :, ls] = jnp.where(in1, jnp.exp(s1 - (a[0:1] + jnp.log(z))), 0.0)
        e2_ref[h, :, ls] = jnp.where(in2, jnp.exp(s2 - b[0:1]), 0.0)
        n_taken = jnp.maximum(jnp.sum(jnp.where(in1, 1.0, 0.0), axis=0, keepdims=True),
                              jnp.sum(jnp.where(in2, 1.0, 0.0), axis=0, keepdims=True))
        return jnp.max(jnp.maximum(n_taken, jnp.sum(counts, axis=0, keepdims=True)))

    chunks = [slice(c * LANE_CHUNK, (c + 1) * LANE_CHUNK) for c in range(tn // LANE_CHUNK)]
    most_taken = {}
    for h in range(PEER_HEADS):
        s1, s2 = head_scores(h)
        for ls in chunks:
            most_taken[h, ls.start] = select(h, ls, s1[:, ls], s2[:, ls], False)
    for h in range(PEER_HEADS):
        for ls in chunks:
            @pl.when(most_taken[h, ls.start] > PEER_TOPK)
            def _(h=h, ls=ls):
                s1, s2 = head_scores(h)
                select(h, ls, s1[:, ls], s2[:, ls], True)


def peer_topk(x, g, wq, subkeys, tn):
    T, D = x.shape
    const = lambda *shape: pl.BlockSpec(shape, lambda i: (0,) * len(shape), pipeline_mode=pl.Buffered(1))
    return pl.pallas_call(
        _peer_topk_body,
        grid=(T // tn,),
        in_specs=[pl.BlockSpec((tn, D), lambda i: (i, 0)), const(1, D), const(D, D),
                  const(2 * PEER_HEADS, PEER_KEYS, PEER_QDIM // 2)],
        out_specs=[pl.BlockSpec((tn, D), lambda i: (i, 0))]
                  + [pl.BlockSpec((PEER_HEADS, PEER_KEYS, tn), lambda i: (0, 0, i))] * 4,
        out_shape=[jax.ShapeDtypeStruct((T, D), BF16)]
                  + [jax.ShapeDtypeStruct((PEER_HEADS, PEER_KEYS, T), F32)] * 4,
        scratch_shapes=[pltpu.VMEM((PEER_TOPK, LANE_CHUNK), F32)] * 3,
        compiler_params=_cparams(("parallel",)),
        name="peer_topk",
    )(x, g, wq, subkeys)


LANE_CHUNK = 128


PEER_I1_STEP = 8


def _peer_main_body(hn_ref, cnt_ref, e1_ref, rank_ref, e2_ref, u_ref, vt_prev_ref, vt_ref, x_ref, o_ref,
                    acc_ref, p_ref):
    j = pl.program_id(1)
    tn = hn_ref.shape[0]
    slot = j % 2

    @pl.when(j == 0)
    def _():
        acc_ref[...] = jnp.zeros(acc_ref.shape, F32)
        p_ref[1] = jnp.zeros(p_ref.shape[1:], BF16)

    hn = hn_ref[...]
    for k in range(PEER_I1_STEP):
        rows = slice(k * PEER_KEYS, (k + 1) * PEER_KEYS)
        at = lax.dot_general(u_ref[rows, :], hn, (((1,), (1,)), ((), ())),
                             preferred_element_type=F32)
        for c in range(tn // LANE_CHUNK):
            ls = slice(c * LANE_CHUNK, (c + 1) * LANE_CHUNK)
            w = jnp.zeros((PEER_KEYS, LANE_CHUNK), F32)
            for h in range(PEER_HEADS):
                picked = jnp.where(rank_ref[h, :, ls] < cnt_ref[h, k:k + 1, ls], e2_ref[h, :, ls], 0.0)
                w = w + picked * e1_ref[h, k:k + 1, ls]
            p_ref[slot, rows, ls] = (w * jax.nn.gelu(at[:, ls])).astype(BF16)

    acc_ref[...] += jnp.dot(vt_prev_ref[...], p_ref[1 - slot], preferred_element_type=F32)

    @pl.when(j == pl.num_programs(1) - 1)
    def _():
        acc = acc_ref[...] + jnp.dot(vt_ref[...], p_ref[slot], preferred_element_type=F32)
        o_ref[...] = x_ref[...] + acc.T


def peer_main(hn, cnt, e1, rank2, e2, u, vt, x, tn):
    T, D = x.shape
    e_blk = PEER_I1_STEP * PEER_KEYS
    prev = lambda j: jnp.maximum(j - 1, 0)
    per_i1 = pl.BlockSpec((PEER_HEADS, PEER_I1_STEP, tn), lambda i, j: (0, j, i))
    once = pl.Buffered(1)
    per_i2 = pl.BlockSpec((PEER_HEADS, PEER_KEYS, tn), lambda i, j: (0, 0, i), pipeline_mode=once)
    return pl.pallas_call(
        _peer_main_body,
        grid=(T // tn, PEER_KEYS // PEER_I1_STEP),
        in_specs=[pl.BlockSpec((tn, D), lambda i, j: (i, 0), pipeline_mode=once),
                  per_i1, per_i1, per_i2, per_i2,
                  pl.BlockSpec((e_blk, D), lambda i, j: (j, 0)),
                  pl.BlockSpec((D, e_blk), lambda i, j: (0, prev(j))),
                  pl.BlockSpec((D, e_blk), lambda i, j: (0, j)),
                  pl.BlockSpec((tn, D), lambda i, j: (i, 0), pipeline_mode=once)],
        out_specs=pl.BlockSpec((tn, D), lambda i, j: (i, 0)),
        out_shape=jax.ShapeDtypeStruct((T, D), F32),
        scratch_shapes=[pltpu.VMEM((D, tn), F32), pltpu.VMEM((2, e_blk, tn), BF16)],
        compiler_params=_cparams(("parallel", "arbitrary")),
        name="peer_main",
    )(hn, cnt, e1, rank2, e2, u, vt, vt, x)


def peer_ffn_update(x, g, wq, subkeys, u, vt):
    hn, cnt, e1, rank2, e2 = peer_topk(x, g, wq, subkeys, tn=256)
    return peer_main(hn, cnt, e1, rank2, e2, u, vt, x, tn=1024)


BIAS_TILES = 9
LOG2E = math.log2(math.e)
KEY_TILES = 4
KEY_BLOCK = KEY_TILES * Q_BLOCK
WIN_TILES = WINDOW // Q_BLOCK + 1


def rel_bias_tiles(tbl):
    H = tbl.shape[1]
    d = jnp.arange(BIAS_TILES)[:, None, None]
    s = jnp.arange(Q_BLOCK)[None, :, None]
    t = jnp.arange(Q_BLOCK)[None, None, :]
    bias = tbl[rel_bucket(d * Q_BLOCK + t - s)] * LOG2E
    return jnp.moveaxis(bias, -1, 2).reshape(BIAS_TILES, Q_BLOCK, H * Q_BLOCK)


INT_MIN = -2 ** 31
NEG_INF_KEY = -1900671691


def _sortable_key(x):
    bits = pltpu.bitcast(x, jnp.int32)
    return jnp.where(bits < 0, bits ^ jnp.int32(0x7FFFFFFF), bits)


def _head_rms(x, g, scale):
    return x * lax.rsqrt(jnp.mean(x * x, axis=-1, keepdims=True) + EPS) * g * scale


def _dsa_body(kv_ref, kiw_ref, q_ref, qi_ref, wq_ref, qg_ref, kg_ref, bias_ref, o_ref,
              kn_scr, ki_scr, vt_scr, qall_scr, qiall_scr, sc_scr, th_scr,
              m_scr, l_scr, acc_scr, *, top_k):
    i = pl.program_id(1)
    dh = HEAD_DIM
    QB = Q_BLOCK

    @pl.when(i == 0)
    def _prep():
        kv = kv_ref[...]
        kn_scr[...] = _head_rms(kv[:, :dh], kg_ref[...], 1.0).astype(BF16)
        vt_scr[...] = kv[:, dh:].T.astype(BF16)
        ki_scr[...] = kiw_ref[:, :dh].astype(BF16)

    q = q_ref[...]
    qi = qi_ref[...]
    for h in range(DSA_HEADS):
        qh = _head_rms(q[:, h * dh:(h + 1) * dh], qg_ref[...], dh ** -0.5 * LOG2E)
        qall_scr[h * QB:(h + 1) * QB, :] = qh.astype(BF16)
        qiall_scr[h * QB:(h + 1) * QB, :] = qi[:, h * dh:(h + 1) * dh].astype(BF16)
    w_t = (wq_ref[:, dh:dh + IDX_HEADS] * (IDX_HEADS ** -0.5)).T

    s_loc = lax.broadcasted_iota(jnp.int32, (QB, QB), 0)
    t_loc = lax.broadcasted_iota(jnp.int32, (QB, QB), 1)
    n_tiles = i + 1
    n_blk = (i + KEY_TILES) // KEY_TILES
    nt_dims = (((1,), (1,)), ((), ()))

    def score_block(kb, carry):
        rows = pl.ds(pl.multiple_of(kb * KEY_BLOCK, KEY_BLOCK), KEY_BLOCK)
        kib = ki_scr[rows, :]
        sc = jnp.zeros((KEY_BLOCK, QB), F32)
        for h in range(IDX_HEADS):
            r = lax.dot_general(kib, qiall_scr[h * QB:(h + 1) * QB, :], nt_dims,
                                preferred_element_type=F32)
            sc = sc + w_t[h:h + 1, :] * jnp.maximum(r, 0.0)
        sc = jnp.where(sc == 0.0, 0.0, sc)
        s_pos = kb * KEY_BLOCK + lax.broadcasted_iota(jnp.int32, (KEY_BLOCK, QB), 0)
        t_pos = i * QB + lax.broadcasted_iota(jnp.int32, (KEY_BLOCK, QB), 1)
        sc_scr[rows, :] = _sortable_key(jnp.where(s_pos <= t_pos, sc, NEG_INF))
        return carry

    lax.fori_loop(0, n_blk, score_block, 0)

    def count_ge(cand):
        def body(kb, c):
            blk = sc_scr[pl.ds(pl.multiple_of(kb * KEY_BLOCK, KEY_BLOCK), KEY_BLOCK), :]
            return c + jnp.sum((blk >= cand).astype(jnp.int32), axis=0, keepdims=True)
        return lax.fori_loop(0, n_blk, body, jnp.zeros((1, QB), jnp.int32))

    th_scr[...] = jnp.full((1, QB), NEG_INF_KEY + 1, jnp.int32)

    @pl.when(n_tiles * QB > top_k)
    def _select():
        def bit_body(it, carry):
            ans_u, cnt_ans = carry
            cand_u = ans_u | (jnp.int32(1) << (31 - it))
            c = count_ge(cand_u ^ jnp.int32(INT_MIN))
            ok = c >= top_k
            return jnp.where(ok, cand_u, ans_u), jnp.where(ok, c, cnt_ans)

        ans_u, cnt = lax.fori_loop(
            0, 32, bit_body,
            (jnp.zeros((1, QB), jnp.int32), jnp.full((1, QB), 1, jnp.int32) * (n_blk * KEY_BLOCK)))
        th_scr[...] = ans_u ^ jnp.int32(INT_MIN)

        @pl.when(jnp.max(cnt) > top_k)
        def _ties():
            theta = th_scr[...]
            need = (top_k - count_ge(theta + 1)).astype(F32)
            tri = (s_loc >= t_loc).astype(BF16)

            def body(kt, seen):
                rows = pl.ds(pl.multiple_of(kt * QB, QB), QB)
                tile = sc_scr[rows, :]
                eq = tile == theta
                rank = seen + jnp.dot(tri, eq.astype(BF16), preferred_element_type=F32)
                sc_scr[rows, :] = jnp.where(eq & (rank > need), theta - 1, tile)
                return seen + jnp.sum(eq.astype(F32), axis=0, keepdims=True)

            lax.fori_loop(0, n_tiles, body, jnp.zeros((1, QB), F32))

    theta = th_scr[...]
    m_scr[...] = jnp.full(m_scr.shape, NEG_INF, F32)
    l_scr[...] = jnp.zeros(l_scr.shape, F32)
    acc_scr[...] = jnp.zeros(acc_scr.shape, F32)

    def att_block(kb, carry):
        row0 = pl.multiple_of(kb * KEY_BLOCK, KEY_BLOCK)
        rows = pl.ds(row0, KEY_BLOCK)
        masks = [sc_scr[pl.ds(row0 + j * QB, QB), :] >= theta for j in range(KEY_TILES)]
        _attend_block(kn_scr[rows, :], vt_scr[:, rows], qall_scr, bias_ref, i - kb * KEY_TILES,
                      masks, m_scr, l_scr, acc_scr)
        return carry

    lax.fori_loop(0, n_blk, att_block, 0)

    o_t = acc_scr[...] / jnp.maximum(l_scr[...], 1e-30)
    o_ref[...] = jnp.concatenate([o_t[:, h * QB:(h + 1) * QB].T for h in range(DSA_HEADS)], axis=1)


def dsa_attention(proj, col, q_gain, k_gain, bias, B, S):
    nq = S // Q_BLOCK
    top_k = min(DSA_TOPK_MAX, S // 4)
    HQ = DSA_HEADS * Q_BLOCK
    const = lambda *shape: pl.BlockSpec(shape, lambda b, i: (0,) * len(shape), pipeline_mode=pl.Buffered(1))
    return pl.pallas_call(
        functools.partial(_dsa_body, top_k=top_k),
        grid=(B, nq),
        in_specs=[pl.BlockSpec((S, 128), lambda b, i: (b, col["d_kv"] // 128)),
                  pl.BlockSpec((S, 128), lambda b, i: (b, col["d_kiw"] // 128)),
                  pl.BlockSpec((Q_BLOCK, MIX_W), lambda b, i: (b * nq + i, col["d_q"] // MIX_W)),
                  pl.BlockSpec((Q_BLOCK, MIX_W), lambda b, i: (b * nq + i, col["d_qi"] // MIX_W)),
                  pl.BlockSpec((Q_BLOCK, 128), lambda b, i: (b * nq + i, col["d_kiw"] // 128)),
                  const(1, HEAD_DIM), const(1, HEAD_DIM), const(BIAS_TILES, Q_BLOCK, HQ)],
        out_specs=pl.BlockSpec((Q_BLOCK, MIX_W), lambda b, i: (b * nq + i, 0)),
        out_shape=jax.ShapeDtypeStruct((B * S, MIX_W), F32),
        scratch_shapes=[pltpu.VMEM((S, HEAD_DIM), BF16), pltpu.VMEM((S, HEAD_DIM), BF16),
                        pltpu.VMEM((HEAD_DIM, S), BF16),
                        pltpu.VMEM((HQ, HEAD_DIM), BF16), pltpu.VMEM((HQ, HEAD_DIM), BF16),
                        pltpu.VMEM((S, Q_BLOCK), jnp.int32),
                        pltpu.VMEM((1, Q_BLOCK), jnp.int32),
                        pltpu.VMEM((1, HQ), F32), pltpu.VMEM((1, HQ), F32), pltpu.VMEM((HEAD_DIM, HQ), F32)],
        compiler_params=_cparams(("parallel", "arbitrary")),
        name="dsa_attention",
    )(proj, proj, proj, proj, proj, q_gain, k_gain, bias)


NSA_G = NSA_HEADS // NSA_KV_HEADS
CMP_PAD = 56
CMP_NEAR = 64
SEL_TAKEN = -3e38


def _cmp_rows(S):
    return -(-(CMP_PAD + S // CMP_STRIDE) // 64) * 64


def _nsa_prep_body(ks_ref, vs_ref, kw_ref, vw_ref, kg_ref, kso_ref, vso_ref, kwo_ref, vwo_ref):
    dh = HEAD_DIM
    for g in range(NSA_KV_HEADS):
        cs = slice(g * dh, (g + 1) * dh)
        kso_ref[0, g] = _head_rms(ks_ref[:, cs], kg_ref[1:2, :], 1.0).astype(BF16)
        kwo_ref[0, g] = _head_rms(kw_ref[:, cs], kg_ref[2:3, :], 1.0).astype(BF16)
        vso_ref[0, g] = vs_ref[:, cs].T.astype(BF16)
        vwo_ref[0, g] = vw_ref[:, cs].T.astype(BF16)


def nsa_prep(proj, k_gain, B, S, ts=512):
    nt = S // ts
    c0 = COL["n_kv"] // 128
    col = lambda j: pl.BlockSpec((ts, 128), lambda b, t: (b * nt + t, c0 + j))
    k_out = pl.BlockSpec((1, NSA_KV_HEADS, ts, HEAD_DIM), lambda b, t: (b, 0, t, 0))
    v_out = pl.BlockSpec((1, NSA_KV_HEADS, HEAD_DIM, ts), lambda b, t: (b, 0, 0, t))
    k_sds = jax.ShapeDtypeStruct((B, NSA_KV_HEADS, S, HEAD_DIM), BF16)
    v_sds = jax.ShapeDtypeStruct((B, NSA_KV_HEADS, HEAD_DIM, S), BF16)
    return pl.pallas_call(
        _nsa_prep_body,
        grid=(B, nt),
        in_specs=[col(2), col(3), col(4), col(5), pl.BlockSpec((3, HEAD_DIM), lambda b, t: (0, 0))],
        out_specs=[k_out, v_out, k_out, v_out],
        out_shape=[k_sds, v_sds, k_sds, v_sds],
        compiler_params=_cparams(("parallel", "parallel")),
        name="nsa_prep",
    )(proj, proj, proj, proj, k_gain)


def _nsa_compress_body(xk_ref, xv_ref, pe_ref, w1_ref, w2_ref, kg_ref, kc_ref, vct_ref):
    R = xk_ref.shape[3]
    ncp = kc_ref.shape[2]
    half = CMP_STRIDE * HEAD_DIM
    row = lax.broadcasted_iota(jnp.int32, (R, HEAD_DIM), 0)
    for j, x_ref in enumerate((xk_ref, xv_ref)):
        x = x_ref[0, 0, 0]
        pe = pe_ref[j]
        lo = jnp.dot((x + pe[:, :half]).astype(BF16), w1_ref[j, :half, :], preferred_element_type=F32)
        hi = jnp.dot((x + pe[:, half:]).astype(BF16), w1_ref[j, half:, :], preferred_element_type=F32)
        hid = jax.nn.gelu(lo + pltpu.roll(hi, R - 1, 0))
        c = jnp.dot(hid.astype(BF16), w2_ref[j], preferred_element_type=F32)
        if j == 0:
            c = _head_rms(c, kg_ref[0:1, :], 1.0)
        c = jnp.where(row < R - 1, c, 0.0)
        c = jnp.concatenate([jnp.zeros((CMP_PAD, HEAD_DIM), F32), c,
                             jnp.zeros((ncp - CMP_PAD - R, HEAD_DIM), F32)], axis=0)
        if j == 0:
            kc_ref[0, 0] = c.astype(BF16)
        else:
            vct_ref[0, 0] = c.T.astype(BF16)


def nsa_compress(xc, pe, w1, w2, k_gain, S):
    B = xc.shape[0]
    R = S // CMP_STRIDE
    ncp = _cmp_rows(S)
    const = lambda *shape: pl.BlockSpec(shape, lambda b, g: (0,) * len(shape), pipeline_mode=pl.Buffered(1))
    return pl.pallas_call(
        _nsa_compress_body,
        grid=(B, NSA_KV_HEADS),
        in_specs=[pl.BlockSpec((1, 1, 1, R, CMP_STRIDE * HEAD_DIM), lambda b, g: (b, 0, g, 0, 0)),
                  pl.BlockSpec((1, 1, 1, R, CMP_STRIDE * HEAD_DIM), lambda b, g: (b, 1, g, 0, 0)),
                  const(2, 1, CMP_BLOCK * HEAD_DIM), const(2, CMP_BLOCK * HEAD_DIM, CMP_HIDDEN),
                  const(2, CMP_HIDDEN, HEAD_DIM), const(3, HEAD_DIM)],
        out_specs=[pl.BlockSpec((1, 1, ncp, HEAD_DIM), lambda b, g: (b, g, 0, 0)),
                   pl.BlockSpec((1, 1, HEAD_DIM, ncp), lambda b, g: (b, g, 0, 0))],
        out_shape=[jax.ShapeDtypeStruct((B, NSA_KV_HEADS, ncp, HEAD_DIM), BF16),
                   jax.ShapeDtypeStruct((B, NSA_KV_HEADS, HEAD_DIM, ncp), BF16)],
        compiler_params=_cparams(("parallel", "parallel")),
        name="nsa_compress",
    )(xc, xc, pe, w1, w2, k_gain)


def _attend_block(k_b, v_b, q_scr, bias_ref, d0, masks, m_scr, l_scr, acc_scr):
    QB = Q_BLOCK
    CH = 2 * QB
    n_sub = len(masks)
    masks2 = [jnp.concatenate([mk, mk], axis=1) for mk in masks]
    d = [jnp.clip(d0 - j, 0, BIAS_TILES - 1) for j in range(n_sub)]
    chunks = [slice(c * CH, (c + 1) * CH) for c in range(q_scr.shape[0] // CH)]
    logits = [lax.dot_general(k_b, q_scr[ls, :], (((1,), (1,)), ((), ())), preferred_element_type=F32)
              for ls in chunks]
    for ls, lg in zip(chunks, logits):
        parts = [jnp.where(masks2[j], lg[j * QB:(j + 1) * QB] + bias_ref[d[j], :, ls], NEG_INF)
                 for j in range(n_sub)]
        m_old = m_scr[:, ls]
        m_new = m_old
        for part in parts:
            m_new = jnp.maximum(m_new, jnp.max(part, axis=0, keepdims=True))
        probs = [jnp.exp2(part - m_new) for part in parts]
        alpha = jnp.exp2(m_old - m_new)
        l_new = alpha * l_scr[:, ls]
        for pr in probs:
            l_new = l_new + jnp.sum(pr, axis=0, keepdims=True)
        l_scr[:, ls] = l_new
        pb = jnp.concatenate([pr.astype(BF16) for pr in probs], axis=0)
        acc_scr[:, ls] = alpha * acc_scr[:, ls] + jnp.dot(v_b, pb, preferred_element_type=F32)
        m_scr[:, ls] = m_new


def _nsa_body(q_ref, gl_ref, kc_ref, vct_ref, ks_ref, vst_ref, kw_ref, vwt_ref, qg_ref, bias_ref,
              gcd_ref, c31_ref, ovl_ref, o_ref,
              qall_scr, lc_scr, sel_scr, gt_scr, ms_scr, ls_scr, as_scr, mw_scr, lw_scr, aw_scr,
              *, n_sel, top_n):
    g = pl.program_id(1)
    i = pl.program_id(2)
    dh = HEAD_DIM
    QB = Q_BLOCK
    GQ = NSA_G * QB
    ncp = kc_ref.shape[2]

    q = q_ref[...]
    for r in range(NSA_G):
        qall_scr[r * QB:(r + 1) * QB, :] = _head_rms(q[:, r * dh:(r + 1) * dh], qg_ref[...],
                                                     dh ** -0.5 * LOG2E).astype(BF16)
    gt_scr[...] = jax.nn.sigmoid(gl_ref[...]).T

    lc_scr[...] = lax.dot_general(kc_ref[0, 0], qall_scr[...], (((1,), (1,)), ((), ())),
                                  preferred_element_type=F32) + c31_ref[0]
    near = pl.ds(pl.multiple_of(i * (QB // CMP_STRIDE), 8), CMP_NEAR)
    lc_scr[near, :] = lc_scr[near, :] + gcd_ref[0]
    n_p = lax.broadcasted_iota(jnp.int32, (ncp, GQ), 0)
    t_c = i * QB + (lax.broadcasted_iota(jnp.int32, (ncp, GQ), 1) & (QB - 1))
    cmp_end = jnp.where(n_p >= CMP_PAD, (n_p - CMP_PAD) * CMP_STRIDE + (CMP_BLOCK - 1), 1 << 30)
    mask_c = cmp_end <= t_c
    lc = jnp.where(mask_c, lc_scr[...], NEG_INF)
    e = jnp.where(mask_c, jnp.exp2(lc - jnp.max(lc, axis=0, keepdims=True)), 0.0)
    pc = (e / jnp.maximum(jnp.sum(e, axis=0, keepdims=True), 1e-30)).astype(BF16)
    o_c = jnp.dot(vct_ref[0, 0], pc, preferred_element_type=F32)

    imp = jnp.zeros((n_sel, QB), F32)
    for r in range(NSA_G):
        imp = imp + jnp.dot(ovl_ref[...], pc[:, r * QB:(r + 1) * QB], preferred_element_type=F32)
    j_io = lax.broadcasted_iota(jnp.int32, (n_sel, QB), 0).astype(F32)
    cur = ((i * QB + lax.broadcasted_iota(jnp.int32, (n_sel, QB), 1))
           >> (SEL_BLOCK.bit_length() - 1)).astype(F32)
    imp = jnp.where((j_io == 0) | (j_io == cur), FORCED_SCORE, imp)
    imp = jnp.where(j_io <= cur, imp, NEG_INF)
    sel = jnp.zeros((n_sel, QB), F32)
    for _ in range(top_n):
        mx = jnp.max(imp, axis=0, keepdims=True)
        first = jnp.min(jnp.where(imp == mx, j_io, float(n_sel)), axis=0, keepdims=True)
        hit = j_io == first
        sel = jnp.where(hit, 1.0, sel)
        imp = jnp.where(hit, SEL_TAKEN, imp)
    sel_scr[...] = sel

    for m_scr, l_scr, a_scr in ((ms_scr, ls_scr, as_scr), (mw_scr, lw_scr, aw_scr)):
        m_scr[...] = jnp.full(m_scr.shape, NEG_INF, F32)
        l_scr[...] = jnp.zeros(l_scr.shape, F32)
        a_scr[...] = jnp.zeros(a_scr.shape, F32)
    s_loc = lax.broadcasted_iota(jnp.int32, (QB, QB), 0)
    t_loc = lax.broadcasted_iota(jnp.int32, (QB, QB), 1)
    blocks_per_tile = QB // SEL_BLOCK

    def slc_block(kb, carry):
        rows = pl.ds(pl.multiple_of(kb * KEY_BLOCK, KEY_BLOCK), KEY_BLOCK)
        masks = []
        for j in range(KEY_TILES):
            kt = kb * KEY_TILES + j
            picked = jnp.where(s_loc < SEL_BLOCK, sel_scr[pl.ds(kt * blocks_per_tile, 1), :],
                               sel_scr[pl.ds(kt * blocks_per_tile + 1, 1), :])
            dist = (t_loc + i * QB) - (s_loc + kt * QB)
            masks.append(jnp.where(dist >= 0, picked, 0.0) > 0.5)
        _attend_block(ks_ref[0, 0, rows, :], vst_ref[0, 0, :, rows], qall_scr, bias_ref,
                      i - kb * KEY_TILES, masks, ms_scr, ls_scr, as_scr)
        return carry

    lax.fori_loop(0, (i + KEY_TILES) // KEY_TILES, slc_block, 0)

    kt0 = jnp.maximum(i - WINDOW // QB, 0)
    rows = pl.ds(pl.multiple_of(kt0 * QB, QB), WIN_TILES * QB)
    masks = []
    for j in range(WIN_TILES):
        dist = (t_loc + i * QB) - (s_loc + (kt0 + j) * QB)
        masks.append((dist >= 0) & (dist < WINDOW))
    _attend_block(kw_ref[0, 0, rows, :], vwt_ref[0, 0, :, rows], qall_scr, bias_ref, i - kt0, masks,
                  mw_scr, lw_scr, aw_scr)

    o_s = as_scr[...] / jnp.maximum(ls_scr[...], 1e-30)
    o_w = aw_scr[...] / jnp.maximum(lw_scr[...], 1e-30)
    outs = []
    for r in range(NSA_G):
        ls = slice(r * QB, (r + 1) * QB)
        gate = lambda j: gt_scr[pl.ds(j * NSA_HEADS + g * NSA_G + r, 1), :]
        o_r = gate(0) * o_c[:, ls] + gate(1) * o_s[:, ls] + gate(2) * o_w[:, ls]
        outs.append(o_r.T)
    o_ref[...] = jnp.concatenate(outs, axis=1)


def nsa_attention(proj, kc, vct, ks, vst, kw, vwt, q_gain, bias, gcd, c31, ovl, B, S):
    nq = S // Q_BLOCK
    GQ = NSA_G * Q_BLOCK
    GW = NSA_G * HEAD_DIM
    ncp = kc.shape[2]
    n_sel = S // SEL_BLOCK
    const = lambda *shape: pl.BlockSpec(shape, lambda b, g, i: (0,) * len(shape), pipeline_mode=pl.Buffered(1))
    per_group = lambda *shape: pl.BlockSpec((1, 1) + shape, lambda b, g, i: (b, g, 0, 0))
    return pl.pallas_call(
        functools.partial(_nsa_body, n_sel=n_sel, top_n=min(SEL_TOPN, n_sel)),
        grid=(B, NSA_KV_HEADS, nq),
        in_specs=[pl.BlockSpec((Q_BLOCK, GW), lambda b, g, i: (b * nq + i, COL["n_q"] // GW + g)),
                  pl.BlockSpec((Q_BLOCK, 128), lambda b, g, i: (b * nq + i, COL["n_g"] // 128)),
                  per_group(ncp, HEAD_DIM), per_group(HEAD_DIM, ncp),
                  per_group(S, HEAD_DIM), per_group(HEAD_DIM, S),
                  per_group(S, HEAD_DIM), per_group(HEAD_DIM, S),
                  const(1, HEAD_DIM),
                  pl.BlockSpec((BIAS_TILES, Q_BLOCK, GQ), lambda b, g, i: (0, 0, g)),
                  pl.BlockSpec((1, CMP_NEAR, GQ), lambda b, g, i: (g, 0, 0)),
                  pl.BlockSpec((1, 1, GQ), lambda b, g, i: (g, 0, 0)),
                  const(n_sel, ncp)],
        out_specs=pl.BlockSpec((Q_BLOCK, GW), lambda b, g, i: (b * nq + i, g)),
        out_shape=jax.ShapeDtypeStruct((B * S, MIX_W), F32),
        scratch_shapes=[pltpu.VMEM((GQ, HEAD_DIM), BF16), pltpu.VMEM((ncp, GQ), F32),
                        pltpu.VMEM((n_sel, Q_BLOCK), F32), pltpu.VMEM((128, Q_BLOCK), F32)]
                       + [pltpu.VMEM((1, GQ), F32), pltpu.VMEM((1, GQ), F32), pltpu.VMEM((HEAD_DIM, GQ), F32)] * 2,
        compiler_params=_cparams(("parallel", "parallel", "arbitrary")),
        name="nsa_attention",
    )(proj, proj, kc, vct, ks, vst, kw, vwt, q_gain, bias, gcd, c31, ovl)


def nsa_tables(tbl, S):
    ncp = _cmp_rows(S)
    n_rel = jnp.arange(CMP_NEAR)[:, None] - CMP_PAD
    t_loc = jnp.arange(Q_BLOCK)[None, :]
    near = tbl[rel_bucket(t_loc - CMP_STRIDE * n_rel - (CMP_BLOCK - 1))] * LOG2E
    far = tbl[N_BUCKETS - 1] * LOG2E
    gcd = jnp.moveaxis(near - far, -1, 0).reshape(NSA_KV_HEADS, NSA_G, CMP_NEAR, Q_BLOCK)
    gcd = jnp.moveaxis(gcd, 1, 2).reshape(NSA_KV_HEADS, CMP_NEAR, NSA_G * Q_BLOCK)
    c31 = jnp.repeat(far, Q_BLOCK).reshape(NSA_KV_HEADS, 1, NSA_G * Q_BLOCK)
    n = np.arange(ncp) - CMP_PAD
    sel_start = np.arange(S // SEL_BLOCK) * SEL_BLOCK
    start = n * CMP_STRIDE
    ovl = ((start[None, :] < sel_start[:, None] + SEL_BLOCK) & (start[None, :] + CMP_BLOCK > sel_start[:, None])
           & (n[None, :] >= 0) & (n[None, :] < (S - CMP_BLOCK) // CMP_STRIDE + 1))
    return gcd, c31, jnp.asarray(ovl, BF16)


def nsa_mixer_pallas(proj, cmp_pe, cmp_w1, cmp_w2, q_gain, k_gain, rel_tbl, bias, B, S):
    dh = HEAD_DIM
    c0 = COL["n_kv"]
    xc = proj[:, c0:c0 + 2 * NSA_KV_HEADS * dh].reshape(B, S // CMP_STRIDE, CMP_STRIDE, 2, NSA_KV_HEADS, dh)
    xc = xc.transpose(0, 3, 4, 1, 2, 5).reshape(B, 2, NSA_KV_HEADS, S // CMP_STRIDE, CMP_STRIDE * dh)
    kc, vct = nsa_compress(xc, cmp_pe.reshape(2, 1, CMP_BLOCK * dh), cmp_w1.astype(BF16),
                           cmp_w2.astype(BF16), k_gain, S)
    ks, vst, kw, vwt = nsa_prep(proj, k_gain, B, S)
    gcd, c31, ovl = nsa_tables(rel_tbl, S)
    return nsa_attention(proj, kc, vct, ks, vst, kw, vwt, q_gain[None], bias, gcd, c31, ovl, B, S)


def rel_bucket(dist):
    n = jnp.maximum(dist, 0)
    max_exact = N_BUCKETS // 2
    nf = jnp.maximum(n, 1).astype(jnp.float32)
    large = max_exact + (jnp.log(nf / max_exact) / math.log(MAX_DISTANCE / max_exact)
                         * (N_BUCKETS - max_exact)).astype(jnp.int32)
    return jnp.where(n < max_exact, n, jnp.minimum(large, N_BUCKETS - 1))


CONV_HALO = 8


def _conv_body(b_ref, c_ref, x_ref, w_ref, o_ref, prev_scr):
    @pl.when(pl.program_id(1) == 0)
    def _():
        prev_scr[...] = jnp.zeros(prev_scr.shape, F32)

    bx = c_ref[...] * x_ref[...]
    ts = bx.shape[0]
    row = lax.broadcasted_iota(jnp.int32, bx.shape, 0)
    prev = prev_scr[...]
    last1 = prev[CONV_HALO - 1:CONV_HALO]
    last2 = prev[CONV_HALO - 2:CONV_HALO - 1]
    back1 = jnp.where(row == 0, last1, pltpu.roll(bx, 1, 0))
    back2 = jnp.where(row == 0, last2, jnp.where(row == 1, last1, pltpu.roll(bx, 2, 0)))
    w = w_ref[...]
    o_ref[...] = b_ref[...] * (w[0:1] * back2 + w[1:2] * back1 + w[2:3] * bx)
    prev_scr[...] = bx[ts - CONV_HALO:ts]


def conv_mixer(proj, conv_w, B, S, ts=512):
    nt = S // ts
    col = lambda name: pl.BlockSpec((ts, MIX_W), lambda b, t: (b * nt + t, COL[name] // MIX_W))
    return pl.pallas_call(
        _conv_body,
        grid=(B, nt),
        in_specs=[col("a_b"), col("a_c"), col("a_x"), pl.BlockSpec((CONV_K, MIX_W), lambda b, t: (0, 0))],
        out_specs=pl.BlockSpec((ts, MIX_W), lambda b, t: (b * nt + t, 0)),
        out_shape=jax.ShapeDtypeStruct((B * S, MIX_W), F32),
        scratch_shapes=[pltpu.VMEM((CONV_HALO, MIX_W), F32)],
        compiler_params=_cparams(("parallel", "arbitrary")),
        name="conv_mixer",
    )(proj, proj, proj, conv_w)


def _sgu_body(uv_ref, g_ref, w_ref, b_ref, o_ref):
    uv = jax.nn.gelu(uv_ref[...])
    u = uv[:, :MIX_W]
    v = uv[:, MIX_W:]
    v = v - jnp.mean(v, axis=-1, keepdims=True)
    v = (v * lax.rsqrt(jnp.mean(v * v, axis=-1, keepdims=True) + EPS) * g_ref[...]).astype(BF16)
    gw = MIX_W // SGU_GROUPS
    group = lax.broadcasted_iota(jnp.int32, (SGU_CHUNK, MIX_W), 1) >> (gw.bit_length() - 1)
    for c in range(uv.shape[0] // SGU_CHUNK):
        rows = slice(c * SGU_CHUNK, (c + 1) * SGU_CHUNK)
        vc = v[rows]
        s = b_ref[...]
        for g in range(SGU_GROUPS):
            s = s + jnp.dot(w_ref[g], jnp.where(group == g, vc, jnp.zeros_like(vc)),
                            preferred_element_type=F32)
        o_ref[rows, :] = u[rows] * s


def sgu_mixer_pallas(proj, ln_gain, w_s, b_s, T, tm=512):
    tri = jnp.tril(jnp.ones((SGU_CHUNK, SGU_CHUNK), dtype=bool))
    w = jnp.where(tri[None], w_s, 0).astype(BF16)
    b = jnp.repeat(b_s.T, MIX_W // SGU_GROUPS, axis=1)
    const = lambda *shape: pl.BlockSpec(shape, lambda i: (0,) * len(shape), pipeline_mode=pl.Buffered(1))
    return pl.pallas_call(
        _sgu_body,
        grid=(T // tm,),
        in_specs=[pl.BlockSpec((tm, 2 * MIX_W), lambda i: (i, COL["c_uv"] // (2 * MIX_W))),
                  const(1, MIX_W), const(SGU_GROUPS, SGU_CHUNK, SGU_CHUNK), const(SGU_CHUNK, MIX_W)],
        out_specs=pl.BlockSpec((tm, MIX_W), lambda i: (i, 0)),
        out_shape=jax.ShapeDtypeStruct((T, MIX_W), F32),
        compiler_params=_cparams(("parallel",)),
        name="sgu_mixer",
    )(proj, ln_gain[None], w, b)


def _layer(x2, p2, B, S, rel_bias, g_mix, w_in, conv_w, nsa_cmp_pe, nsa_cmp_w1, nsa_cmp_w2,
           nsa_q_gain, nsa_k_gain, sgu_ln_gain, sgu_w, sgu_b, dsa_q_gain, dsa_k_gain,
           w_gate, w_branch, w_out, g_ffn, peer_wq, peer_subkeys, peer_u, peer_v,
           g_ple, w_ple_gate, w_ple_proj):
    T = B * S
    proj = rms_matmul(x2, g_mix[None], pack_w_in(w_in), tm=1024, tn=768)

    y_a = conv_mixer(proj, conv_w, B, S)
    y_b = nsa_mixer_pallas(proj, nsa_cmp_pe, nsa_cmp_w1, nsa_cmp_w2, nsa_q_gain, nsa_k_gain,
                           rel_bias[:, :NSA_HEADS], rel_bias_tiles(rel_bias[:, :NSA_HEADS]), B, S)
    y_c = sgu_mixer_pallas(proj, sgu_ln_gain, sgu_w, sgu_b, T)
    y_d = dsa_attention(proj, COL, dsa_q_gain[None], dsa_k_gain[None],
                        rel_bias_tiles(rel_bias[:, NSA_HEADS:]), B, S)
    ys = [y_a, y_b, y_c, y_d]

    x2 = merge_mixers(x2, g_mix[None], ys, w_gate.astype(BF16),
                      w_branch.reshape(N_MIXERS, MIX_W, D_MODEL).astype(BF16),
                      w_out.astype(BF16), tm=256)

    sk = peer_subkeys.reshape(2 * PEER_HEADS, PEER_KEYS, PEER_QDIM // 2).astype(BF16)
    x2 = peer_ffn_update(x2, g_ffn[None], peer_wq.astype(BF16), sk,
                         peer_u.astype(BF16), peer_v.astype(BF16).T)

    x2 = ple_update(x2, g_ple[None], p2, w_ple_gate.astype(BF16), w_ple_proj.astype(BF16), tm=512)
    return x2


def kernel(x, p, rel_bias, g_mix, w_in, conv_w, nsa_cmp_pe, nsa_cmp_w1, nsa_cmp_w2, nsa_q_gain,
           nsa_k_gain, sgu_ln_gain, sgu_w, sgu_b, dsa_q_gain, dsa_k_gain, w_gate, w_branch, w_out,
           g_ffn, peer_wq, peer_subkeys, peer_u, peer_v, g_ple, w_ple_gate, w_ple_proj):
    B, S, D = x.shape
    depth = p.shape[0]
    x2 = x.reshape(B * S, D)
    for l in range(depth):
        x2 = _layer(x2, p[l].reshape(B * S, PLE_DIM), B, S, rel_bias, g_mix[l], w_in[l], conv_w[l],
                    nsa_cmp_pe[l], nsa_cmp_w1[l], nsa_cmp_w2[l], nsa_q_gain[l], nsa_k_gain[l],
                    sgu_ln_gain[l], sgu_w[l], sgu_b[l], dsa_q_gain[l], dsa_k_gain[l],
                    w_gate[l], w_branch[l], w_out[l], g_ffn[l], peer_wq[l], peer_subkeys[l],
                    peer_u[l], peer_v[l], g_ple[l], w_ple_gate[l], w_ple_proj[l])
    return x2.reshape(B, S, D)
```

```python
import functools
import math

import jax
import jax.numpy as jnp
import numpy as np
from jax import lax
from jax.experimental import pallas as pl
from jax.experimental.pallas import tpu as pltpu

F32 = jnp.float32
BF16 = jnp.bfloat16

D_MODEL = 1024
HEAD_DIM = 64
N_MIXERS = 4
MIX_W = D_MODEL // 2
Q_BLOCK = 128
EPS = 1e-6
NEG_INF = -1e30
CONV_K = 3
NSA_HEADS = MIX_W // HEAD_DIM
NSA_KV_HEADS = 2
CMP_BLOCK = 32
CMP_STRIDE = 16
CMP_HIDDEN = 256
SEL_BLOCK = 64
SEL_TOPN = 8
WINDOW = 512
FORCED_SCORE = 1e4
SGU_CHUNK = 128
SGU_GROUPS = 8
DSA_HEADS = MIX_W // HEAD_DIM
IDX_HEADS = 8
IDX_DIM = 64
DSA_TOPK_MAX = 256
N_BUCKETS = 32
MAX_DISTANCE = 1024
PEER_HEADS = 8
PEER_KEYS = 128
PEER_QDIM = 128
PEER_TOPK = 16
N_EXPERTS = PEER_KEYS * PEER_KEYS
PLE_DIM = 256

SPLIT_WIDTHS = (
    MIX_W, MIX_W, MIX_W,
    NSA_HEADS * HEAD_DIM,
    6 * NSA_KV_HEADS * HEAD_DIM,
    3 * NSA_HEADS,
    2 * MIX_W,
    DSA_HEADS * HEAD_DIM, HEAD_DIM, HEAD_DIM,
    IDX_HEADS * IDX_DIM, IDX_DIM, IDX_HEADS,
)
IN_WIDTH = sum(SPLIT_WIDTHS)

COL = dict(a_b=0, a_c=512, a_x=1024, n_q=1536, d_q=2048, d_qi=2560, c_uv=3072, n_kv=4096,
           n_g=4864, d_kv=4992, d_kiw=5120)
N_PACK = 5376


def pack_w_in(w):
    o = dict(zip(("a_b", "a_c", "a_x", "n_q", "n_kv", "n_g", "c_uv", "d_q", "d_k", "d_v", "d_qi", "d_ki",
                  "d_wi", "end"), [0] + [int(c) for c in np.cumsum(SPLIT_WIDTHS)]))
    z = lambda n: jnp.zeros((w.shape[0], n), w.dtype)
    cols = [w[:, o["a_b"]:o["n_kv"]], w[:, o["d_q"]:o["d_k"]], w[:, o["d_qi"]:o["d_ki"]],
            w[:, o["c_uv"]:o["d_q"]], w[:, o["n_kv"]:o["n_g"]],
            w[:, o["n_g"]:o["c_uv"]], z(128 - 3 * NSA_HEADS),
            w[:, o["d_k"]:o["d_qi"]],
            w[:, o["d_ki"]:o["end"]], z(128 - IDX_DIM - IDX_HEADS)]
    packed = jnp.concatenate(cols, axis=1)
    return jnp.pad(packed, ((0, 0), (0, N_PACK - packed.shape[1]))).astype(BF16)

VMEM_LIMIT_BYTES = 56 * 1024 * 1024


def _cparams(sem):
    return pltpu.CompilerParams(dimension_semantics=sem, vmem_limit_bytes=VMEM_LIMIT_BYTES)


def _rms(x, g):
    return x * lax.rsqrt(jnp.mean(x * x, axis=-1, keepdims=True) + EPS) * g


def _rms_matmul_body(x_ref, g_ref, w_ref, o_ref, h_ref):
    @pl.when(pl.program_id(1) == 0)
    def _():
        h_ref[...] = _rms(x_ref[...], g_ref[...]).astype(BF16)

    o_ref[...] = jnp.dot(h_ref[...], w_ref[...], preferred_element_type=F32)


def rms_matmul(x, g, w, tm, tn):
    T, D = x.shape
    N = w.shape[1]
    return pl.pallas_call(
        _rms_matmul_body,
        grid=(T // tm, N // tn),
        in_specs=[pl.BlockSpec((tm, D), lambda i, j: (i, 0)),
                  pl.BlockSpec((1, D), lambda i, j: (0, 0)),
                  pl.BlockSpec((D, tn), lambda i, j: (0, j))],
        out_specs=pl.BlockSpec((tm, tn), lambda i, j: (i, j)),
        out_shape=jax.ShapeDtypeStruct((T, N), F32),
        scratch_shapes=[pltpu.VMEM((tm, D), BF16)],
        compiler_params=_cparams(("parallel", "arbitrary")),
        name="rms_matmul",
    )(x, g, w)


def _merge_body(x_ref, g_ref, ya_ref, yb_ref, yc_ref, yd_ref, wg_ref, wb_ref, wo_ref, o_ref):
    x = x_ref[...]
    h = _rms(x, g_ref[...]).astype(BF16)
    merged = jnp.zeros(x.shape, F32)
    for m, y_ref in enumerate((ya_ref, yb_ref, yc_ref, yd_ref)):
        z = jnp.dot(y_ref[...].astype(BF16), wb_ref[m], preferred_element_type=F32)
        gate = jax.nn.sigmoid(jnp.dot(h, wg_ref[:, m * D_MODEL:(m + 1) * D_MODEL],
                                      preferred_element_type=F32))
        merged = merged + gate * z
    o_ref[...] = x + jnp.dot(merged.astype(BF16), wo_ref[...], preferred_element_type=F32)


def merge_mixers(x, g, ys, w_gate, w_branch, w_out, tm):
    T, D = x.shape
    const = lambda *shape: pl.BlockSpec(shape, lambda i: (0,) * len(shape), pipeline_mode=pl.Buffered(1))
    return pl.pallas_call(
        _merge_body,
        grid=(T // tm,),
        in_specs=[pl.BlockSpec((tm, D), lambda i: (i, 0)),
                  const(1, D)]
                 + [pl.BlockSpec((tm, MIX_W), lambda i: (i, 0)) for _ in range(N_MIXERS)]
                 + [const(D, N_MIXERS * D), const(N_MIXERS, MIX_W, D), const(D, D)],
        out_specs=pl.BlockSpec((tm, D), lambda i: (i, 0)),
        out_shape=jax.ShapeDtypeStruct((T, D), F32),
        compiler_params=_cparams(("parallel",)),
        name="merge_mixers",
    )(x, g, *ys, w_gate, w_branch, w_out)


def _ple_body(x_ref, g_ref, p_ref, wg_ref, wp_ref, o_ref):
    x = x_ref[...]
    h = _rms(x, g_ref[...]).astype(BF16)
    gate = jax.nn.sigmoid(jnp.dot(h, wg_ref[...], preferred_element_type=F32))
    proj = jnp.dot(p_ref[...].astype(BF16), wp_ref[...], preferred_element_type=F32)
    o_ref[...] = x + gate * proj


def ple_update(x, g, p, w_gate, w_proj, tm):
    T, D = x.shape
    const = lambda *shape: pl.BlockSpec(shape, lambda i: (0,) * len(shape), pipeline_mode=pl.Buffered(1))
    return pl.pallas_call(
        _ple_body,
        grid=(T // tm,),
        in_specs=[pl.BlockSpec((tm, D), lambda i: (i, 0)), const(1, D),
                  pl.BlockSpec((tm, PLE_DIM), lambda i: (i, 0)),
                  const(D, D), const(PLE_DIM, D)],
        out_specs=pl.BlockSpec((tm, D), lambda i: (i, 0)),
        out_shape=jax.ShapeDtypeStruct((T, D), F32),
        compiler_params=_cparams(("parallel",)),
        name="ple_update",
    )(x, g, p, w_gate, w_proj)


TAKEN = -3e38
NO_PRIORITY = 1e9


def _extract_top16(cur, prio, tops_ref, one_per_round):
    rank = jnp.full(cur.shape, float(PEER_TOPK), F32)
    for r in range(PEER_TOPK):
        m = jnp.max(cur, axis=0, keepdims=True)
        if tops_ref is not None:
            tops_ref[r:r + 1, :] = m
        hit = cur == m
        if one_per_round:
            hit = prio == jnp.min(jnp.where(hit, prio, NO_PRIORITY), axis=0, keepdims=True)
        rank = jnp.where(hit, float(r), rank)
        cur = jnp.where(hit, TAKEN, cur)
    return rank


_CAND_ROWS_J = (16, 8, 5, 4, 3, 2, 2, 2)


def _peer_topk_body(x_ref, g_ref, wq_ref, sk_ref, hn_ref, cnt_ref, e1_ref, rank_ref, e2_ref,
                    t1_ref, t2_ref, j_ref):
    hn = _rms(x_ref[...], g_ref[...]).astype(BF16)
    hn_ref[...] = hn
    q = jnp.dot(hn, wq_ref[...], preferred_element_type=F32).astype(BF16)
    half = PEER_QDIM // 2
    tn = x_ref.shape[0]
    key_prio = lax.broadcasted_iota(jnp.int32, (PEER_KEYS, LANE_CHUNK), 0).astype(F32)
    j8 = lax.broadcasted_iota(jnp.int32, (8, LANE_CHUNK), 0)
    j16 = lax.broadcasted_iota(jnp.int32, (PEER_TOPK, LANE_CHUNK), 0)
    cand_prio = jnp.concatenate([j16] + [j8 + PEER_TOPK * i for i in range(1, 8)]
                                + [(j8 + 8) * PEER_TOPK], axis=0).astype(F32)

    def head_scores(h):
        return [lax.dot_general(sk_ref[2 * h + p], q[:, (2 * h + p) * half:(2 * h + p + 1) * half],
                                (((1,), (1,)), ((), ())), preferred_element_type=F32) for p in range(2)]

    def select(h, ls, s1, s2, one_per_round):
        rank1 = _extract_top16(s1, key_prio, t1_ref, one_per_round)
        rank2 = _extract_top16(s2, key_prio, t2_ref, one_per_round)
        a = t1_ref[...]
        b = t2_ref[...]
        b8 = b[0:8]
        pieces = [a[0:1] + b]
        for i in range(1, 8):
            cand_i = a[i:i + 1] + b8
            if _CAND_ROWS_J[i] < 8:
                cand_i = jnp.where(j8 < _CAND_ROWS_J[i], cand_i, NEG_INF)
            pieces.append(cand_i)
        pieces.append(a[8:16] + b[0:1])
        cand = jnp.concatenate(pieces, axis=0)
        taken = _extract_top16(cand, cand_prio, None, one_per_round) < PEER_TOPK
        top = a[0:1] + b[0:1]
        z = jnp.sum(jnp.where(taken, jnp.exp(cand - top), 0.0), axis=0, keepdims=True)
        one = jnp.where(taken, 1.0, 0.0)
        j_ref[0:1, :] = jnp.sum(one[0:PEER_TOPK], axis=0, keepdims=True)
        for i in range(1, 8):
            j_ref[i:i + 1, :] = jnp.sum(one[8 + 8 * i:16 + 8 * i], axis=0, keepdims=True)
        j_ref[8:PEER_TOPK, :] = one[9 * 8:10 * 8]
        counts = j_ref[...]
        cnt = jnp.zeros(s1.shape, F32)
        for r in range(PEER_TOPK):
            cnt = jnp.where(rank1 == float(r), counts[r:r + 1], cnt)
        in1 = rank1 < PEER_TOPK
        in2 = rank2 < PEER_TOPK
        cnt_ref[h, :, ls] = cnt
        rank_ref[h, :, ls] = rank2
        e1_ref[h, :, ls] = jnp.where(in1, jnp.exp(s1 - (a[0:1] + jnp.log(z))), 0.0)
        e2_ref[h, :, ls] = jnp.where(in2, jnp.exp(s2 - b[0:1]), 0.0)
        n_taken = jnp.maximum(jnp.sum(jnp.where(in1, 1.0, 0.0), axis=0, keepdims=True),
                              jnp.sum(jnp.where(in2, 1.0, 0.0), axis=0, keepdims=True))
        return jnp.max(jnp.maximum(n_taken, jnp.sum(counts, axis=0, keepdims=True)))

    chunks = [slice(c * LANE_CHUNK, (c + 1) * LANE_CHUNK) for c in range(tn // LANE_CHUNK)]
    most_taken = {}
    for h in range(PEER_HEADS):
        s1, s2 = head_scores(h)
        for ls in chunks:
            most_taken[h, ls.start] = select(h, ls, s1[:, ls], s2[:, ls], False)
    for h in range(PEER_HEADS):
        for ls in chunks:
            @pl.when(most_taken[h, ls.start] > PEER_TOPK)
            def _(h=h, ls=ls):
                s1, s2 = head_scores(h)
                select(h, ls, s1[:, ls], s2[:, ls], True)


def peer_topk(x, g, wq, subkeys, tn):
    T, D = x.shape
    const = lambda *shape: pl.BlockSpec(shape, lambda i: (0,) * len(shape), pipeline_mode=pl.Buffered(1))
    return pl.pallas_call(
        _peer_topk_body,
        grid=(T // tn,),
        in_specs=[pl.BlockSpec((tn, D), lambda i: (i, 0)), const(1, D), const(D, D),
                  const(2 * PEER_HEADS, PEER_KEYS, PEER_QDIM // 2)],
        out_specs=[pl.BlockSpec((tn, D), lambda i: (i, 0))]
                  + [pl.BlockSpec((PEER_HEADS, PEER_KEYS, tn), lambda i: (0, 0, i))] * 4,
        out_shape=[jax.ShapeDtypeStruct((T, D), BF16)]
                  + [jax.ShapeDtypeStruct((PEER_HEADS, PEER_KEYS, T), F32)] * 4,
        scratch_shapes=[pltpu.VMEM((PEER_TOPK, LANE_CHUNK), F32)] * 3,
        compiler_params=_cparams(("parallel",)),
        name="peer_topk",
    )(x, g, wq, subkeys)


LANE_CHUNK = 128


PEER_I1_STEP = 8


def _peer_main_body(hn_ref, cnt_ref, e1_ref, rank_ref, e2_ref, u_ref, vt_prev_ref, vt_ref, x_ref, o_ref,
                    acc_ref, p_ref):
    j = pl.program_id(1)
    tn = hn_ref.shape[0]
    slot = j % 2

    @pl.when(j == 0)
    def _():
        acc_ref[...] = jnp.zeros(acc_ref.shape, F32)
        p_ref[1] = jnp.zeros(p_ref.shape[1:], BF16)

    acc_ref[...] += jnp.dot(vt_prev_ref[...], p_ref[1 - slot], preferred_element_type=F32)

    hn = hn_ref[...]
    for k in range(PEER_I1_STEP):
        rows = slice(k * PEER_KEYS, (k + 1) * PEER_KEYS)
        at = lax.dot_general(u_ref[rows, :], hn, (((1,), (1,)), ((), ())),
                             preferred_element_type=F32)
        for c in range(tn // LANE_CHUNK):
            ls = slice(c * LANE_CHUNK, (c + 1) * LANE_CHUNK)
            w = jnp.zeros((PEER_KEYS, LANE_CHUNK), F32)
            for h in range(PEER_HEADS):
                picked = jnp.where(rank_ref[h, :, ls] < cnt_ref[h, k:k + 1, ls], e2_ref[h, :, ls], 0.0)
                w = w + picked * e1_ref[h, k:k + 1, ls]
            p_ref[slot, rows, ls] = (w * jax.nn.gelu(at[:, ls])).astype(BF16)

    @pl.when(j == pl.num_programs(1) - 1)
    def _():
        acc = acc_ref[...] + jnp.dot(vt_ref[...], p_ref[slot], preferred_element_type=F32)
        o_ref[...] = x_ref[...] + acc.T


def peer_main(hn, cnt, e1, rank2, e2, u, vt, x, tn):
    T, D = x.shape
    e_blk = PEER_I1_STEP * PEER_KEYS
    prev = lambda j: jnp.maximum(j - 1, 0)
    per_i1 = pl.BlockSpec((PEER_HEADS, PEER_I1_STEP, tn), lambda i, j: (0, j, i))
    once = pl.Buffered(1)
    per_i2 = pl.BlockSpec((PEER_HEADS, PEER_KEYS, tn), lambda i, j: (0, 0, i), pipeline_mode=once)
    return pl.pallas_call(
        _peer_main_body,
        grid=(T // tn, PEER_KEYS // PEER_I1_STEP),
        in_specs=[pl.BlockSpec((tn, D), lambda i, j: (i, 0), pipeline_mode=once),
                  per_i1, per_i1, per_i2, per_i2,
                  pl.BlockSpec((e_blk, D), lambda i, j: (j, 0)),
                  pl.BlockSpec((D, e_blk), lambda i, j: (0, prev(j))),
                  pl.BlockSpec((D, e_blk), lambda i, j: (0, j)),
                  pl.BlockSpec((tn, D), lambda i, j: (i, 0), pipeline_mode=once)],
        out_specs=pl.BlockSpec((tn, D), lambda i, j: (i, 0)),
        out_shape=jax.ShapeDtypeStruct((T, D), F32),
        scratch_shapes=[pltpu.VMEM((D, tn), F32), pltpu.VMEM((2, e_blk, tn), BF16)],
        compiler_params=_cparams(("parallel", "arbitrary")),
        name="peer_main",
    )(hn, cnt, e1, rank2, e2, u, vt, vt, x)


def peer_ffn_update(x, g, wq, subkeys, u, vt):
    hn, cnt, e1, rank2, e2 = peer_topk(x, g, wq, subkeys, tn=256)
    return peer_main(hn, cnt, e1, rank2, e2, u, vt, x, tn=1024)


BIAS_TILES = 9
LOG2E = math.log2(math.e)
KEY_TILES = 4
KEY_BLOCK = KEY_TILES * Q_BLOCK
WIN_TILES = WINDOW // Q_BLOCK + 1


def rel_bias_tiles(tbl):
    H = tbl.shape[1]
    d = jnp.arange(BIAS_TILES)[:, None, None]
    s = jnp.arange(Q_BLOCK)[None, :, None]
    t = jnp.arange(Q_BLOCK)[None, None, :]
    bias = tbl[rel_bucket(d * Q_BLOCK + t - s)] * LOG2E
    return jnp.moveaxis(bias, -1, 2).reshape(BIAS_TILES, Q_BLOCK, H * Q_BLOCK)


INT_MIN = -2 ** 31
NEG_INF_KEY = -1900671691


def _sortable_key(x):
    bits = pltpu.bitcast(x, jnp.int32)
    return jnp.where(bits < 0, bits ^ jnp.int32(0x7FFFFFFF), bits)


def _head_rms(x, g, scale):
    return x * lax.rsqrt(jnp.mean(x * x, axis=-1, keepdims=True) + EPS) * g * scale


def _dsa_body(kv_ref, kiw_ref, q_ref, qi_ref, wq_ref, qg_ref, kg_ref, bias_ref, o_ref,
              kn_scr, ki_scr, vt_scr, qall_scr, qiall_scr, sc_scr, th_scr,
              m_scr, l_scr, acc_scr, *, top_k):
    i = pl.program_id(1)
    dh = HEAD_DIM
    QB = Q_BLOCK

    @pl.when(i == 0)
    def _prep():
        kv = kv_ref[...]
        kn_scr[...] = _head_rms(kv[:, :dh], kg_ref[...], 1.0).astype(BF16)
        vt_scr[...] = kv[:, dh:].T.astype(BF16)
        ki_scr[...] = kiw_ref[:, :dh].astype(BF16)

    q = q_ref[...]
    qi = qi_ref[...]
    for h in range(DSA_HEADS):
        qh = _head_rms(q[:, h * dh:(h + 1) * dh], qg_ref[...], dh ** -0.5 * LOG2E)
        qall_scr[h * QB:(h + 1) * QB, :] = qh.astype(BF16)
        qiall_scr[h * QB:(h + 1) * QB, :] = qi[:, h * dh:(h + 1) * dh].astype(BF16)
    w_t = (wq_ref[:, dh:dh + IDX_HEADS] * (IDX_HEADS ** -0.5)).T

    s_loc = lax.broadcasted_iota(jnp.int32, (QB, QB), 0)
    t_loc = lax.broadcasted_iota(jnp.int32, (QB, QB), 1)
    n_tiles = i + 1
    n_blk = (i + KEY_TILES) // KEY_TILES
    nt_dims = (((1,), (1,)), ((), ()))

    def score_block(kb, carry):
        rows = pl.ds(pl.multiple_of(kb * KEY_BLOCK, KEY_BLOCK), KEY_BLOCK)
        kib = ki_scr[rows, :]
        sc = jnp.zeros((KEY_BLOCK, QB), F32)
        for h in range(IDX_HEADS):
            r = lax.dot_general(kib, qiall_scr[h * QB:(h + 1) * QB, :], nt_dims,
                                preferred_element_type=F32)
            sc = sc + w_t[h:h + 1, :] * jnp.maximum(r, 0.0)
        sc = jnp.where(sc == 0.0, 0.0, sc)
        s_pos = kb * KEY_BLOCK + lax.broadcasted_iota(jnp.int32, (KEY_BLOCK, QB), 0)
        t_pos = i * QB + lax.broadcasted_iota(jnp.int32, (KEY_BLOCK, QB), 1)
        sc_scr[rows, :] = _sortable_key(jnp.where(s_pos <= t_pos, sc, NEG_INF))
        return carry

    lax.fori_loop(0, n_blk, score_block, 0)

    def count_ge(cand):
        def body(kb, c):
            blk = sc_scr[pl.ds(pl.multiple_of(kb * KEY_BLOCK, KEY_BLOCK), KEY_BLOCK), :]
            return c + jnp.sum((blk >= cand).astype(jnp.int32), axis=0, keepdims=True)
        return lax.fori_loop(0, n_blk, body, jnp.zeros((1, QB), jnp.int32))

    th_scr[...] = jnp.full((1, QB), NEG_INF_KEY + 1, jnp.int32)

    @pl.when(n_tiles * QB > top_k)
    def _select():
        def bit_body(it, carry):
            ans_u, cnt_ans = carry
            cand_u = ans_u | (jnp.int32(1) << (31 - it))
            c = count_ge(cand_u ^ jnp.int32(INT_MIN))
            ok = c >= top_k
            return jnp.where(ok, cand_u, ans_u), jnp.where(ok, c, cnt_ans)

        ans_u, cnt = lax.fori_loop(
            0, 32, bit_body,
            (jnp.zeros((1, QB), jnp.int32), jnp.full((1, QB), 1, jnp.int32) * (n_blk * KEY_BLOCK)))
        th_scr[...] = ans_u ^ jnp.int32(INT_MIN)

        @pl.when(jnp.max(cnt) > top_k)
        def _ties():
            theta = th_scr[...]
            need = (top_k - count_ge(theta + 1)).astype(F32)
            tri = (s_loc >= t_loc).astype(BF16)

            def body(kt, seen):
                rows = pl.ds(pl.multiple_of(kt * QB, QB), QB)
                tile = sc_scr[rows, :]
                eq = tile == theta
                rank = seen + jnp.dot(tri, eq.astype(BF16), preferred_element_type=F32)
                sc_scr[rows, :] = jnp.where(eq & (rank > need), theta - 1, tile)
                return seen + jnp.sum(eq.astype(F32), axis=0, keepdims=True)

            lax.fori_loop(0, n_tiles, body, jnp.zeros((1, QB), F32))

    theta = th_scr[...]
    m_scr[...] = jnp.full(m_scr.shape, NEG_INF, F32)
    l_scr[...] = jnp.zeros(l_scr.shape, F32)
    acc_scr[...] = jnp.zeros(acc_scr.shape, F32)

    def att_block(kb, carry):
        row0 = pl.multiple_of(kb * KEY_BLOCK, KEY_BLOCK)
        rows = pl.ds(row0, KEY_BLOCK)
        masks = [sc_scr[pl.ds(row0 + j * QB, QB), :] >= theta for j in range(KEY_TILES)]
        _attend_block(kn_scr[rows, :], vt_scr[:, rows], qall_scr, bias_ref, i - kb * KEY_TILES,
                      masks, m_scr, l_scr, acc_scr)
        return carry

    lax.fori_loop(0, n_blk, att_block, 0)

    o_t = acc_scr[...] / jnp.maximum(l_scr[...], 1e-30)
    o_ref[...] = jnp.concatenate([o_t[:, h * QB:(h + 1) * QB].T for h in range(DSA_HEADS)], axis=1)


def dsa_attention(proj, col, q_gain, k_gain, bias, B, S):
    nq = S // Q_BLOCK
    top_k = min(DSA_TOPK_MAX, S // 4)
    HQ = DSA_HEADS * Q_BLOCK
    const = lambda *shape: pl.BlockSpec(shape, lambda b, i: (0,) * len(shape), pipeline_mode=pl.Buffered(1))
    return pl.pallas_call(
        functools.partial(_dsa_body, top_k=top_k),
        grid=(B, nq),
        in_specs=[pl.BlockSpec((S, 128), lambda b, i: (b, col["d_kv"] // 128)),
                  pl.BlockSpec((S, 128), lambda b, i: (b, col["d_kiw"] // 128)),
                  pl.BlockSpec((Q_BLOCK, MIX_W), lambda b, i: (b * nq + i, col["d_q"] // MIX_W)),
                  pl.BlockSpec((Q_BLOCK, MIX_W), lambda b, i: (b * nq + i, col["d_qi"] // MIX_W)),
                  pl.BlockSpec((Q_BLOCK, 128), lambda b, i: (b * nq + i, col["d_kiw"] // 128)),
                  const(1, HEAD_DIM), const(1, HEAD_DIM), const(BIAS_TILES, Q_BLOCK, HQ)],
        out_specs=pl.BlockSpec((Q_BLOCK, MIX_W), lambda b, i: (b * nq + i, 0)),
        out_shape=jax.ShapeDtypeStruct((B * S, MIX_W), F32),
        scratch_shapes=[pltpu.VMEM((S, HEAD_DIM), BF16), pltpu.VMEM((S, HEAD_DIM), BF16),
                        pltpu.VMEM((HEAD_DIM, S), BF16),
                        pltpu.VMEM((HQ, HEAD_DIM), BF16), pltpu.VMEM((HQ, HEAD_DIM), BF16),
                        pltpu.VMEM((S, Q_BLOCK), jnp.int32),
                        pltpu.VMEM((1, Q_BLOCK), jnp.int32),
                        pltpu.VMEM((1, HQ), F32), pltpu.VMEM((1, HQ), F32), pltpu.VMEM((HEAD_DIM, HQ), F32)],
        compiler_params=_cparams(("parallel", "arbitrary")),
        name="dsa_attention",
    )(proj, proj, proj, proj, proj, q_gain, k_gain, bias)


NSA_G = NSA_HEADS // NSA_KV_HEADS
CMP_PAD = 56
CMP_NEAR = 64
SEL_TAKEN = -3e38


def _cmp_rows(S):
    return -(-(CMP_PAD + S // CMP_STRIDE) // 64) * 64


def _nsa_prep_body(ks_ref, vs_ref, kw_ref, vw_ref, kg_ref, kso_ref, vso_ref, kwo_ref, vwo_ref):
    dh = HEAD_DIM
    for g in range(NSA_KV_HEADS):
        cs = slice(g * dh, (g + 1) * dh)
        kso_ref[0, g] = _head_rms(ks_ref[:, cs], kg_ref[1:2, :], 1.0).astype(BF16)
        kwo_ref[0, g] = _head_rms(kw_ref[:, cs], kg_ref[2:3, :], 1.0).astype(BF16)
        vso_ref[0, g] = vs_ref[:, cs].T.astype(BF16)
        vwo_ref[0, g] = vw_ref[:, cs].T.astype(BF16)


def nsa_prep(proj, k_gain, B, S, ts=512):
    nt = S // ts
    c0 = COL["n_kv"] // 128
    col = lambda j: pl.BlockSpec((ts, 128), lambda b, t: (b * nt + t, c0 + j))
    k_out = pl.BlockSpec((1, NSA_KV_HEADS, ts, HEAD_DIM), lambda b, t: (b, 0, t, 0))
    v_out = pl.BlockSpec((1, NSA_KV_HEADS, HEAD_DIM, ts), lambda b, t: (b, 0, 0, t))
    k_sds = jax.ShapeDtypeStruct((B, NSA_KV_HEADS, S, HEAD_DIM), BF16)
    v_sds = jax.ShapeDtypeStruct((B, NSA_KV_HEADS, HEAD_DIM, S), BF16)
    return pl.pallas_call(
        _nsa_prep_body,
        grid=(B, nt),
        in_specs=[col(2), col(3), col(4), col(5), pl.BlockSpec((3, HEAD_DIM), lambda b, t: (0, 0))],
        out_specs=[k_out, v_out, k_out, v_out],
        out_shape=[k_sds, v_sds, k_sds, v_sds],
        compiler_params=_cparams(("parallel", "parallel")),
        name="nsa_prep",
    )(proj, proj, proj, proj, k_gain)


def _nsa_compress_body(xk_ref, xv_ref, pe_ref, w1_ref, w2_ref, kg_ref, kc_ref, vct_ref):
    R = xk_ref.shape[3]
    ncp = kc_ref.shape[2]
    half = CMP_STRIDE * HEAD_DIM
    row = lax.broadcasted_iota(jnp.int32, (R, HEAD_DIM), 0)
    for j, x_ref in enumerate((xk_ref, xv_ref)):
        x = x_ref[0, 0, 0]
        pe = pe_ref[j]
        lo = jnp.dot((x + pe[:, :half]).astype(BF16), w1_ref[j, :half, :], preferred_element_type=F32)
        hi = jnp.dot((x + pe[:, half:]).astype(BF16), w1_ref[j, half:, :], preferred_element_type=F32)
        hid = jax.nn.gelu(lo + pltpu.roll(hi, R - 1, 0))
        c = jnp.dot(hid.astype(BF16), w2_ref[j], preferred_element_type=F32)
        if j == 0:
            c = _head_rms(c, kg_ref[0:1, :], 1.0)
        c = jnp.where(row < R - 1, c, 0.0)
        c = jnp.concatenate([jnp.zeros((CMP_PAD, HEAD_DIM), F32), c,
                             jnp.zeros((ncp - CMP_PAD - R, HEAD_DIM), F32)], axis=0)
        if j == 0:
            kc_ref[0, 0] = c.astype(BF16)
        else:
            vct_ref[0, 0] = c.T.astype(BF16)


def nsa_compress(xc, pe, w1, w2, k_gain, S):
    B = xc.shape[0]
    R = S // CMP_STRIDE
    ncp = _cmp_rows(S)
    const = lambda *shape: pl.BlockSpec(shape, lambda b, g: (0,) * len(shape), pipeline_mode=pl.Buffered(1))
    return pl.pallas_call(
        _nsa_compress_body,
        grid=(B, NSA_KV_HEADS),
        in_specs=[pl.BlockSpec((1, 1, 1, R, CMP_STRIDE * HEAD_DIM), lambda b, g: (b, 0, g, 0, 0)),
                  pl.BlockSpec((1, 1, 1, R, CMP_STRIDE * HEAD_DIM), lambda b, g: (b, 1, g, 0, 0)),
                  const(2, 1, CMP_BLOCK * HEAD_DIM), const(2, CMP_BLOCK * HEAD_DIM, CMP_HIDDEN),
                  const(2, CMP_HIDDEN, HEAD_DIM), const(3, HEAD_DIM)],
        out_specs=[pl.BlockSpec((1, 1, ncp, HEAD_DIM), lambda b, g: (b, g, 0, 0)),
                   pl.BlockSpec((1, 1, HEAD_DIM, ncp), lambda b, g: (b, g, 0, 0))],
        out_shape=[jax.ShapeDtypeStruct((B, NSA_KV_HEADS, ncp, HEAD_DIM), BF16),
                   jax.ShapeDtypeStruct((B, NSA_KV_HEADS, HEAD_DIM, ncp), BF16)],
        compiler_params=_cparams(("parallel", "parallel")),
        name="nsa_compress",
    )(xc, xc, pe, w1, w2, k_gain)


def _attend_block(k_b, v_b, q_scr, bias_ref, d0, masks, m_scr, l_scr, acc_scr):
    QB = Q_BLOCK
    CH = 2 * QB
    n_sub = len(masks)
    masks2 = [jnp.concatenate([mk, mk], axis=1) for mk in masks]
    d = [jnp.clip(d0 - j, 0, BIAS_TILES - 1) for j in range(n_sub)]
    chunks = [slice(c * CH, (c + 1) * CH) for c in range(q_scr.shape[0] // CH)]
    logits = [lax.dot_general(k_b, q_scr[ls, :], (((1,), (1,)), ((), ())), preferred_element_type=F32)
              for ls in chunks]
    for ls, lg in zip(chunks, logits):
        parts = [jnp.where(masks2[j], lg[j * QB:(j + 1) * QB] + bias_ref[d[j], :, ls], NEG_INF)
                 for j in range(n_sub)]
        m_old = m_scr[:, ls]
        m_new = m_old
        for part in parts:
            m_new = jnp.maximum(m_new, jnp.max(part, axis=0, keepdims=True))
        probs = [jnp.exp2(part - m_new) for part in parts]
        alpha = jnp.exp2(m_old - m_new)
        l_new = alpha * l_scr[:, ls]
        for pr in probs:
            l_new = l_new + jnp.sum(pr, axis=0, keepdims=True)
        l_scr[:, ls] = l_new
        pb = jnp.concatenate([pr.astype(BF16) for pr in probs], axis=0)
        acc_scr[:, ls] = alpha * acc_scr[:, ls] + jnp.dot(v_b, pb, preferred_element_type=F32)
        m_scr[:, ls] = m_new


def _nsa_body(q_ref, gl_ref, kc_ref, vct_ref, ks_ref, vst_ref, kw_ref, vwt_ref, qg_ref, bias_ref,
              gcd_ref, c31_ref, ovl_ref, o_ref,
              qall_scr, lc_scr, sel_scr, gt_scr, ms_scr, ls_scr, as_scr, mw_scr, lw_scr, aw_scr,
              *, n_sel, top_n):
    g = pl.program_id(1)
    i = pl.program_id(2)
    dh = HEAD_DIM
    QB = Q_BLOCK
    GQ = NSA_G * QB
    ncp = kc_ref.shape[2]

    q = q_ref[...]
    for r in range(NSA_G):
        qall_scr[r * QB:(r + 1) * QB, :] = _head_rms(q[:, r * dh:(r + 1) * dh], qg_ref[...],
                                                     dh ** -0.5 * LOG2E).astype(BF16)
    gt_scr[...] = jax.nn.sigmoid(gl_ref[...]).T

    lc_scr[...] = lax.dot_general(kc_ref[0, 0], qall_scr[...], (((1,), (1,)), ((), ())),
                                  preferred_element_type=F32) + c31_ref[0]
    near = pl.ds(pl.multiple_of(i * (QB // CMP_STRIDE), 8), CMP_NEAR)
    lc_scr[near, :] = lc_scr[near, :] + gcd_ref[0]
    n_p = lax.broadcasted_iota(jnp.int32, (ncp, GQ), 0)
    t_c = i * QB + (lax.broadcasted_iota(jnp.int32, (ncp, GQ), 1) & (QB - 1))
    cmp_end = jnp.where(n_p >= CMP_PAD, (n_p - CMP_PAD) * CMP_STRIDE + (CMP_BLOCK - 1), 1 << 30)
    mask_c = cmp_end <= t_c
    lc = jnp.where(mask_c, lc_scr[...], NEG_INF)
    e = jnp.where(mask_c, jnp.exp2(lc - jnp.max(lc, axis=0, keepdims=True)), 0.0)
    pc = (e / jnp.maximum(jnp.sum(e, axis=0, keepdims=True), 1e-30)).astype(BF16)
    o_c = jnp.dot(vct_ref[0, 0], pc, preferred_element_type=F32)

    imp = jnp.zeros((n_sel, QB), F32)
    for r in range(NSA_G):
        imp = imp + jnp.dot(ovl_ref[...], pc[:, r * QB:(r + 1) * QB], preferred_element_type=F32)
    j_io = lax.broadcasted_iota(jnp.int32, (n_sel, QB), 0).astype(F32)
    cur = ((i * QB + lax.broadcasted_iota(jnp.int32, (n_sel, QB), 1))
           >> (SEL_BLOCK.bit_length() - 1)).astype(F32)
    imp = jnp.where((j_io == 0) | (j_io == cur), FORCED_SCORE, imp)
    imp = jnp.where(j_io <= cur, imp, NEG_INF)
    sel = jnp.zeros((n_sel, QB), F32)
    for _ in range(top_n):
        mx = jnp.max(imp, axis=0, keepdims=True)
        first = jnp.min(jnp.where(imp == mx, j_io, float(n_sel)), axis=0, keepdims=True)
        hit = j_io == first
        sel = jnp.where(hit, 1.0, sel)
        imp = jnp.where(hit, SEL_TAKEN, imp)
    sel_scr[...] = sel

    for m_scr, l_scr, a_scr in ((ms_scr, ls_scr, as_scr), (mw_scr, lw_scr, aw_scr)):
        m_scr[...] = jnp.full(m_scr.shape, NEG_INF, F32)
        l_scr[...] = jnp.zeros(l_scr.shape, F32)
        a_scr[...] = jnp.zeros(a_scr.shape, F32)
    s_loc = lax.broadcasted_iota(jnp.int32, (QB, QB), 0)
    t_loc = lax.broadcasted_iota(jnp.int32, (QB, QB), 1)
    blocks_per_tile = QB // SEL_BLOCK

    def slc_block(kb, carry):
        rows = pl.ds(pl.multiple_of(kb * KEY_BLOCK, KEY_BLOCK), KEY_BLOCK)
        masks = []
        for j in range(KEY_TILES):
            kt = kb * KEY_TILES + j
            picked = jnp.where(s_loc < SEL_BLOCK, sel_scr[pl.ds(kt * blocks_per_tile, 1), :],
                               sel_scr[pl.ds(kt * blocks_per_tile + 1, 1), :])
            dist = (t_loc + i * QB) - (s_loc + kt * QB)
            masks.append(jnp.where(dist >= 0, picked, 0.0) > 0.5)
        _attend_block(ks_ref[0, 0, rows, :], vst_ref[0, 0, :, rows], qall_scr, bias_ref,
                      i - kb * KEY_TILES, masks, ms_scr, ls_scr, as_scr)
        return carry

    lax.fori_loop(0, (i + KEY_TILES) // KEY_TILES, slc_block, 0)

    kt0 = jnp.maximum(i - WINDOW // QB, 0)
    rows = pl.ds(pl.multiple_of(kt0 * QB, QB), WIN_TILES * QB)
    masks = []
    for j in range(WIN_TILES):
        dist = (t_loc + i * QB) - (s_loc + (kt0 + j) * QB)
        masks.append((dist >= 0) & (dist < WINDOW))
    _attend_block(kw_ref[0, 0, rows, :], vwt_ref[0, 0, :, rows], qall_scr, bias_ref, i - kt0, masks,
                  mw_scr, lw_scr, aw_scr)

    o_s = as_scr[...] / jnp.maximum(ls_scr[...], 1e-30)
    o_w = aw_scr[...] / jnp.maximum(lw_scr[...], 1e-30)
    outs = []
    for r in range(NSA_G):
        ls = slice(r * QB, (r + 1) * QB)
        gate = lambda j: gt_scr[pl.ds(j * NSA_HEADS + g * NSA_G + r, 1), :]
        o_r = gate(0) * o_c[:, ls] + gate(1) * o_s[:, ls] + gate(2) * o_w[:, ls]
        outs.append(o_r.T)
    o_ref[...] = jnp.concatenate(outs, axis=1)


def nsa_attention(proj, kc, vct, ks, vst, kw, vwt, q_gain, bias, gcd, c31, ovl, B, S):
    nq = S // Q_BLOCK
    GQ = NSA_G * Q_BLOCK
    GW = NSA_G * HEAD_DIM
    ncp = kc.shape[2]
    n_sel = S // SEL_BLOCK
    const = lambda *shape: pl.BlockSpec(shape, lambda b, g, i: (0,) * len(shape), pipeline_mode=pl.Buffered(1))
    per_group = lambda *shape: pl.BlockSpec((1, 1) + shape, lambda b, g, i: (b, g, 0, 0))
    return pl.pallas_call(
        functools.partial(_nsa_body, n_sel=n_sel, top_n=min(SEL_TOPN, n_sel)),
        grid=(B, NSA_KV_HEADS, nq),
        in_specs=[pl.BlockSpec((Q_BLOCK, GW), lambda b, g, i: (b * nq + i, COL["n_q"] // GW + g)),
                  pl.BlockSpec((Q_BLOCK, 128), lambda b, g, i: (b * nq + i, COL["n_g"] // 128)),
                  per_group(ncp, HEAD_DIM), per_group(HEAD_DIM, ncp),
                  per_group(S, HEAD_DIM), per_group(HEAD_DIM, S),
                  per_group(S, HEAD_DIM), per_group(HEAD_DIM, S),
                  const(1, HEAD_DIM),
                  pl.BlockSpec((BIAS_TILES, Q_BLOCK, GQ), lambda b, g, i: (0, 0, g)),
                  pl.BlockSpec((1, CMP_NEAR, GQ), lambda b, g, i: (g, 0, 0)),
                  pl.BlockSpec((1, 1, GQ), lambda b, g, i: (g, 0, 0)),
                  const(n_sel, ncp)],
        out_specs=pl.BlockSpec((Q_BLOCK, GW), lambda b, g, i: (b * nq + i, g)),
        out_shape=jax.ShapeDtypeStruct((B * S, MIX_W), F32),
        scratch_shapes=[pltpu.VMEM((GQ, HEAD_DIM), BF16), pltpu.VMEM((ncp, GQ), F32),
                        pltpu.VMEM((n_sel, Q_BLOCK), F32), pltpu.VMEM((128, Q_BLOCK), F32)]
                       + [pltpu.VMEM((1, GQ), F32), pltpu.VMEM((1, GQ), F32), pltpu.VMEM((HEAD_DIM, GQ), F32)] * 2,
        compiler_params=_cparams(("parallel", "parallel", "arbitrary")),
        name="nsa_attention",
    )(proj, proj, kc, vct, ks, vst, kw, vwt, q_gain, bias, gcd, c31, ovl)


def nsa_tables(tbl, S):
    ncp = _cmp_rows(S)
    n_rel = jnp.arange(CMP_NEAR)[:, None] - CMP_PAD
    t_loc = jnp.arange(Q_BLOCK)[None, :]
    near = tbl[rel_bucket(t_loc - CMP_STRIDE * n_rel - (CMP_BLOCK - 1))] * LOG2E
    far = tbl[N_BUCKETS - 1] * LOG2E
    gcd = jnp.moveaxis(near - far, -1, 0).reshape(NSA_KV_HEADS, NSA_G, CMP_NEAR, Q_BLOCK)
    gcd = jnp.moveaxis(gcd, 1, 2).reshape(NSA_KV_HEADS, CMP_NEAR, NSA_G * Q_BLOCK)
    c31 = jnp.repeat(far, Q_BLOCK).reshape(NSA_KV_HEADS, 1, NSA_G * Q_BLOCK)
    n = np.arange(ncp) - CMP_PAD
    sel_start = np.arange(S // SEL_BLOCK) * SEL_BLOCK
    start = n * CMP_STRIDE
    ovl = ((start[None, :] < sel_start[:, None] + SEL_BLOCK) & (start[None, :] + CMP_BLOCK > sel_start[:, None])
           & (n[None, :] >= 0) & (n[None, :] < (S - CMP_BLOCK) // CMP_STRIDE + 1))
    return gcd, c31, jnp.asarray(ovl, BF16)


def nsa_mixer_pallas(proj, cmp_pe, cmp_w1, cmp_w2, q_gain, k_gain, rel_tbl, bias, B, S):
    dh = HEAD_DIM
    c0 = COL["n_kv"]
    xc = proj[:, c0:c0 + 2 * NSA_KV_HEADS * dh].reshape(B, S // CMP_STRIDE, CMP_STRIDE, 2, NSA_KV_HEADS, dh)
    xc = xc.transpose(0, 3, 4, 1, 2, 5).reshape(B, 2, NSA_KV_HEADS, S // CMP_STRIDE, CMP_STRIDE * dh)
    kc, vct = nsa_compress(xc, cmp_pe.reshape(2, 1, CMP_BLOCK * dh), cmp_w1.astype(BF16),
                           cmp_w2.astype(BF16), k_gain, S)
    ks, vst, kw, vwt = nsa_prep(proj, k_gain, B, S)
    gcd, c31, ovl = nsa_tables(rel_tbl, S)
    return nsa_attention(proj, kc, vct, ks, vst, kw, vwt, q_gain[None], bias, gcd, c31, ovl, B, S)


def rel_bucket(dist):
    n = jnp.maximum(dist, 0)
    max_exact = N_BUCKETS // 2
    nf = jnp.maximum(n, 1).astype(jnp.float32)
    large = max_exact + (jnp.log(nf / max_exact) / math.log(MAX_DISTANCE / max_exact)
                         * (N_BUCKETS - max_exact)).astype(jnp.int32)
    return jnp.where(n < max_exact, n, jnp.minimum(large, N_BUCKETS - 1))


CONV_HALO = 8


def _conv_body(b_ref, c_ref, x_ref, w_ref, o_ref, prev_scr):
    @pl.when(pl.program_id(1) == 0)
    def _():
        prev_scr[...] = jnp.zeros(prev_scr.shape, F32)

    bx = c_ref[...] * x_ref[...]
    ts = bx.shape[0]
    row = lax.broadcasted_iota(jnp.int32, bx.shape, 0)
    prev = prev_scr[...]
    last1 = prev[CONV_HALO - 1:CONV_HALO]
    last2 = prev[CONV_HALO - 2:CONV_HALO - 1]
    back1 = jnp.where(row == 0, last1, pltpu.roll(bx, 1, 0))
    back2 = jnp.where(row == 0, last2, jnp.where(row == 1, last1, pltpu.roll(bx, 2, 0)))
    w = w_ref[...]
    o_ref[...] = b_ref[...] * (w[0:1] * back2 + w[1:2] * back1 + w[2:3] * bx)
    prev_scr[...] = bx[ts - CONV_HALO:ts]


def conv_mixer(proj, conv_w, B, S, ts=512):
    nt = S // ts
    col = lambda name: pl.BlockSpec((ts, MIX_W), lambda b, t: (b * nt + t, COL[name] // MIX_W))
    return pl.pallas_call(
        _conv_body,
        grid=(B, nt),
        in_specs=[col("a_b"), col("a_c"), col("a_x"), pl.BlockSpec((CONV_K, MIX_W), lambda b, t: (0, 0))],
        out_specs=pl.BlockSpec((ts, MIX_W), lambda b, t: (b * nt + t, 0)),
        out_shape=jax.ShapeDtypeStruct((B * S, MIX_W), F32),
        scratch_shapes=[pltpu.VMEM((CONV_HALO, MIX_W), F32)],
        compiler_params=_cparams(("parallel", "arbitrary")),
        name="conv_mixer",
    )(proj, proj, proj, conv_w)


def _sgu_body(uv_ref, g_ref, w_ref, b_ref, o_ref):
    uv = jax.nn.gelu(uv_ref[...])
    u = uv[:, :MIX_W]
    v = uv[:, MIX_W:]
    v = v - jnp.mean(v, axis=-1, keepdims=True)
    v = (v * lax.rsqrt(jnp.mean(v * v, axis=-1, keepdims=True) + EPS) * g_ref[...]).astype(BF16)
    gw = MIX_W // SGU_GROUPS
    group = lax.broadcasted_iota(jnp.int32, (SGU_CHUNK, MIX_W), 1) >> (gw.bit_length() - 1)
    for c in range(uv.shape[0] // SGU_CHUNK):
        rows = slice(c * SGU_CHUNK, (c + 1) * SGU_CHUNK)
        vc = v[rows]
        s = b_ref[...]
        for g in range(SGU_GROUPS):
            s = s + jnp.dot(w_ref[g], jnp.where(group == g, vc, jnp.zeros_like(vc)),
                            preferred_element_type=F32)
        o_ref[rows, :] = u[rows] * s


def sgu_mixer_pallas(proj, ln_gain, w_s, b_s, T, tm=512):
    tri = jnp.tril(jnp.ones((SGU_CHUNK, SGU_CHUNK), dtype=bool))
    w = jnp.where(tri[None], w_s, 0).astype(BF16)
    b = jnp.repeat(b_s.T, MIX_W // SGU_GROUPS, axis=1)
    const = lambda *shape: pl.BlockSpec(shape, lambda i: (0,) * len(shape), pipeline_mode=pl.Buffered(1))
    return pl.pallas_call(
        _sgu_body,
        grid=(T // tm,),
        in_specs=[pl.BlockSpec((tm, 2 * MIX_W), lambda i: (i, COL["c_uv"] // (2 * MIX_W))),
                  const(1, MIX_W), const(SGU_GROUPS, SGU_CHUNK, SGU_CHUNK), const(SGU_CHUNK, MIX_W)],
        out_specs=pl.BlockSpec((tm, MIX_W), lambda i: (i, 0)),
        out_shape=jax.ShapeDtypeStruct((T, MIX_W), F32),
        compiler_params=_cparams(("parallel",)),
        name="sgu_mixer",
    )(proj, ln_gain[None], w, b)


def _layer(x2, p2, B, S, rel_bias, g_mix, w_in, conv_w, nsa_cmp_pe, nsa_cmp_w1, nsa_cmp_w2,
           nsa_q_gain, nsa_k_gain, sgu_ln_gain, sgu_w, sgu_b, dsa_q_gain, dsa_k_gain,
           w_gate, w_branch, w_out, g_ffn, peer_wq, peer_subkeys, peer_u, peer_v,
           g_ple, w_ple_gate, w_ple_proj):
    T = B * S
    proj = rms_matmul(x2, g_mix[None], pack_w_in(w_in), tm=1024, tn=768)

    y_a = conv_mixer(proj, conv_w, B, S)
    y_b = nsa_mixer_pallas(proj, nsa_cmp_pe, nsa_cmp_w1, nsa_cmp_w2, nsa_q_gain, nsa_k_gain,
                           rel_bias[:, :NSA_HEADS], rel_bias_tiles(rel_bias[:, :NSA_HEADS]), B, S)
    y_c = sgu_mixer_pallas(proj, sgu_ln_gain, sgu_w, sgu_b, T)
    y_d = dsa_attention(proj, COL, dsa_q_gain[None], dsa_k_gain[None],
                        rel_bias_tiles(rel_bias[:, NSA_HEADS:]), B, S)
    ys = [y_a, y_b, y_c, y_d]

    x2 = merge_mixers(x2, g_mix[None], ys, w_gate.astype(BF16),
                      w_branch.reshape(N_MIXERS, MIX_W, D_MODEL).astype(BF16),
                      w_out.astype(BF16), tm=256)

    sk = peer_subkeys.reshape(2 * PEER_HEADS, PEER_KEYS, PEER_QDIM // 2).astype(BF16)
    x2 = peer_ffn_update(x2, g_ffn[None], peer_wq.astype(BF16), sk,
                         peer_u.astype(BF16), peer_v.astype(BF16).T)

    x2 = ple_update(x2, g_ple[None], p2, w_ple_gate.astype(BF16), w_ple_proj.astype(BF16), tm=512)
    return x2


def kernel(x, p, rel_bias, g_mix, w_in, conv_w, nsa_cmp_pe, nsa_cmp_w1, nsa_cmp_w2, nsa_q_gain,
           nsa_k_gain, sgu_ln_gain, sgu_w, sgu_b, dsa_q_gain, dsa_k_gain, w_gate, w_branch, w_out,
           g_ffn, peer_wq, peer_subkeys, peer_u, peer_v, g_ple, w_ple_gate, w_ple_proj):
    B, S, D = x.shape
    depth = p.shape[0]
    x2 = x.reshape(B * S, D)
    for l in range(depth):
        x2 = _layer(x2, p[l].reshape(B * S, PLE_DIM), B, S, rel_bias, g_mix[l], w_in[l], conv_w[l],
                    nsa_cmp_pe[l], nsa_cmp_w1[l], nsa_cmp_w2[l], nsa_q_gain[l], nsa_k_gain[l],
                    sgu_ln_gain[l], sgu_w[l], sgu_b[l], dsa_q_gain[l], dsa_k_gain[l],
                    w_gate[l], w_branch[l], w_out[l], g_ffn[l], peer_wq[l], peer_subkeys[l],
                    peer_u[l], peer_v[l], g_ple[l], w_ple_gate[l], w_ple_proj[l])
    return x2.reshape(B, S, D)
```
